```python
import math
import jax, jax.numpy as jnp
from jax import lax
import numpy as np

D_MODEL = 2048
BATCH = 4
SEQ = 2048
DEPTH = 2
DEC_BATCH = 128
DEC_SEQ = 1
PAST_LEN = 16384
PAGE_SIZE = 128

N_MIXERS = 2
N_CONV_LAYERS = (DEPTH + 1) // 2
N_MLSTM_LAYERS = DEPTH // 2
D_CONV = D_MODEL
CONV_WIDTH = 31
N_HEADS = 8
D_QK = D_MODEL // 16
D_V = D_MODEL // 8
MLSTM_CHUNK = 128
MLSTM_SPLITS = [N_HEADS * D_QK, 2 * N_HEADS * D_QK, 2 * N_HEADS * D_QK + N_HEADS * D_V,
                2 * N_HEADS * D_QK + 2 * N_HEADS * D_V, 2 * N_HEADS * D_QK + 2 * N_HEADS * D_V + N_HEADS]
MLSTM_PROJ = 2 * N_HEADS * D_QK + 2 * N_HEADS * D_V + 2 * N_HEADS
D_FF = ((8 * D_MODEL // 3 + 255) // 256) * 256
FFN_CONV_WIDTH = 3
D_PLE = 256
ALPHA = (2.0 * DEPTH) ** 0.25
BETA = (8.0 * DEPTH) ** -0.25
LN_EPS = 1e-5

kernel_name = 'conformer_mlstm_convffn_deepnorm_step'


def layer_norm(x, g, b):
    xf = x.astype(jnp.float32)
    mu = xf.mean(-1, keepdims=True)
    var = jnp.mean(jnp.square(xf - mu), -1, keepdims=True)
    return ((xf - mu) * lax.rsqrt(var + LN_EPS) * g.astype(jnp.float32) + b.astype(jnp.float32)).astype(x.dtype)


def rms_norm(x, g):
    xf = x.astype(jnp.float32)
    return (xf * lax.rsqrt(jnp.mean(jnp.square(xf), -1, keepdims=True) + LN_EPS) * g.astype(jnp.float32)).astype(x.dtype)


def causal_dwconv(x, past, w, b):
    width, chans = w.shape
    xx = jnp.concatenate([past.astype(x.dtype), x], axis=1)
    y = lax.conv_general_dilated(xx, w.astype(xx.dtype)[:, None, :], window_strides=(1,), padding='VALID',
                                 dimension_numbers=('NWC', 'WIO', 'NWC'), feature_group_count=chans)
    return y + b, xx[:, xx.shape[1] - (width - 1):]


def conformer_conv(x, past, w_in, b_in, w_dw, b_dw, ln_g, ln_b, w_out, b_out):
    a, gt = jnp.split(x @ w_in + b_in, 2, axis=-1)
    u = a * jax.nn.sigmoid(gt)
    y, new_past = causal_dwconv(u, past, w_dw, b_dw)
    y = jax.nn.silu(layer_norm(y, ln_g, ln_b))
    return y @ w_out + b_out, new_past


def mlstm_chunk_scan(q, k, v, log_i, log_f, c0, n0, m0):
    bsz, slen = q.shape[0], q.shape[1]
    lch = math.gcd(slen, MLSTM_CHUNK)
    nch = slen // lch

    def to_chunks(t):
        return jnp.moveaxis(t.reshape((bsz, nch, lch) + t.shape[2:]), 1, 0)

    causal = jnp.tril(jnp.ones((lch, lch), dtype=bool))

    def step(carry, inp):
        c, n, m = carry
        qc, kc, vc, li, lf = inp
        bh = jnp.moveaxis(jnp.cumsum(lf, axis=1), 1, 2)
        lih = jnp.moveaxis(li, 1, 2)
        d = jnp.where(causal, bh[..., :, None] - bh[..., None, :] + lih[..., None, :], -jnp.inf)
        inter = bh + m[..., None]
        m_tok = jnp.maximum(inter, d.max(-1))
        w_intra = jnp.exp(d - m_tok[..., None])
        w_inter = jnp.exp(inter - m_tok)
        s = jnp.einsum('blhk,bshk->bhls', qc, kc) * w_intra
        num = (jnp.einsum('bhls,bshv->blhv', s, vc)
               + jnp.einsum('blhk,bhkv->blhv', qc, c) * jnp.moveaxis(w_inter, 1, 2)[..., None])
        den = s.sum(-1) + jnp.einsum('blhk,bhk->bhl', qc, n) * w_inter
        den = jnp.maximum(jnp.abs(den), jnp.exp(-m_tok))
        h = num / jnp.moveaxis(den, 1, 2)[..., None]
        m_new = m_tok[..., -1]
        g_state = jnp.exp(inter[..., -1] - m_new)
        g_tok = jnp.exp(bh[..., -1:] - bh + lih - m_new[..., None])
        c_new = g_state[..., None, None] * c + jnp.einsum('bhs,bshk,bshv->bhkv', g_tok, kc, vc)
        n_new = g_state[..., None] * n + jnp.einsum('bhs,bshk->bhk', g_tok, kc)
        return (c_new, n_new, m_new), h

    xs = (to_chunks(q), to_chunks(k), to_chunks(v), to_chunks(log_i), to_chunks(log_f))
    (c, n, m), hs = lax.scan(step, (c0, n0, m0), xs)
    h = jnp.moveaxis(hs, 0, 1).reshape(bsz, slen, N_HEADS, D_V)
    return h, c, n, m


def mlstm_mixer(x, c0, n0, m0, w_in, b_gates, hn_g, w_out, b_out):
    bsz, slen, _ = x.shape
    z = (x @ w_in).astype(jnp.float32)
    q, k, v, o, ig, fg = jnp.split(z, MLSTM_SPLITS, axis=-1)
    q = q.reshape(bsz, slen, N_HEADS, D_QK) * (D_QK ** -0.5)
    k = k.reshape(bsz, slen, N_HEADS, D_QK)
    v = v.reshape(bsz, slen, N_HEADS, D_V)
    bg = b_gates.astype(jnp.float32)
    log_i = ig + bg[0]
    log_f = jax.nn.log_sigmoid(fg + bg[1])
    h, c, n, m = mlstm_chunk_scan(q, k, v, log_i, log_f, c0.astype(jnp.float32),
                                  n0.astype(jnp.float32), m0.astype(jnp.float32))
    mu = h.mean(-1, keepdims=True)
    var = jnp.mean(jnp.square(h - mu), -1, keepdims=True)
    hn = (h - mu) * lax.rsqrt(var + LN_EPS) * hn_g.astype(jnp.float32).reshape(N_HEADS, D_V)
    out = (jax.nn.sigmoid(o) * hn.reshape(bsz, slen, N_HEADS * D_V)).astype(x.dtype)
    return out @ w_out + b_out, c, n, m


def conv_ffn(x, past, w_gate, w_up, w_dw, b_dw, w_down, b_down):
    g, new_past = causal_dwconv(x @ w_gate, past, w_dw, b_dw)
    return (jax.nn.silu(g) * (x @ w_up)) @ w_down + b_down, new_past


def trunk(x, p, conv_buf, c_st, n_st, m_st, ffn_buf,
          cv_w_in, cv_b_in, cv_w_dw, cv_b_dw, cv_ln_g, cv_ln_b, cv_w_out, cv_b_out,
          ml_w_in, ml_b_gates, ml_hn_g, ml_w_out, ml_b_out,
          ln_mix_g, ln_mix_b, ln_ffn_g, ln_ffn_b,
          ff_w_gate, ff_w_up, ff_w_dw, ff_b_dw, ff_w_down, ff_b_down,
          pl_w_proj, pl_g, pl_w_gate):
    new_conv, new_c, new_n, new_m, new_ffn = [], [], [], [], []
    for i in range(DEPTH):
        j = i // N_MIXERS
        if i % N_MIXERS == 0:
            mix, buf = conformer_conv(x, conv_buf[j], cv_w_in[j], cv_b_in[j], cv_w_dw[j], cv_b_dw[j],
                                      cv_ln_g[j], cv_ln_b[j], cv_w_out[j], cv_b_out[j])
            new_conv.append(buf.astype(conv_buf.dtype))
        else:
            mix, c, n, m = mlstm_mixer(x, c_st[j], n_st[j], m_st[j], ml_w_in[j], ml_b_gates[j],
                                       ml_hn_g[j], ml_w_out[j], ml_b_out[j])
            new_c.append(c.astype(c_st.dtype))
            new_n.append(n.astype(n_st.dtype))
            new_m.append(m.astype(m_st.dtype))
        x = layer_norm(ALPHA * x + mix.astype(x.dtype), ln_mix_g[i], ln_mix_b[i])
        f, fbuf = conv_ffn(x, ffn_buf[i], ff_w_gate[i], ff_w_up[i], ff_w_dw[i], ff_b_dw[i],
                           ff_w_down[i], ff_b_down[i])
        new_ffn.append(fbuf.astype(ffn_buf.dtype))
        x = layer_norm(ALPHA * x + f.astype(x.dtype), ln_ffn_g[i], ln_ffn_b[i])
        x = x + jax.nn.sigmoid(x @ pl_w_gate[i]) * rms_norm(p[i] @ pl_w_proj[i], pl_g[i])
    return (x, jnp.stack(new_conv), jnp.stack(new_c), jnp.stack(new_n), jnp.stack(new_m), jnp.stack(new_ffn))


def setup_inputs(seed: int = 0) -> dict:
    key = jax.random.key(seed)
    ks = iter(jax.random.split(key, 48))

    def nrm(shape, scale):
        return jax.random.normal(next(ks), shape, jnp.float32) * scale

    na, nb = N_CONV_LAYERS, N_MLSTM_LAYERS
    f_bias = jnp.broadcast_to(jnp.linspace(3.0, 6.0, N_HEADS, dtype=jnp.float32), (nb, N_HEADS))
    return {
        'x_prompt': nrm((BATCH, SEQ, D_MODEL), 1.0),
        'x_sample': nrm((DEC_BATCH, DEC_SEQ, D_MODEL), 1.0),
        'p_prompt': nrm((DEPTH, BATCH, SEQ, D_PLE), 1.0),
        'p_sample': nrm((DEPTH, DEC_BATCH, DEC_SEQ, D_PLE), 1.0),
        'state_conv': nrm((na, DEC_BATCH, CONV_WIDTH - 1, D_CONV), 0.5),
        'state_mlstm_c': nrm((nb, DEC_BATCH, N_HEADS, D_QK, D_V), 1.0),
        'state_mlstm_n': nrm((nb, DEC_BATCH, N_HEADS, D_QK), 1.0),
        'state_mlstm_m': nrm((nb, DEC_BATCH, N_HEADS), 1.0),
        'state_ffn_conv': nrm((DEPTH, DEC_BATCH, FFN_CONV_WIDTH - 1, D_FF), 1.0),
        'cv_w_in': nrm((na, D_MODEL, 2 * D_CONV), D_MODEL ** -0.5),
        'cv_b_in': nrm((na, 2 * D_CONV), 0.02),
        'cv_w_dw': nrm((na, CONV_WIDTH, D_CONV), CONV_WIDTH ** -0.5),
        'cv_b_dw': nrm((na, D_CONV), 0.02),
        'cv_ln_g': 1.0 + nrm((na, D_CONV), 0.02),
        'cv_ln_b': nrm((na, D_CONV), 0.02),
        'cv_w_out': nrm((na, D_CONV, D_MODEL), BETA * D_CONV ** -0.5),
        'cv_b_out': nrm((na, D_MODEL), 0.02),
        'ml_w_in': nrm((nb, D_MODEL, MLSTM_PROJ), D_MODEL ** -0.5),
        'ml_b_gates': jnp.stack([nrm((nb, N_HEADS), 0.1), f_bias + nrm((nb, N_HEADS), 0.1)], axis=1),
        'ml_hn_g': 1.0 + nrm((nb, N_HEADS * D_V), 0.02),
        'ml_w_out': nrm((nb, N_HEADS * D_V, D_MODEL), BETA * (N_HEADS * D_V) ** -0.5),
        'ml_b_out': nrm((nb, D_MODEL), 0.02),
        'ln_mix_g': 1.0 + nrm((DEPTH, D_MODEL), 0.02),
        'ln_mix_b': nrm((DEPTH, D_MODEL), 0.02),
        'ln_ffn_g': 1.0 + nrm((DEPTH, D_MODEL), 0.02),
        'ln_ffn_b': nrm((DEPTH, D_MODEL), 0.02),
        'ff_w_gate': nrm((DEPTH, D_MODEL, D_FF), D_MODEL ** -0.5),
        'ff_w_up': nrm((DEPTH, D_MODEL, D_FF), D_MODEL ** -0.5),
        'ff_w_dw': nrm((DEPTH, FFN_CONV_WIDTH, D_FF), FFN_CONV_WIDTH ** -0.5),
        'ff_b_dw': nrm((DEPTH, D_FF), 0.02),
        'ff_w_down': nrm((DEPTH, D_FF, D_MODEL), BETA * D_FF ** -0.5),
        'ff_b_down': nrm((DEPTH, D_MODEL), 0.02),
        'pl_w_proj': nrm((DEPTH, D_PLE, D_MODEL), D_PLE ** -0.5),
        'pl_g': 1.0 + nrm((DEPTH, D_MODEL), 0.02),
        'pl_w_gate': nrm((DEPTH, D_MODEL, D_MODEL), D_MODEL ** -0.5),
    }


def reference(x_prompt, x_sample, p_prompt, p_sample, state_conv, state_mlstm_c, state_mlstm_n,
              state_mlstm_m, state_ffn_conv,
              cv_w_in, cv_b_in, cv_w_dw, cv_b_dw, cv_ln_g, cv_ln_b, cv_w_out, cv_b_out,
              ml_w_in, ml_b_gates, ml_hn_g, ml_w_out, ml_b_out,
              ln_mix_g, ln_mix_b, ln_ffn_g, ln_ffn_b,
              ff_w_gate, ff_w_up, ff_w_dw, ff_b_dw, ff_w_down, ff_b_down,
              pl_w_proj, pl_g, pl_w_gate):
    weights = (cv_w_in, cv_b_in, cv_w_dw, cv_b_dw, cv_ln_g, cv_ln_b, cv_w_out, cv_b_out,
               ml_w_in, ml_b_gates, ml_hn_g, ml_w_out, ml_b_out,
               ln_mix_g, ln_mix_b, ln_ffn_g, ln_ffn_b,
               ff_w_gate, ff_w_up, ff_w_dw, ff_b_dw, ff_w_down, ff_b_down,
               pl_w_proj, pl_g, pl_w_gate)
    bp = x_prompt.shape[0]
    dt = x_prompt.dtype
    z_conv = jnp.zeros((N_CONV_LAYERS, bp, CONV_WIDTH - 1, D_CONV), dt)
    z_c = jnp.zeros((N_MLSTM_LAYERS, bp, N_HEADS, D_QK, D_V), dt)
    z_n = jnp.zeros((N_MLSTM_LAYERS, bp, N_HEADS, D_QK), dt)
    z_m = jnp.zeros((N_MLSTM_LAYERS, bp, N_HEADS), dt)
    z_ffn = jnp.zeros((DEPTH, bp, FFN_CONV_WIDTH - 1, D_FF), dt)
    y_prompt, conv_p, c_p, n_p, m_p, ffn_p = trunk(x_prompt, p_prompt, z_conv, z_c, z_n, z_m, z_ffn, *weights)
    y_sample, conv_s, c_s, n_s, m_s, ffn_s = trunk(x_sample, p_sample, state_conv, state_mlstm_c, state_mlstm_n,
                                                   state_mlstm_m, state_ffn_conv, *weights)
    return (y_prompt, y_sample, conv_p, conv_s, c_p, n_p, m_p, c_s, n_s, m_s, ffn_p, ffn_s)
```

```python
import functools

import jax
import jax.numpy as jnp
from jax import lax
from jax.experimental import pallas as pl
from jax.experimental.pallas import tpu as pltpu

D_MODEL = 2048
D_FF = 5632
D_PLE = 256
N_HEADS = 8
D_QK = 128
D_V = 256
CONV_WIDTH = 31
FFN_CONV_WIDTH = 3
DEPTH = 2
ALPHA = (2.0 * DEPTH) ** 0.25
LN_EPS = 1e-5
MLSTM_CHUNK = 128
GATE_PAD = 128

F32 = jnp.float32
BF16 = jnp.bfloat16

VMEM_CAP_BYTES = 58 * 1024 * 1024


def _params(sem, vmem_bytes):
    return pltpu.CompilerParams(dimension_semantics=sem,
                                vmem_limit_bytes=min(int(vmem_bytes), VMEM_CAP_BYTES))


def _ln(x, g, b):
    mu = jnp.mean(x, axis=-1, keepdims=True)
    xc = x - mu
    var = jnp.mean(xc * xc, axis=-1, keepdims=True)
    return xc * lax.rsqrt(var + LN_EPS) * g + b


def _sigmoid(x):
    return 1.0 / (1.0 + jnp.exp(-x))


def _log_sigmoid(x):
    return jnp.minimum(x, 0.0) - jnp.log(1.0 + jnp.exp(-jnp.abs(x)))


def _resident(shape):
    nd = len(shape)
    return pl.BlockSpec(shape, lambda *_: (0,) * nd, pipeline_mode=pl.Buffered(1))


def _mm_glu_body(x_ref, wa_ref, wg_ref, ba_ref, bg_ref, u_ref):
    xb = x_ref[...].astype(BF16)
    a = jnp.dot(xb, wa_ref[...], preferred_element_type=F32) + ba_ref[...]
    g = jnp.dot(xb, wg_ref[...], preferred_element_type=F32) + bg_ref[...]
    u_ref[...] = a * _sigmoid(g)


def mm_glu(x, w, b, tm, tn):
    m, k = x.shape
    n = w.shape[1] // 2
    nj = n // tn
    b2 = b.reshape(1, 2 * n)
    vmem = 2 * tm * k * 4 + 4 * k * tn * 2 + 2 * tm * tn * 4 + tm * k * 2 + 6 * tm * tn * 4
    return pl.pallas_call(
        _mm_glu_body,
        grid=(nj, m // tm),
        in_specs=[pl.BlockSpec((tm, k), lambda j, i: (i, 0)),
                  pl.BlockSpec((k, tn), lambda j, i: (0, j)),
                  pl.BlockSpec((k, tn), lambda j, i: (0, j + nj)),
                  pl.BlockSpec((1, tn), lambda j, i: (0, j)),
                  pl.BlockSpec((1, tn), lambda j, i: (0, j + nj))],
        out_specs=pl.BlockSpec((tm, tn), lambda j, i: (i, j)),
        out_shape=jax.ShapeDtypeStruct((m, n), F32),
        compiler_params=_params(("arbitrary", "arbitrary"), vmem + (8 << 20)),
        name="mm_glu",
    )(x, w, w, b2, b2)


SUBLANES = 8
CONV_ROWS = 64
CONV_HALO = 32


def _conv_prompt_body(u_ref, w_ref, b_ref, y_ref, pad_ref):
    s, tc = u_ref.shape
    pad_ref[pl.ds(0, CONV_HALO), :] = jnp.zeros((CONV_HALO, tc), F32)
    pad_ref[pl.ds(CONV_HALO, s), :] = u_ref[...]
    win = CONV_ROWS + SUBLANES

    def chunk(i, carry):
        r0 = pl.multiple_of(i * CONV_ROWS, CONV_ROWS)
        acc = jnp.broadcast_to(b_ref[...], (CONV_ROWS, tc))
        for r in range(SUBLANES):
            part = None
            for a in range((CONV_WIDTH - 1 - r) // SUBLANES + 1):
                d = SUBLANES * a + r
                start = pl.multiple_of(r0 + CONV_HALO - SUBLANES * (a + 1), SUBLANES)
                term = pad_ref[pl.ds(start, win), :] * w_ref[pl.ds(CONV_WIDTH - 1 - d, 1), :]
                part = term if part is None else part + term
            acc = acc + part[SUBLANES - r:SUBLANES - r + CONV_ROWS, :]
        y_ref[pl.ds(r0, CONV_ROWS), :] = acc
        return carry

    lax.fori_loop(0, s // CONV_ROWS, chunk, 0)


def conv_prompt(u, w, b, tc):
    bsz, s, c = u.shape
    return pl.pallas_call(
        _conv_prompt_body,
        grid=(bsz, c // tc),
        in_specs=[pl.BlockSpec((None, s, tc), lambda bi, ci: (bi, 0, ci)),
                  pl.BlockSpec((CONV_WIDTH, tc), lambda bi, ci: (0, ci)),
                  pl.BlockSpec((1, tc), lambda bi, ci: (0, ci))],
        out_specs=pl.BlockSpec((None, s, tc), lambda bi, ci: (bi, 0, ci)),
        out_shape=jax.ShapeDtypeStruct((bsz, s, c), F32),
        scratch_shapes=[pltpu.VMEM((s + CONV_HALO, tc), F32)],
        compiler_params=_params(("arbitrary", "arbitrary"), 5 * s * tc * 4 + (8 << 20)),
        name="conv_prompt",
    )(u, w, b.reshape(1, c))


def _conv_sample_body(p_ref, u_ref, w_ref, b_ref, y_ref):
    hist = CONV_WIDTH - 1
    acc = jnp.sum(p_ref[...] * w_ref[pl.ds(0, hist), :][None], axis=1)
    y_ref[...] = acc + u_ref[...] * w_ref[pl.ds(hist, 1), :] + b_ref[...]


def conv_sample(past, u, w, b, bb):
    bsz, hist, c = past.shape
    return pl.pallas_call(
        _conv_sample_body,
        grid=(bsz // bb,),
        in_specs=[pl.BlockSpec((bb, hist, c), lambda i: (i, 0, 0)),
                  pl.BlockSpec((bb, c), lambda i: (i, 0)),
                  pl.BlockSpec((CONV_WIDTH, c), lambda i: (0, 0)),
                  pl.BlockSpec((1, c), lambda i: (0, 0))],
        out_specs=pl.BlockSpec((bb, c), lambda i: (i, 0)),
        out_shape=jax.ShapeDtypeStruct((bsz, c), F32),
        compiler_params=_params(("arbitrary",), 6 * bb * 32 * c * 4 + (8 << 20)),
        name="conv_sample",
    )(past, u, w, b.reshape(1, c))


def _mm_ln_body(pre_norm, *refs):
    if pre_norm:
        xin_ref, res_ref, w_ref, b_ref, pg_ref, pb_ref, g_ref, be_ref, o_ref = refs
        t = _ln(xin_ref[...].astype(F32), pg_ref[...], pb_ref[...])
        xb = (t * _sigmoid(t)).astype(BF16)
    else:
        xin_ref, res_ref, w_ref, b_ref, g_ref, be_ref, o_ref = refs
        xb = xin_ref[...].astype(BF16)
    mix = jnp.dot(xb, w_ref[...], preferred_element_type=F32) + b_ref[...]
    o_ref[...] = _ln(ALPHA * res_ref[...] + mix, g_ref[...], be_ref[...])


def mm_ln(xin, resid, w, b, g, be, tm, pre=None):
    m, k = xin.shape
    n = w.shape[1]
    row = lambda i: (i, 0)
    vec = lambda v: v.reshape(1, -1)
    in_specs = [pl.BlockSpec((tm, k), row), pl.BlockSpec((tm, n), row),
                _resident((k, n)), _resident((1, n))]
    args = [xin, resid, w, vec(b)]
    if pre is not None:
        in_specs += [_resident((1, k)), _resident((1, k))]
        args += [vec(pre[0]), vec(pre[1])]
    in_specs += [_resident((1, n)), _resident((1, n))]
    args += [vec(g), vec(be)]
    vmem = 2 * tm * k * xin.dtype.itemsize + 4 * tm * n * 4 + k * n * 2 + 6 * tm * n * 4
    return pl.pallas_call(
        functools.partial(_mm_ln_body, pre is not None),
        grid=(m // tm,),
        in_specs=in_specs,
        out_specs=pl.BlockSpec((tm, n), row),
        out_shape=jax.ShapeDtypeStruct((m, n), F32),
        compiler_params=_params(("arbitrary",), vmem + (8 << 20)),
        name="mm_ln",
    )(*args)


FFN_TAIL = 8


def _ffn_body(fresh_seq, tiles_per_seq, *refs):
    if fresh_seq:
        (x_ref, wg_ref, wu_ref, wd_ref, wdw_ref, bdw_ref, bd_ref, lg_ref, lb_ref,
         o_ref, gt_ref, xb_ref, gs_ref, carry_ref) = refs
    else:
        (x_ref, p2_ref, p1_ref, wg_ref, wu_ref, wd_ref, wdw_ref, bdw_ref, bd_ref, lg_ref, lb_ref,
         o_ref, gt_ref, xb_ref) = refs
    i = pl.program_id(0)
    f = pl.program_id(1)
    tm = x_ref.shape[0]

    @pl.when(f == 0)
    def _():
        xb_ref[...] = x_ref[...].astype(BF16)
        o_ref[...] = jnp.zeros(o_ref.shape, F32)

    xb = xb_ref[...]
    g = jnp.dot(xb, wg_ref[...], preferred_element_type=F32)
    up = jnp.dot(xb, wu_ref[...], preferred_element_type=F32)
    if fresh_seq:
        gs_ref[pl.ds(FFN_TAIL, tm), :] = g
        seq_start = (i % tiles_per_seq) == 0

        @pl.when(seq_start)
        def _():
            gs_ref[pl.ds(0, FFN_TAIL), :] = jnp.zeros((FFN_TAIL, g.shape[1]), F32)

        @pl.when(jnp.logical_not(seq_start))
        def _():
            gs_ref[pl.ds(0, FFN_TAIL), :] = carry_ref[f]

        g1 = gs_ref[pl.ds(FFN_TAIL - 1, tm), :]
        g2 = gs_ref[pl.ds(FFN_TAIL - 2, tm), :]
        tail = gs_ref[pl.ds(tm, FFN_TAIL), :]
        carry_ref[f] = tail
        gt_ref[...] = tail
    else:
        g1 = p1_ref[...]
        g2 = p2_ref[...]
        gt_ref[...] = g
    gc = (wdw_ref[pl.ds(0, 1), :] * g2 + wdw_ref[pl.ds(1, 1), :] * g1
          + wdw_ref[pl.ds(2, 1), :] * g + bdw_ref[...])
    h = (gc * _sigmoid(gc) * up).astype(BF16)
    o_ref[...] += jnp.dot(h, wd_ref[...], preferred_element_type=F32)

    @pl.when(f == pl.num_programs(1) - 1)
    def _():
        o_ref[...] = _ln(ALPHA * x_ref[...] + o_ref[...] + bd_ref[...], lg_ref[...], lb_ref[...])


def ffn(x, past, wg, wu, wd, wdw, bdw, bd, lg, lb, tm, tf, seq_len):
    m, k = x.shape
    dff = wg.shape[1]
    nf = dff // tf
    fresh = past is None
    vec = lambda v: v.reshape(1, -1)
    in_specs = [pl.BlockSpec((tm, k), lambda i, f: (i, 0))]
    args = [x]
    if not fresh:
        in_specs += [pl.BlockSpec((tm, tf), lambda i, f: (i, f)),
                     pl.BlockSpec((tm, tf), lambda i, f: (i, f + nf))]
        args += [past, past]
    in_specs += [pl.BlockSpec((k, tf), lambda i, f: (0, f)),
                 pl.BlockSpec((k, tf), lambda i, f: (0, f)),
                 pl.BlockSpec((tf, k), lambda i, f: (f, 0)),
                 pl.BlockSpec((FFN_CONV_WIDTH, tf), lambda i, f: (0, f)),
                 pl.BlockSpec((1, tf), lambda i, f: (0, f)),
                 _resident((1, k)), _resident((1, k)), _resident((1, k))]
    args += [wg, wu, wd, wdw, vec(bdw), vec(bd), vec(lg), vec(lb)]
    tail_rows = FFN_TAIL if fresh else tm
    scratch = [pltpu.VMEM((tm, k), BF16)]
    if fresh:
        scratch += [pltpu.VMEM((tm + FFN_TAIL, tf), F32), pltpu.VMEM((nf, FFN_TAIL, tf), F32)]
    vmem = 4 * tm * k * 4 + tm * k * 2 + 12 * k * tf + 10 * tm * tf * 4 + nf * FFN_TAIL * tf * 4
    out, gt = pl.pallas_call(
        functools.partial(_ffn_body, fresh, max(seq_len // tm, 1)),
        grid=(m // tm, nf),
        in_specs=in_specs,
        out_specs=[pl.BlockSpec((tm, k), lambda i, f: (i, 0)),
                   pl.BlockSpec((None, tail_rows, tf), lambda i, f: (i, 0, f))],
        out_shape=[jax.ShapeDtypeStruct((m, k), F32),
                   jax.ShapeDtypeStruct((m // tm, tail_rows, dff), F32)],
        scratch_shapes=scratch,
        compiler_params=_params(("arbitrary", "arbitrary"), vmem + (8 << 20)),
        name="ffn",
    )(*args)
    return out, gt


def _ple_body(x_ref, p_ref, wg_ref, wp_ref, g_ref, o_ref):
    x = x_ref[...]
    gate = _sigmoid(jnp.dot(x.astype(BF16), wg_ref[...], preferred_element_type=F32))
    e = jnp.dot(p_ref[...].astype(BF16), wp_ref[...], preferred_element_type=F32)
    e = e * lax.rsqrt(jnp.mean(e * e, axis=-1, keepdims=True) + LN_EPS) * g_ref[...]
    o_ref[...] = x + gate * e


def ple(x, p, wg, wp, g, tm):
    m, k = x.shape
    dp = p.shape[1]
    row = lambda i: (i, 0)
    vmem = 4 * tm * k * 4 + 2 * tm * dp * 4 + k * k * 2 + dp * k * 2 + 6 * tm * k * 4
    return pl.pallas_call(
        _ple_body,
        grid=(m // tm,),
        in_specs=[pl.BlockSpec((tm, k), row), pl.BlockSpec((tm, dp), row),
                  _resident((k, k)), _resident((dp, k)), _resident((1, k))],
        out_specs=pl.BlockSpec((tm, k), row),
        out_shape=jax.ShapeDtypeStruct((m, k), F32),
        compiler_params=_params(("arbitrary",), vmem + (8 << 20)),
        name="ple",
    )(x, p, wg, wp, g.reshape(1, k))


def _mm_qkv_body(q_tiles, x_ref, w_ref, o_ref):
    z = jnp.dot(x_ref[...].astype(BF16), w_ref[...], preferred_element_type=F32)
    scale = jnp.where(pl.program_id(0) < q_tiles, D_QK ** -0.5, 1.0).astype(F32)
    o_ref[...] = (z * scale).astype(BF16)


def mm_qkv(x, w, tm, tn):
    m, k = x.shape
    n = w.shape[1]
    vmem = 2 * tm * k * 4 + 4 * k * tn + 4 * tm * tn + tm * k * 2 + 3 * tm * tn * 4
    return pl.pallas_call(
        functools.partial(_mm_qkv_body, (N_HEADS * D_QK) // tn),
        grid=(n // tn, m // tm),
        in_specs=[pl.BlockSpec((tm, k), lambda j, i: (i, 0)),
                  pl.BlockSpec((k, tn), lambda j, i: (0, j))],
        out_specs=pl.BlockSpec((tm, tn), lambda j, i: (i, j)),
        out_shape=jax.ShapeDtypeStruct((m, n), BF16),
        compiler_params=_params(("arbitrary", "arbitrary"), vmem + (8 << 20)),
        name="mm_qkv",
    )(x, w)


def _mm_o_gates_body(x_ref, wo_ref, wgt_ref, o_ref, gt_ref):
    xb = x_ref[...].astype(BF16)
    o_ref[...] = jnp.dot(xb, wo_ref[...], preferred_element_type=F32)
    gt_ref[...] = jnp.dot(xb, wgt_ref[...], preferred_element_type=F32)


def mm_o_gates(x, wo, wgt, tm):
    m, k = x.shape
    n = wo.shape[1]
    row = lambda i: (i, 0)
    vmem = 2 * tm * k * 4 + k * n * 2 + k * GATE_PAD * 2 + 2 * tm * (n + GATE_PAD) * 4 + tm * k * 2 + 2 * tm * n * 4
    return pl.pallas_call(
        _mm_o_gates_body,
        grid=(m // tm,),
        in_specs=[pl.BlockSpec((tm, k), row), _resident((k, n)), _resident((k, GATE_PAD))],
        out_specs=[pl.BlockSpec((tm, n), row), pl.BlockSpec((tm, GATE_PAD), row)],
        out_shape=[jax.ShapeDtypeStruct((m, n), F32), jax.ShapeDtypeStruct((m, GATE_PAD), F32)],
        compiler_params=_params(("arbitrary",), vmem + (8 << 20)),
        name="mm_o_gates",
    )(x, wo, wgt)


def _mlstm_chunk_body(q_ref, k_ref, v_ref, o_ref, gc_ref, gr_ref, brow_ref, bcol_ref, hng_ref,
                      out_ref, c_ref, n_ref, m_ref):
    length = q_ref.shape[0]

    @pl.when(pl.program_id(1) == 0)
    def _():
        c_ref[...] = jnp.zeros(c_ref.shape, F32)
        n_ref[...] = jnp.zeros(n_ref.shape, F32)
        m_ref[...] = jnp.zeros(m_ref.shape, F32)

    row = lax.broadcasted_iota(jnp.int32, (length, length), 0)
    col = lax.broadcasted_iota(jnp.int32, (length, length), 1)
    causal = row >= col
    tril = causal.astype(F32)
    triu = (row <= col).astype(F32)

    gcol = gc_ref[...] + brow_ref[...]
    bh_col = jnp.dot(tril, _log_sigmoid(gcol), precision=lax.Precision.HIGHEST,
                     preferred_element_type=F32)
    grow = gr_ref[...] + bcol_ref[...]
    li_row = grow[0:N_HEADS, :]
    bh_row = jnp.dot(_log_sigmoid(grow[N_HEADS:2 * N_HEADS, :]), triu,
                     precision=lax.Precision.HIGHEST, preferred_element_type=F32)
    a_row = li_row - bh_row

    for h in range(N_HEADS):
        qk = slice(h * D_QK, (h + 1) * D_QK)
        vs = slice(h * D_V, (h + 1) * D_V)
        bh_c = bh_col[:, N_HEADS + h:N_HEADS + h + 1]
        li_c = gcol[:, h:h + 1]
        m_old = m_ref[h][0:1, 0:1]
        d = jnp.where(causal, bh_c + a_row[h:h + 1, :], -jnp.inf)
        inter = bh_c + m_old
        m_tok = jnp.maximum(inter, jnp.max(d, axis=1, keepdims=True))
        w_intra = jnp.exp(d - m_tok)
        w_inter = jnp.exp(inter - m_tok)
        qh = q_ref[:, qk]
        kh = k_ref[:, qk]
        vh = v_ref[:, vs]
        c_old = c_ref[h]
        n_old = n_ref[h][0:1, :]
        s = lax.dot_general(qh, kh, (((1,), (1,)), ((), ())), preferred_element_type=F32) * w_intra
        num = (jnp.dot(s.astype(BF16), vh, preferred_element_type=F32)
               + jnp.dot(qh, c_old.astype(BF16), preferred_element_type=F32) * w_inter)
        qn = jnp.sum(qh.astype(F32) * n_old, axis=1, keepdims=True)
        den = jnp.sum(s, axis=1, keepdims=True) + qn * w_inter
        den = jnp.maximum(jnp.abs(den), jnp.exp(-m_tok))
        hv = num * (1.0 / den)
        mu = jnp.mean(hv, axis=1, keepdims=True)
        hc = hv - mu
        var = jnp.mean(hc * hc, axis=1, keepdims=True)
        hn = hc * lax.rsqrt(var + LN_EPS) * hng_ref[:, vs]
        out_ref[:, vs] = (_sigmoid(o_ref[:, vs]) * hn).astype(BF16)
        m_new = m_tok[length - 1:length, :]
        g_state = jnp.exp(inter[length - 1:length, :] - m_new)
        g_tok = jnp.exp(bh_c[length - 1:length, :] - bh_c + li_c - m_new)
        kg = kh.astype(F32) * g_tok
        c_ref[h] = g_state * c_old + lax.dot_general(
            kg.astype(BF16), vh, (((0,), (0,)), ((), ())), preferred_element_type=F32)
        n_new = g_state * n_old + jnp.sum(kg, axis=0, keepdims=True)
        n_ref[h] = jnp.broadcast_to(n_new, n_ref.shape[1:])
        m_ref[h] = jnp.broadcast_to(m_new, m_ref.shape[1:])


def mlstm_chunk(qkv, o, gates, gates_t, b_gates, hn_g, bsz, seq_len):
    length = MLSTM_CHUNK
    nch = seq_len // length
    hq = N_HEADS * D_QK
    hv = N_HEADS * D_V
    rows = lambda b, c: (b * nch + c, 0)
    brow = jnp.pad(b_gates.reshape(1, 2 * N_HEADS), ((0, 0), (0, GATE_PAD - 2 * N_HEADS)))
    bcol = b_gates.reshape(2 * N_HEADS, 1)
    state = lambda b, c: (b, 0, 0, 0)
    out, c_new, n_new, m_new = pl.pallas_call(
        _mlstm_chunk_body,
        grid=(bsz, nch),
        in_specs=[pl.BlockSpec((length, hq), rows),
                  pl.BlockSpec((length, hq), lambda b, c: (b * nch + c, 1)),
                  pl.BlockSpec((length, hv), lambda b, c: (b * nch + c, 1)),
                  pl.BlockSpec((length, hv), rows),
                  pl.BlockSpec((length, GATE_PAD), rows),
                  pl.BlockSpec((2 * N_HEADS, length), lambda b, c: (0, b * nch + c)),
                  pl.BlockSpec((1, GATE_PAD), lambda b, c: (0, 0)),
                  pl.BlockSpec((2 * N_HEADS, 1), lambda b, c: (0, 0)),
                  pl.BlockSpec((1, hv), lambda b, c: (0, 0))],
        out_specs=[pl.BlockSpec((length, hv), rows),
                   pl.BlockSpec((None, N_HEADS, D_QK, D_V), state),
                   pl.BlockSpec((None, N_HEADS, 8, D_QK), state),
                   pl.BlockSpec((None, N_HEADS, 8, 128), state)],
        out_shape=[jax.ShapeDtypeStruct((bsz * seq_len, hv), BF16),
                   jax.ShapeDtypeStruct((bsz, N_HEADS, D_QK, D_V), F32),
                   jax.ShapeDtypeStruct((bsz, N_HEADS, 8, D_QK), F32),
                   jax.ShapeDtypeStruct((bsz, N_HEADS, 8, 128), F32)],
        compiler_params=_params(("arbitrary", "arbitrary"), 32 << 20),
        name="mlstm_chunk",
    )(qkv, qkv, qkv, o, gates, gates_t, brow, bcol, hn_g.reshape(1, hv))
    return out, c_new, n_new[:, :, 0, :], m_new[:, :, 0, 0]


def _mlstm_step_body(q_ref, k_ref, qt_ref, kt_ref, v_ref, o_ref, g_ref, c0_ref, n0_ref, m0_ref,
                     bg_ref, hng_ref, out_ref, c_ref, n_ref, m_ref):
    bb = q_ref.shape[0]
    for b in range(bb):
        for h in range(N_HEADS):
            vs = slice(h * D_V, (h + 1) * D_V)
            li = g_ref[b, h:h + 1, :] + bg_ref[h:h + 1, :]
            lf = _log_sigmoid(g_ref[b, N_HEADS + h:N_HEADS + h + 1, :]
                              + bg_ref[N_HEADS + h:N_HEADS + h + 1, :])
            inter = lf + m0_ref[b, h:h + 1, :]
            m_tok = jnp.maximum(inter, li)
            w_intra = jnp.exp(li - m_tok)
            w_inter = jnp.exp(inter - m_tok)
            q_row = q_ref[b, h:h + 1, :]
            k_row = k_ref[b, h:h + 1, :]
            q_col = qt_ref[b, :, h:h + 1]
            k_col = kt_ref[b, :, h:h + 1]
            v_row = v_ref[b, :, vs]
            n_old = n0_ref[b, h:h + 1, :]
            c_old = c0_ref[b, h]
            s = jnp.sum(q_row * k_row, axis=1, keepdims=True) * w_intra
            qc = jnp.sum(q_col * c_old, axis=0, keepdims=True)
            num = s * v_row + qc * w_inter
            den = s + jnp.sum(q_row * n_old, axis=1, keepdims=True) * w_inter
            den = jnp.maximum(jnp.abs(den), jnp.exp(-m_tok))
            hv = num * (1.0 / den)
            mu = jnp.mean(hv, axis=1, keepdims=True)
            hc = hv - mu
            var = jnp.mean(hc * hc, axis=1, keepdims=True)
            hn = hc * lax.rsqrt(var + LN_EPS) * hng_ref[:, vs]
            out_ref[b, :, vs] = _sigmoid(o_ref[b, :, vs]) * hn
            c_ref[b, h] = w_inter * c_old + (w_intra * k_col) * v_row
            n_ref[b, h:h + 1, :] = w_inter * n_old + w_intra * k_row
            m_ref[b, h:h + 1, :] = m_tok


def mlstm_step(q, k, v, o, gates, c0, n0, m0, b_gates, hn_g, bb):
    bsz = q.shape[0]
    hv = N_HEADS * D_V
    qt = jnp.swapaxes(q, 1, 2)
    kt = jnp.swapaxes(k, 1, 2)
    b3 = lambda i: (i, 0, 0)
    return pl.pallas_call(
        _mlstm_step_body,
        grid=(bsz // bb,),
        in_specs=[pl.BlockSpec((bb, N_HEADS, D_QK), b3), pl.BlockSpec((bb, N_HEADS, D_QK), b3),
                  pl.BlockSpec((bb, D_QK, N_HEADS), b3), pl.BlockSpec((bb, D_QK, N_HEADS), b3),
                  pl.BlockSpec((bb, 1, hv), b3), pl.BlockSpec((bb, 1, hv), b3),
                  pl.BlockSpec((bb, 2 * N_HEADS, 1), b3),
                  pl.BlockSpec((bb, N_HEADS, D_QK, D_V), lambda i: (i, 0, 0, 0)),
                  pl.BlockSpec((bb, N_HEADS, D_QK), b3),
                  pl.BlockSpec((bb, N_HEADS, 1), b3),
                  pl.BlockSpec((2 * N_HEADS, 1), lambda i: (0, 0)),
                  pl.BlockSpec((1, hv), lambda i: (0, 0))],
        out_specs=[pl.BlockSpec((bb, 1, hv), b3),
                   pl.BlockSpec((bb, N_HEADS, D_QK, D_V), lambda i: (i, 0, 0, 0)),
                   pl.BlockSpec((bb, N_HEADS, D_QK), b3),
                   pl.BlockSpec((bb, N_HEADS, 1), b3)],
        out_shape=[jax.ShapeDtypeStruct((bsz, 1, hv), F32),
                   jax.ShapeDtypeStruct((bsz, N_HEADS, D_QK, D_V), F32),
                   jax.ShapeDtypeStruct((bsz, N_HEADS, D_QK), F32),
                   jax.ShapeDtypeStruct((bsz, N_HEADS, 1), F32)],
        compiler_params=_params(("arbitrary",), 4 * bb * N_HEADS * D_QK * D_V * 4 + (16 << 20)),
        name="mlstm_step",
    )(q, k, qt, kt, v, o, gates, c0, n0, m0, b_gates.reshape(2 * N_HEADS, 1), hn_g.reshape(1, hv))


def _trunk(x, p, fresh, bsz, seq_len, conv_state, c_st, n_st, m_st, ffn_state, w, tiles):
    tm, tm_ffn = tiles
    hq = N_HEADS * D_QK
    hv = N_HEADS * D_V
    new_ffn = []

    def ffn_block(x, i):
        past = None if fresh else ffn_state[i].reshape(bsz, 2 * D_FF)
        y, gt = ffn(x, past, w["ff_w_gate"][i], w["ff_w_up"][i], w["ff_w_down"][i], w["ff_w_dw"][i],
                    w["ff_b_dw"][i], w["ff_b_down"][i], w["ln_ffn_g"][i], w["ln_ffn_b"][i],
                    tm_ffn, 512, seq_len)
        if fresh:
            tps = seq_len // tm_ffn
            new_ffn.append(gt.reshape(bsz, tps, FFN_TAIL, D_FF)[:, tps - 1, FFN_TAIL - 2:, :])
        else:
            new_ffn.append(jnp.stack([ffn_state[i][:, 1, :], gt[0]], axis=1))
        return ple(y, p[i], w["pl_w_gate"][i], w["pl_w_proj"][i], w["pl_g"][i], tm)

    u = mm_glu(x, w["cv_w_in"][0], w["cv_b_in"][0], tm, 512)
    if fresh:
        u3 = u.reshape(bsz, seq_len, D_MODEL)
        y = conv_prompt(u3, w["cv_w_dw"][0], w["cv_b_dw"][0], 256).reshape(bsz * seq_len, D_MODEL)
        new_conv = u3[:, seq_len - (CONV_WIDTH - 1):, :]
    else:
        y = conv_sample(conv_state[0], u, w["cv_w_dw"][0], w["cv_b_dw"][0], 8)
        new_conv = jnp.concatenate([conv_state[0][:, 1:, :], u[:, None, :]], axis=1)
    x = mm_ln(y, x, w["cv_w_out"][0], w["cv_b_out"][0], w["ln_mix_g"][0], w["ln_mix_b"][0], tm,
              pre=(w["cv_ln_g"][0], w["cv_ln_b"][0]))
    x = ffn_block(x, 0)

    qkv = mm_qkv(x, w["ml_w_qkv"], tm, 1024)
    o, gates = mm_o_gates(x, w["ml_w_o"], w["ml_w_gates"], tm)
    if fresh:
        mix_in, c_new, n_new, m_new = mlstm_chunk(
            qkv, o, gates, gates[:, :2 * N_HEADS].T, w["ml_b_gates"][0], w["ml_hn_g"][0], bsz, seq_len)
    else:
        qf = qkv[:, :hq].astype(F32).reshape(bsz, N_HEADS, D_QK)
        kf = qkv[:, hq:2 * hq].astype(F32).reshape(bsz, N_HEADS, D_QK)
        vf = qkv[:, 2 * hq:].astype(F32).reshape(bsz, 1, hv)
        mix3, c_new, n_new, m3 = mlstm_step(
            qf, kf, vf, o.reshape(bsz, 1, hv), gates[:, :2 * N_HEADS, None],
            c_st[0], n_st[0], m_st[0][:, :, None], w["ml_b_gates"][0], w["ml_hn_g"][0], 4)
        mix_in = mix3.reshape(bsz, hv)
        m_new = m3[:, :, 0]
    x = mm_ln(mix_in, x, w["ml_w_out"][0], w["ml_b_out"][0], w["ln_mix_g"][1], w["ln_mix_b"][1], tm)
    x = ffn_block(x, 1)
    return x, new_conv[None], c_new[None], n_new[None], m_new[None], jnp.stack(new_ffn)


def kernel(x_prompt, x_sample, p_prompt, p_sample, state_conv, state_mlstm_c, state_mlstm_n,
           state_mlstm_m, state_ffn_conv,
           cv_w_in, cv_b_in, cv_w_dw, cv_b_dw, cv_ln_g, cv_ln_b, cv_w_out, cv_b_out,
           ml_w_in, ml_b_gates, ml_hn_g, ml_w_out, ml_b_out,
           ln_mix_g, ln_mix_b, ln_ffn_g, ln_ffn_b,
           ff_w_gate, ff_w_up, ff_w_dw, ff_b_dw, ff_w_down, ff_b_down,
           pl_w_proj, pl_g, pl_w_gate):
    bp, sp, _ = x_prompt.shape
    bs, ss, _ = x_sample.shape
    qkv_cols = 2 * N_HEADS * D_QK + N_HEADS * D_V
    o_cols = qkv_cols + N_HEADS * D_V
    ml_w = ml_w_in[0]
    w = dict(
        cv_w_in=cv_w_in.astype(BF16), cv_b_in=cv_b_in, cv_w_dw=cv_w_dw, cv_b_dw=cv_b_dw,
        cv_ln_g=cv_ln_g, cv_ln_b=cv_ln_b, cv_w_out=cv_w_out.astype(BF16), cv_b_out=cv_b_out,
        ml_w_qkv=ml_w[:, :qkv_cols].astype(BF16), ml_w_o=ml_w[:, qkv_cols:o_cols].astype(BF16),
        ml_w_gates=jnp.pad(ml_w[:, o_cols:], ((0, 0), (0, GATE_PAD - 2 * N_HEADS))).astype(BF16),
        ml_b_gates=ml_b_gates, ml_hn_g=ml_hn_g, ml_w_out=ml_w_out.astype(BF16), ml_b_out=ml_b_out,
        ln_mix_g=ln_mix_g, ln_mix_b=ln_mix_b, ln_ffn_g=ln_ffn_g, ln_ffn_b=ln_ffn_b,
        ff_w_gate=ff_w_gate.astype(BF16), ff_w_up=ff_w_up.astype(BF16), ff_w_dw=ff_w_dw,
        ff_b_dw=ff_b_dw, ff_w_down=ff_w_down.astype(BF16), ff_b_down=ff_b_down,
        pl_w_proj=pl_w_proj.astype(BF16), pl_g=pl_g, pl_w_gate=pl_w_gate.astype(BF16),
    )
    yp, conv_p, c_p, n_p, m_p, ffn_p = _trunk(
        x_prompt.reshape(bp * sp, D_MODEL), p_prompt.reshape(DEPTH, bp * sp, D_PLE), True, bp, sp,
        None, None, None, None, None, w, (512, 512))
    ys, conv_s, c_s, n_s, m_s, ffn_s = _trunk(
        x_sample.reshape(bs * ss, D_MODEL), p_sample.reshape(DEPTH, bs * ss, D_PLE), False, bs, ss,
        state_conv, state_mlstm_c, state_mlstm_n, state_mlstm_m, state_ffn_conv, w, (128, 128))
    return (yp.reshape(bp, sp, D_MODEL), ys.reshape(bs, ss, D_MODEL), conv_p, conv_s,
            c_p, n_p, m_p, c_s, n_s, m_s, ffn_p, ffn_s)
```

```python
import functools

import jax
import jax.numpy as jnp
from jax import lax
from jax.experimental import pallas as pl
from jax.experimental.pallas import tpu as pltpu

D_MODEL = 2048
D_FF = 5632
D_PLE = 256
N_HEADS = 8
D_QK = 128
D_V = 256
CONV_WIDTH = 31
FFN_CONV_WIDTH = 3
DEPTH = 2
ALPHA = (2.0 * DEPTH) ** 0.25
LN_EPS = 1e-5
MLSTM_CHUNK = 128
GATE_PAD = 128
SUBLANES = 8

F32 = jnp.float32
BF16 = jnp.bfloat16

VMEM_CAP_BYTES = 58 * 1024 * 1024
VMEM_SLACK_BYTES = 8 * 1024 * 1024


def _params(sem, vmem_bytes):
    return pltpu.CompilerParams(
        dimension_semantics=sem,
        vmem_limit_bytes=min(int(vmem_bytes) + VMEM_SLACK_BYTES, VMEM_CAP_BYTES))


def _ln(x, g, b):
    mu = jnp.mean(x, axis=-1, keepdims=True)
    xc = x - mu
    var = jnp.mean(xc * xc, axis=-1, keepdims=True)
    return xc * lax.rsqrt(var + LN_EPS) * g + b


def _sigmoid(x):
    return 1.0 / (1.0 + jnp.exp(-x))


def _log_sigmoid(x):
    return jnp.minimum(x, 0.0) - jnp.log(1.0 + jnp.exp(-jnp.abs(x)))


def _const_spec(shape, index):
    return pl.BlockSpec(shape, lambda *_: index, pipeline_mode=pl.Buffered(1))


def _layer_vec(v):
    return v.reshape(v.shape[0], 1, v.shape[1])


def _vec_spec(n, layer):
    return _const_spec((None, 1, n), (layer, 0, 0))


def _mm_glu_body(x_ref, wa_ref, wg_ref, ba_ref, bg_ref, u_ref, wab_ref, wgb_ref):
    @pl.when(pl.program_id(1) == 0)
    def _():
        wab_ref[...] = wa_ref[...].astype(BF16)
        wgb_ref[...] = wg_ref[...].astype(BF16)

    xb = x_ref[...].astype(BF16)
    a = jnp.dot(xb, wab_ref[...], preferred_element_type=F32) + ba_ref[...]
    g = jnp.dot(xb, wgb_ref[...], preferred_element_type=F32) + bg_ref[...]
    u_ref[...] = a * _sigmoid(g)


def mm_glu(x, w, b, layer, tm, tn):
    m, k = x.shape
    n = w.shape[2] // 2
    nj = n // tn
    b3 = _layer_vec(b)
    vmem = 2 * tm * k * 4 + 4 * k * tn * 4 + 2 * k * tn * 2 + 2 * tm * tn * 4 + tm * k * 2 + 6 * tm * tn * 4
    return pl.pallas_call(
        _mm_glu_body,
        grid=(nj, m // tm),
        in_specs=[pl.BlockSpec((tm, k), lambda j, i: (i, 0)),
                  pl.BlockSpec((None, k, tn), lambda j, i: (layer, 0, j)),
                  pl.BlockSpec((None, k, tn), lambda j, i: (layer, 0, j + nj)),
                  pl.BlockSpec((None, 1, tn), lambda j, i: (layer, 0, j)),
                  pl.BlockSpec((None, 1, tn), lambda j, i: (layer, 0, j + nj))],
        out_specs=pl.BlockSpec((tm, tn), lambda j, i: (i, j)),
        out_shape=jax.ShapeDtypeStruct((m, n), F32),
        scratch_shapes=[pltpu.VMEM((k, tn), BF16), pltpu.VMEM((k, tn), BF16)],
        compiler_params=_params(("arbitrary", "arbitrary"), vmem),
        name="mm_glu",
    )(x, w, w, b3, b3)


CONV_ROWS = 64
CONV_HALO = 32


def _conv_prompt_body(u_ref, w_ref, b_ref, y_ref, pad_ref):
    s, tc = u_ref.shape
    pad_ref[pl.ds(0, CONV_HALO), :] = jnp.zeros((CONV_HALO, tc), F32)
    pad_ref[pl.ds(CONV_HALO, s), :] = u_ref[...]
    win = CONV_ROWS + SUBLANES

    def chunk(i, carry):
        r0 = pl.multiple_of(i * CONV_ROWS, CONV_ROWS)
        acc = jnp.broadcast_to(b_ref[...], (CONV_ROWS, tc))
        for r in range(SUBLANES):
            part = None
            for a in range((CONV_WIDTH - 1 - r) // SUBLANES + 1):
                d = SUBLANES * a + r
                start = pl.multiple_of(r0 + CONV_HALO - SUBLANES * (a + 1), SUBLANES)
                term = pad_ref[pl.ds(start, win), :] * w_ref[pl.ds(CONV_WIDTH - 1 - d, 1), :]
                part = term if part is None else part + term
            acc = acc + part[SUBLANES - r:SUBLANES - r + CONV_ROWS, :]
        y_ref[pl.ds(r0, CONV_ROWS), :] = acc
        return carry

    lax.fori_loop(0, s // CONV_ROWS, chunk, 0)


def conv_prompt(u, w, b, layer, tc):
    bsz, s, c = u.shape
    return pl.pallas_call(
        _conv_prompt_body,
        grid=(bsz, c // tc),
        in_specs=[pl.BlockSpec((None, s, tc), lambda bi, ci: (bi, 0, ci)),
                  pl.BlockSpec((None, CONV_WIDTH, tc), lambda bi, ci: (layer, 0, ci)),
                  pl.BlockSpec((None, 1, tc), lambda bi, ci: (layer, 0, ci))],
        out_specs=pl.BlockSpec((None, s, tc), lambda bi, ci: (bi, 0, ci)),
        out_shape=jax.ShapeDtypeStruct((bsz, s, c), F32),
        scratch_shapes=[pltpu.VMEM((s + CONV_HALO, tc), F32)],
        compiler_params=_params(("arbitrary", "arbitrary"), 5 * s * tc * 4),
        name="conv_prompt",
    )(u, w, _layer_vec(b))


def _conv_sample_body(p_ref, u_ref, w_ref, b_ref, y_ref, np_ref):
    hist = CONV_WIDTH - 1
    u = u_ref[...]
    acc = jnp.sum(p_ref[...] * w_ref[pl.ds(0, hist), :][None], axis=1)
    y_ref[...] = acc + u * w_ref[pl.ds(hist, 1), :] + b_ref[...]
    np_ref[:, pl.ds(0, hist - 1), :] = p_ref[:, pl.ds(1, hist - 1), :]
    for b in range(u.shape[0]):
        np_ref[b, pl.ds(hist - 1, 1), :] = u_ref[pl.ds(b, 1), :]


def conv_sample(past, u, w, b, layer, bb):
    _, bsz, hist, c = past.shape
    return pl.pallas_call(
        _conv_sample_body,
        grid=(bsz // bb,),
        in_specs=[pl.BlockSpec((None, bb, hist, c), lambda i: (layer, i, 0, 0)),
                  pl.BlockSpec((bb, c), lambda i: (i, 0)),
                  _const_spec((None, CONV_WIDTH, c), (layer, 0, 0)),
                  _vec_spec(c, layer)],
        out_specs=[pl.BlockSpec((bb, c), lambda i: (i, 0)),
                   pl.BlockSpec((None, bb, hist, c), lambda i: (0, i, 0, 0))],
        out_shape=[jax.ShapeDtypeStruct((bsz, c), F32),
                   jax.ShapeDtypeStruct((1, bsz, hist, c), F32)],
        compiler_params=_params(("arbitrary",), 8 * bb * 32 * c * 4),
        name="conv_sample",
    )(past, u, w, _layer_vec(b))


def _mm_ln_body(pre_norm, *refs):
    if pre_norm:
        xin_ref, res_ref, w_ref, b_ref, pg_ref, pb_ref, g_ref, be_ref, o_ref, ob_ref = refs
        t = _ln(xin_ref[...].astype(F32), pg_ref[...], pb_ref[...])
        xb = (t * _sigmoid(t)).astype(BF16)
    else:
        xin_ref, res_ref, w_ref, b_ref, g_ref, be_ref, o_ref, ob_ref = refs
        xb = xin_ref[...].astype(BF16)
    mix = jnp.dot(xb, w_ref[...], preferred_element_type=F32) + b_ref[...]
    out = _ln(ALPHA * res_ref[...] + mix, g_ref[...], be_ref[...])
    o_ref[...] = out
    ob_ref[...] = out.astype(BF16)


def mm_ln(xin, resid, w, b, g, be, mix_layer, ln_layer, tm, pre=None):
    m, k = xin.shape
    n = w.shape[2]
    row = lambda i: (i, 0)
    in_specs = [pl.BlockSpec((tm, k), row), pl.BlockSpec((tm, n), row),
                _const_spec((None, k, n), (mix_layer, 0, 0)), _vec_spec(n, mix_layer)]
    args = [xin, resid, w, _layer_vec(b)]
    if pre is not None:
        in_specs += [_vec_spec(k, mix_layer), _vec_spec(k, mix_layer)]
        args += [_layer_vec(pre[0]), _layer_vec(pre[1])]
    in_specs += [_vec_spec(n, ln_layer), _vec_spec(n, ln_layer)]
    args += [_layer_vec(g), _layer_vec(be)]
    vmem = 2 * tm * k * xin.dtype.itemsize + 4 * tm * n * 4 + 2 * tm * n * 2 + k * n * 2 + 6 * tm * n * 4
    return pl.pallas_call(
        functools.partial(_mm_ln_body, pre is not None),
        grid=(m // tm,),
        in_specs=in_specs,
        out_specs=[pl.BlockSpec((tm, n), row), pl.BlockSpec((tm, n), row)],
        out_shape=[jax.ShapeDtypeStruct((m, n), F32), jax.ShapeDtypeStruct((m, n), BF16)],
        compiler_params=_params(("arbitrary",), vmem),
        name="mm_ln",
    )(*args)


FFN_TAIL = 8
FFN_SUB_ROWS = 256


def _ffn_up_body(fresh_seq, tiles_per_seq, sub_rows, *refs):
    if fresh_seq:
        (x_ref, wg_ref, wu_ref, wdw_ref, bdw_ref, h_ref, gt_ref, wgb_ref, wub_ref, gs_ref) = refs
    else:
        (x_ref, p2_ref, p1_ref, wg_ref, wu_ref, wdw_ref, bdw_ref, h_ref, gt_ref,
         wgb_ref, wub_ref) = refs
    i = pl.program_id(1)
    tm, tn = h_ref.shape

    @pl.when(i == 0)
    def _():
        wgb_ref[...] = wg_ref[...].astype(BF16)
        wub_ref[...] = wu_ref[...].astype(BF16)

    if fresh_seq:
        @pl.when((i % tiles_per_seq) == 0)
        def _():
            gs_ref[pl.ds(0, FFN_TAIL), :] = jnp.zeros((FFN_TAIL, tn), F32)

    w0 = wdw_ref[pl.ds(0, 1), :]
    w1 = wdw_ref[pl.ds(1, 1), :]
    w2 = wdw_ref[pl.ds(2, 1), :]
    for r0 in range(0, tm, sub_rows):
        xs = x_ref[pl.ds(r0, sub_rows), :]
        g = jnp.dot(xs, wgb_ref[...], preferred_element_type=F32)
        up = jnp.dot(xs, wub_ref[...], preferred_element_type=F32)
        if fresh_seq:
            gs_ref[pl.ds(FFN_TAIL + r0, sub_rows), :] = g
            g1 = gs_ref[pl.ds(FFN_TAIL + r0 - 1, sub_rows), :]
            g2 = gs_ref[pl.ds(FFN_TAIL + r0 - 2, sub_rows), :]
        else:
            g1 = p1_ref[pl.ds(r0, sub_rows), :]
            g2 = p2_ref[pl.ds(r0, sub_rows), :]
            gt_ref[pl.ds(r0, sub_rows), :] = g
        gc = w0 * g2 + w1 * g1 + w2 * g + bdw_ref[...]
        h_ref[pl.ds(r0, sub_rows), :] = (gc * _sigmoid(gc) * up).astype(BF16)
    if fresh_seq:
        tail = gs_ref[pl.ds(tm, FFN_TAIL), :]
        gt_ref[...] = tail
        gs_ref[pl.ds(0, FFN_TAIL), :] = tail


def ffn_up(xb, past, wg, wu, wdw, bdw, layer, tm, tn, seq_len):
    m, k = xb.shape
    dff = wg.shape[2]
    nj = dff // tn
    fresh = past is None
    sub = min(FFN_SUB_ROWS, tm)
    in_specs = [pl.BlockSpec((tm, k), lambda j, i: (i, 0))]
    args = [xb]
    if not fresh:
        in_specs += [pl.BlockSpec((tm, tn), lambda j, i: (i, j)),
                     pl.BlockSpec((tm, tn), lambda j, i: (i, j + nj))]
        args += [past, past]
    in_specs += [pl.BlockSpec((None, k, tn), lambda j, i: (layer, 0, j)),
                 pl.BlockSpec((None, k, tn), lambda j, i: (layer, 0, j)),
                 pl.BlockSpec((None, FFN_CONV_WIDTH, tn), lambda j, i: (layer, 0, j)),
                 pl.BlockSpec((None, 1, tn), lambda j, i: (layer, 0, j))]
    args += [wg, wu, wdw, _layer_vec(bdw)]
    tail_rows = FFN_TAIL if fresh else tm
    scratch = [pltpu.VMEM((k, tn), BF16), pltpu.VMEM((k, tn), BF16)]
    if fresh:
        scratch += [pltpu.VMEM((tm + FFN_TAIL, tn), F32)]
    vmem = (2 * tm * k * 2 + 4 * k * tn * 4 + 2 * k * tn * 2 + 2 * tm * tn * 2 + tm * tn * 4
            + 8 * sub * tn * 4 + 4 * tail_rows * tn * 4)
    return pl.pallas_call(
        functools.partial(_ffn_up_body, fresh, max(seq_len // tm, 1), sub),
        grid=(nj, m // tm),
        in_specs=in_specs,
        out_specs=[pl.BlockSpec((tm, tn), lambda j, i: (i, j)),
                   pl.BlockSpec((None, tail_rows, tn), lambda j, i: (i, 0, j))],
        out_shape=[jax.ShapeDtypeStruct((m, dff), BF16),
                   jax.ShapeDtypeStruct((m // tm, tail_rows, dff), F32)],
        scratch_shapes=scratch,
        compiler_params=_params(("arbitrary", "arbitrary"), vmem),
        name="ffn_up",
    )(*args)


def _ffn_down_body(h_ref, x_ref, wd_ref, bd_ref, s_ref, wdb_ref):
    @pl.when(pl.program_id(1) == 0)
    def _():
        wdb_ref[...] = wd_ref[...].astype(BF16)

    s_ref[...] = (ALPHA * x_ref[...] + bd_ref[...]
                  + jnp.dot(h_ref[...], wdb_ref[...], preferred_element_type=F32))


def ffn_down(h, x, wd, bd, layer, tm, tn):
    m, dff = h.shape
    n = x.shape[1]
    vmem = 2 * tm * dff * 2 + 2 * dff * tn * 4 + dff * tn * 2 + 6 * tm * tn * 4
    return pl.pallas_call(
        _ffn_down_body,
        grid=(n // tn, m // tm),
        in_specs=[pl.BlockSpec((tm, dff), lambda j, i: (i, 0)),
                  pl.BlockSpec((tm, tn), lambda j, i: (i, j)),
                  pl.BlockSpec((None, dff, tn), lambda j, i: (layer, 0, j)),
                  pl.BlockSpec((None, 1, tn), lambda j, i: (layer, 0, j))],
        out_specs=pl.BlockSpec((tm, tn), lambda j, i: (i, j)),
        out_shape=jax.ShapeDtypeStruct((m, n), F32),
        scratch_shapes=[pltpu.VMEM((dff, tn), BF16)],
        compiler_params=_params(("arbitrary", "arbitrary"), vmem),
        name="ffn_down",
    )(h, x, wd, _layer_vec(bd))


def _ple_body(s_ref, p_ref, lg_ref, lb_ref, wg_ref, wp_ref, g_ref, o_ref, ob_ref):
    x = _ln(s_ref[...], lg_ref[...], lb_ref[...])
    gate = _sigmoid(jnp.dot(x.astype(BF16), wg_ref[...], preferred_element_type=F32))
    e = jnp.dot(p_ref[...].astype(BF16), wp_ref[...], preferred_element_type=F32)
    e = e * lax.rsqrt(jnp.mean(e * e, axis=-1, keepdims=True) + LN_EPS) * g_ref[...]
    out = x + gate * e
    o_ref[...] = out
    ob_ref[...] = out.astype(BF16)


def ple(s, p, lg, lb, wg, wp, g, layer, tm):
    m, k = s.shape
    dp = p.shape[2]
    row = lambda i: (i, 0)
    vmem = 4 * tm * k * 4 + 2 * tm * k * 2 + 2 * tm * dp * 4 + k * k * 2 + dp * k * 2 + 6 * tm * k * 4
    return pl.pallas_call(
        _ple_body,
        grid=(m // tm,),
        in_specs=[pl.BlockSpec((tm, k), row),
                  pl.BlockSpec((None, tm, dp), lambda i: (layer, i, 0)),
                  _vec_spec(k, layer), _vec_spec(k, layer),
                  _const_spec((None, k, k), (layer, 0, 0)),
                  _const_spec((None, dp, k), (layer, 0, 0)),
                  _vec_spec(k, layer)],
        out_specs=[pl.BlockSpec((tm, k), row), pl.BlockSpec((tm, k), row)],
        out_shape=[jax.ShapeDtypeStruct((m, k), F32), jax.ShapeDtypeStruct((m, k), BF16)],
        compiler_params=_params(("arbitrary",), vmem),
        name="ple",
    )(s, p, _layer_vec(lg), _layer_vec(lb), wg, wp, _layer_vec(g))


def _mm_qkv_body(q_tiles, x_ref, w_ref, o_ref, wb_ref):
    @pl.when(pl.program_id(1) == 0)
    def _():
        wb_ref[...] = w_ref[...].astype(BF16)

    z = jnp.dot(x_ref[...], wb_ref[...], preferred_element_type=F32)
    scale = jnp.where(pl.program_id(0) < q_tiles, D_QK ** -0.5, 1.0).astype(F32)
    o_ref[...] = (z * scale).astype(BF16)


def mm_qkv(xb, w, layer, n, tm, tn):
    m, k = xb.shape
    vmem = 2 * tm * k * 2 + 2 * k * tn * 4 + k * tn * 2 + 2 * tm * tn * 2 + 3 * tm * tn * 4
    return pl.pallas_call(
        functools.partial(_mm_qkv_body, (N_HEADS * D_QK) // tn),
        grid=(n // tn, m // tm),
        in_specs=[pl.BlockSpec((tm, k), lambda j, i: (i, 0)),
                  pl.BlockSpec((None, k, tn), lambda j, i: (layer, 0, j))],
        out_specs=pl.BlockSpec((tm, tn), lambda j, i: (i, j)),
        out_shape=jax.ShapeDtypeStruct((m, n), BF16),
        scratch_shapes=[pltpu.VMEM((k, tn), BF16)],
        compiler_params=_params(("arbitrary", "arbitrary"), vmem),
        name="mm_qkv",
    )(xb, w)


def _mm_o_gates_body(x_ref, wo_ref, wgt_ref, o_ref, gt_ref, wob_ref):
    @pl.when(pl.program_id(0) == 0)
    def _():
        wob_ref[...] = wo_ref[...].astype(BF16)

    xb = x_ref[...]
    o_ref[...] = jnp.dot(xb, wob_ref[...], preferred_element_type=F32)
    gt_ref[...] = jnp.dot(xb, wgt_ref[...], preferred_element_type=F32)


def mm_o_gates(xb, w, wgt, layer, o_block, tm):
    m, k = xb.shape
    n = N_HEADS * D_V
    row = lambda i: (i, 0)
    vmem = 2 * tm * k * 2 + k * n * 4 + k * n * 2 + k * GATE_PAD * 2 + 2 * tm * (n + GATE_PAD) * 4 + 2 * tm * n * 4
    return pl.pallas_call(
        _mm_o_gates_body,
        grid=(m // tm,),
        in_specs=[pl.BlockSpec((tm, k), row),
                  _const_spec((None, k, n), (layer, 0, o_block)),
                  _const_spec((k, GATE_PAD), (0, 0))],
        out_specs=[pl.BlockSpec((tm, n), row), pl.BlockSpec((tm, GATE_PAD), row)],
        out_shape=[jax.ShapeDtypeStruct((m, n), F32), jax.ShapeDtypeStruct((m, GATE_PAD), F32)],
        scratch_shapes=[pltpu.VMEM((k, n), BF16)],
        compiler_params=_params(("arbitrary",), vmem),
        name="mm_o_gates",
    )(xb, w, wgt)


def _mlstm_chunk_body(q_ref, k_ref, v_ref, o_ref, gc_ref, gr_ref, brow_ref, bcol_ref, hng_ref,
                      out_ref, c_ref, n_ref, m_ref):
    length = q_ref.shape[0]

    @pl.when(pl.program_id(1) == 0)
    def _():
        c_ref[...] = jnp.zeros(c_ref.shape, F32)
        n_ref[...] = jnp.zeros(n_ref.shape, F32)
        m_ref[...] = jnp.zeros(m_ref.shape, F32)

    row = lax.broadcasted_iota(jnp.int32, (length, length), 0)
    col = lax.broadcasted_iota(jnp.int32, (length, length), 1)
    causal = row >= col
    tril = causal.astype(F32)
    triu = (row <= col).astype(F32)

    gcol = gc_ref[...] + brow_ref[...]
    bh_col = jnp.dot(tril, _log_sigmoid(gcol), precision=lax.Precision.HIGHEST,
                     preferred_element_type=F32)
    grow = gr_ref[...] + bcol_ref[...]
    li_row = grow[0:N_HEADS, :]
    bh_row = jnp.dot(_log_sigmoid(grow[N_HEADS:2 * N_HEADS, :]), triu,
                     precision=lax.Precision.HIGHEST, preferred_element_type=F32)
    a_row = li_row - bh_row

    for h in range(N_HEADS):
        qk = slice(h * D_QK, (h + 1) * D_QK)
        vs = slice(h * D_V, (h + 1) * D_V)
        bh_c = bh_col[:, N_HEADS + h:N_HEADS + h + 1]
        li_c = gcol[:, h:h + 1]
        m_old = m_ref[h][0:1, 0:1]
        d = jnp.where(causal, bh_c + a_row[h:h + 1, :], -jnp.inf)
        inter = bh_c + m_old
        m_tok = jnp.maximum(inter, jnp.max(d, axis=1, keepdims=True))
        w_intra = jnp.exp(d - m_tok)
        w_inter = jnp.exp(inter - m_tok)
        qh = q_ref[:, qk]
        kh = k_ref[:, qk]
        vh = v_ref[:, vs]
        c_old = c_ref[h]
        n_old = n_ref[h][0:1, :]
        s = lax.dot_general(qh, kh, (((1,), (1,)), ((), ())), preferred_element_type=F32) * w_intra
        num = (jnp.dot(s.astype(BF16), vh, preferred_element_type=F32)
               + jnp.dot(qh, c_old.astype(BF16), preferred_element_type=F32) * w_inter)
        qn = jnp.sum(qh.astype(F32) * n_old, axis=1, keepdims=True)
        den = jnp.sum(s, axis=1, keepdims=True) + qn * w_inter
        den = jnp.maximum(jnp.abs(den), jnp.exp(-m_tok))
        hv = num * (1.0 / den)
        mu = jnp.mean(hv, axis=1, keepdims=True)
        hc = hv - mu
        var = jnp.mean(hc * hc, axis=1, keepdims=True)
        hn = hc * lax.rsqrt(var + LN_EPS) * hng_ref[:, vs]
        out_ref[:, vs] = (_sigmoid(o_ref[:, vs]) * hn).astype(BF16)
        m_new = m_tok[length - 1:length, :]
        g_state = jnp.exp(inter[length - 1:length, :] - m_new)
        g_tok = jnp.exp(bh_c[length - 1:length, :] - bh_c + li_c - m_new)
        kg = kh.astype(F32) * g_tok
        c_ref[h] = g_state * c_old + lax.dot_general(
            kg.astype(BF16), vh, (((0,), (0,)), ((), ())), preferred_element_type=F32)
        n_new = g_state * n_old + jnp.sum(kg, axis=0, keepdims=True)
        n_ref[h] = jnp.broadcast_to(n_new, n_ref.shape[1:])
        m_ref[h] = jnp.broadcast_to(m_new, m_ref.shape[1:])


def mlstm_chunk(qkv, o, gates, gates_t, b_gates, hn_g, bsz, seq_len):
    length = MLSTM_CHUNK
    nch = seq_len // length
    hq = N_HEADS * D_QK
    hv = N_HEADS * D_V
    rows = lambda b, c: (b * nch + c, 0)
    brow = jnp.pad(b_gates.reshape(1, 2 * N_HEADS), ((0, 0), (0, GATE_PAD - 2 * N_HEADS)))
    bcol = b_gates.reshape(2 * N_HEADS, 1)
    state = lambda b, c: (b, 0, 0, 0)
    out, c_new, n_new, m_new = pl.pallas_call(
        _mlstm_chunk_body,
        grid=(bsz, nch),
        in_specs=[pl.BlockSpec((length, hq), rows),
                  pl.BlockSpec((length, hq), lambda b, c: (b * nch + c, 1)),
                  pl.BlockSpec((length, hv), lambda b, c: (b * nch + c, 1)),
                  pl.BlockSpec((length, hv), rows),
                  pl.BlockSpec((length, GATE_PAD), rows),
                  pl.BlockSpec((2 * N_HEADS, length), lambda b, c: (0, b * nch + c)),
                  pl.BlockSpec((1, GATE_PAD), lambda b, c: (0, 0)),
                  pl.BlockSpec((2 * N_HEADS, 1), lambda b, c: (0, 0)),
                  pl.BlockSpec((1, hv), lambda b, c: (0, 0))],
        out_specs=[pl.BlockSpec((length, hv), rows),
                   pl.BlockSpec((None, N_HEADS, D_QK, D_V), state),
                   pl.BlockSpec((None, N_HEADS, 8, D_QK), state),
                   pl.BlockSpec((None, N_HEADS, 8, 128), state)],
        out_shape=[jax.ShapeDtypeStruct((bsz * seq_len, hv), BF16),
                   jax.ShapeDtypeStruct((bsz, N_HEADS, D_QK, D_V), F32),
                   jax.ShapeDtypeStruct((bsz, N_HEADS, 8, D_QK), F32),
                   jax.ShapeDtypeStruct((bsz, N_HEADS, 8, 128), F32)],
        compiler_params=_params(("arbitrary", "arbitrary"), 24 << 20),
        name="mlstm_chunk",
    )(qkv, qkv, qkv, o, gates, gates_t, brow, bcol, hn_g.reshape(1, hv))
    return out, c_new, n_new[:, :, 0, :], m_new[:, :, 0, 0]


def _mlstm_step_body(q_ref, k_ref, kt_ref, v_ref, o_ref, gi_ref, gf_ref, c0_ref, n0_ref, m0_ref,
                     bi_ref, bf_ref, hng_ref, out_ref, c_ref, n_ref, m_ref, wv_ref, wi_ref):
    bb = q_ref.shape[0]
    li = gi_ref[...] + bi_ref[...]
    inter = _log_sigmoid(gf_ref[...] + bf_ref[...]) + m0_ref[...]
    m_tok = jnp.maximum(inter, li)
    w_intra = jnp.exp(li - m_tok)
    w_inter = jnp.exp(inter - m_tok)
    q = q_ref[...]
    k = k_ref[...]
    n_old = n0_ref[...]
    v = v_ref[...]
    s = jnp.sum(q * k, axis=-1, keepdims=True) * w_intra
    den = s + jnp.sum(q * n_old, axis=-1, keepdims=True) * w_inter
    den = jnp.maximum(jnp.abs(den), jnp.exp(-m_tok))
    n_ref[...] = w_inter * n_old + w_intra * k
    m_ref[...] = m_tok
    wv_ref[...] = w_intra * v
    wi_ref[...] = jnp.broadcast_to(w_inter, wi_ref.shape)

    head_row = lax.broadcasted_iota(jnp.int32, (N_HEADS, D_V), 0)
    qb16 = q.astype(BF16)
    qc_rows = []
    for b in range(bb):
        qc = jnp.zeros((N_HEADS, D_V), F32)
        for h in range(N_HEADS):
            c_old = c0_ref[b, h]
            all_heads = jnp.dot(qb16[b], c_old.astype(BF16), preferred_element_type=F32)
            qc = jnp.where(head_row == h, all_heads, qc)
            c_ref[b, h] = (wi_ref[b, pl.ds(h, 1), :] * c_old
                           + kt_ref[b, :, h:h + 1] * wv_ref[b, pl.ds(h, 1), :])
        qc_rows.append(qc)
    qc_all = jnp.stack(qc_rows, axis=0)
    hv = (s * v + qc_all * w_inter) * (1.0 / den)
    mu = jnp.mean(hv, axis=-1, keepdims=True)
    hc = hv - mu
    var = jnp.mean(hc * hc, axis=-1, keepdims=True)
    hn = hc * lax.rsqrt(var + LN_EPS) * hng_ref[...][None]
    out_ref[...] = _sigmoid(o_ref[...]) * hn


def mlstm_step(q, k, v, o, gates, c0, n0, m0, b_gates, hn_g, layer, bb):
    bsz = q.shape[0]
    kt = jnp.swapaxes(k, 1, 2)
    gi = gates[:, :N_HEADS, None]
    gf = gates[:, N_HEADS:2 * N_HEADS, None]
    m0 = m0[..., None]
    b3 = lambda i: (i, 0, 0)
    st3 = lambda i: (layer, i, 0, 0)
    bias = lambda v: v.reshape(N_HEADS, 1)
    out, c_new, n_new, m_new = pl.pallas_call(
        _mlstm_step_body,
        grid=(bsz // bb,),
        in_specs=[pl.BlockSpec((bb, N_HEADS, D_QK), b3), pl.BlockSpec((bb, N_HEADS, D_QK), b3),
                  pl.BlockSpec((bb, D_QK, N_HEADS), b3),
                  pl.BlockSpec((bb, N_HEADS, D_V), b3), pl.BlockSpec((bb, N_HEADS, D_V), b3),
                  pl.BlockSpec((bb, N_HEADS, 1), b3), pl.BlockSpec((bb, N_HEADS, 1), b3),
                  pl.BlockSpec((None, bb, N_HEADS, D_QK, D_V), lambda i: (layer, i, 0, 0, 0)),
                  pl.BlockSpec((None, bb, N_HEADS, D_QK), st3),
                  pl.BlockSpec((None, bb, N_HEADS, 1), st3),
                  _const_spec((N_HEADS, 1), (0, 0)), _const_spec((N_HEADS, 1), (0, 0)),
                  _const_spec((N_HEADS, D_V), (0, 0))],
        out_specs=[pl.BlockSpec((bb, N_HEADS, D_V), b3),
                   pl.BlockSpec((None, bb, N_HEADS, D_QK, D_V), lambda i: (0, i, 0, 0, 0)),
                   pl.BlockSpec((None, bb, N_HEADS, D_QK), lambda i: (0, i, 0, 0)),
                   pl.BlockSpec((None, bb, N_HEADS, 1), lambda i: (0, i, 0, 0))],
        out_shape=[jax.ShapeDtypeStruct((bsz, N_HEADS, D_V), F32),
                   jax.ShapeDtypeStruct((1, bsz, N_HEADS, D_QK, D_V), F32),
                   jax.ShapeDtypeStruct((1, bsz, N_HEADS, D_QK), F32),
                   jax.ShapeDtypeStruct((1, bsz, N_HEADS, 1), F32)],
        scratch_shapes=[pltpu.VMEM((bb, N_HEADS, D_V), F32), pltpu.VMEM((bb, N_HEADS, D_V), F32)],
        compiler_params=_params(("arbitrary",), 4 * bb * N_HEADS * D_QK * D_V * 4 + (8 << 20)),
        name="mlstm_step",
    )(q, k, kt, v, o, gi, gf, c0, n0, m0, bias(b_gates[0]), bias(b_gates[1]),
      hn_g.reshape(N_HEADS, D_V))
    return out, c_new, n_new, m_new[..., 0]


def _tiles(rows):
    big = rows >= 1024
    return dict(glu=512 if big else rows, row=512 if big else rows, up=1024 if big else rows,
                down=512 if big else rows, qkv=1024 if big else rows)


def _trunk(x, p, fresh, bsz, seq_len, states, w):
    conv_state, c_st, n_st, m_st, ffn_state = states
    rows = x.shape[0]
    t = _tiles(rows)
    hq = N_HEADS * D_QK
    hv = N_HEADS * D_V
    new_ffn = []

    def ffn_block(x32, xb, i):
        past = None if fresh else ffn_state[i].reshape(rows, 2 * D_FF)
        h, gt = ffn_up(xb, past, w["ff_w_gate"], w["ff_w_up"], w["ff_w_dw"], w["ff_b_dw"], i,
                       t["up"], 512, seq_len)
        if fresh:
            tps = seq_len // t["up"]
            new_ffn.append(gt.reshape(bsz, tps, FFN_TAIL, D_FF)[:, tps - 1, FFN_TAIL - 2:, :])
        else:
            new_ffn.append(jnp.stack([ffn_state[i][:, 1, :], gt[0]], axis=1))
        s = ffn_down(h, x32, w["ff_w_down"], w["ff_b_down"], i, t["down"], 512)
        return ple(s, p, w["ln_ffn_g"], w["ln_ffn_b"], w["pl_w_gate"], w["pl_w_proj"], w["pl_g"],
                   i, t["row"])

    u = mm_glu(x, w["cv_w_in"], w["cv_b_in"], 0, t["glu"], 512)
    if fresh:
        u3 = u.reshape(bsz, seq_len, D_MODEL)
        y = conv_prompt(u3, w["cv_w_dw"], w["cv_b_dw"], 0, 256).reshape(rows, D_MODEL)
        new_conv = u3[:, seq_len - (CONV_WIDTH - 1):, :][None]
    else:
        y, new_conv = conv_sample(conv_state, u, w["cv_w_dw"], w["cv_b_dw"], 0, 8)
    x32, xb = mm_ln(y, x, w["cv_w_out"], w["cv_b_out"], w["ln_mix_g"], w["ln_mix_b"], 0, 0, t["row"],
                    pre=(w["cv_ln_g"], w["cv_ln_b"]))
    x32, xb = ffn_block(x32, xb, 0)

    qkv = mm_qkv(xb, w["ml_w_in"], 0, 2 * hq + hv, t["qkv"], 1024)
    o, gates = mm_o_gates(xb, w["ml_w_in"], w["ml_w_gates"], 0, (2 * hq + hv) // hv, t["row"])
    if fresh:
        mix_in, c_new, n_new, m_new = mlstm_chunk(
            qkv, o, gates, gates[:, :2 * N_HEADS].T, w["ml_b_gates"][0], w["ml_hn_g"][0], bsz, seq_len)
        c_new, n_new, m_new = c_new[None], n_new[None], m_new[None]
    else:
        qf = qkv[:, :hq].astype(F32).reshape(bsz, N_HEADS, D_QK)
        kf = qkv[:, hq:2 * hq].astype(F32).reshape(bsz, N_HEADS, D_QK)
        vf = qkv[:, 2 * hq:].astype(F32).reshape(bsz, N_HEADS, D_V)
        mix3, c_new, n_new, m_new = mlstm_step(
            qf, kf, vf, o.reshape(bsz, N_HEADS, D_V), gates, c_st, n_st, m_st,
            w["ml_b_gates"][0], w["ml_hn_g"][0], 0, 4)
        mix_in = mix3.reshape(bsz, hv)
    x32, xb = mm_ln(mix_in, x32, w["ml_w_out"], w["ml_b_out"], w["ln_mix_g"], w["ln_mix_b"], 0, 1,
                    t["row"])
    x32, _ = ffn_block(x32, xb, 1)
    return x32, new_conv, c_new, n_new, m_new, jnp.stack(new_ffn)


def kernel(x_prompt, x_sample, p_prompt, p_sample, state_conv, state_mlstm_c, state_mlstm_n,
           state_mlstm_m, state_ffn_conv,
           cv_w_in, cv_b_in, cv_w_dw, cv_b_dw, cv_ln_g, cv_ln_b, cv_w_out, cv_b_out,
           ml_w_in, ml_b_gates, ml_hn_g, ml_w_out, ml_b_out,
           ln_mix_g, ln_mix_b, ln_ffn_g, ln_ffn_b,
           ff_w_gate, ff_w_up, ff_w_dw, ff_b_dw, ff_w_down, ff_b_down,
           pl_w_proj, pl_g, pl_w_gate):
    bp, sp, _ = x_prompt.shape
    bs, ss, _ = x_sample.shape
    gate_col = 2 * N_HEADS * D_QK + 2 * N_HEADS * D_V
    w = dict(
        cv_w_in=cv_w_in, cv_b_in=cv_b_in, cv_w_dw=cv_w_dw, cv_b_dw=cv_b_dw,
        cv_ln_g=cv_ln_g, cv_ln_b=cv_ln_b, cv_w_out=cv_w_out.astype(BF16), cv_b_out=cv_b_out,
        ml_w_in=ml_w_in,
        ml_w_gates=jnp.pad(ml_w_in[0][:, gate_col:], ((0, 0), (0, GATE_PAD - 2 * N_HEADS))).astype(BF16),
        ml_b_gates=ml_b_gates, ml_hn_g=ml_hn_g, ml_w_out=ml_w_out.astype(BF16), ml_b_out=ml_b_out,
        ln_mix_g=ln_mix_g, ln_mix_b=ln_mix_b, ln_ffn_g=ln_ffn_g, ln_ffn_b=ln_ffn_b,
        ff_w_gate=ff_w_gate, ff_w_up=ff_w_up, ff_w_dw=ff_w_dw,
        ff_b_dw=ff_b_dw, ff_w_down=ff_w_down, ff_b_down=ff_b_down,
        pl_w_proj=pl_w_proj.astype(BF16), pl_g=pl_g, pl_w_gate=pl_w_gate.astype(BF16),
    )
    yp, conv_p, c_p, n_p, m_p, ffn_p = _trunk(
        x_prompt.reshape(bp * sp, D_MODEL), p_prompt.reshape(DEPTH, bp * sp, D_PLE), True, bp, sp,
        (None, None, None, None, None), w)
    ys, conv_s, c_s, n_s, m_s, ffn_s = _trunk(
        x_sample.reshape(bs * ss, D_MODEL), p_sample.reshape(DEPTH, bs * ss, D_PLE), False, bs, ss,
        (state_conv, state_mlstm_c, state_mlstm_n, state_mlstm_m, state_ffn_conv), w)
    return (yp.reshape(bp, sp, D_MODEL), ys.reshape(bs, ss, D_MODEL), conv_p, conv_s,
            c_p, n_p, m_p, c_s, n_s, m_s, ffn_p, ffn_s)
```

```python
import functools

import jax
import jax.numpy as jnp
from jax import lax
from jax.experimental import pallas as pl
from jax.experimental.pallas import tpu as pltpu

D_MODEL = 2048
D_FF = 5632
D_PLE = 256
N_HEADS = 8
D_QK = 128
D_V = 256
CONV_WIDTH = 31
FFN_CONV_WIDTH = 3
DEPTH = 2
ALPHA = (2.0 * DEPTH) ** 0.25
LN_EPS = 1e-5
MLSTM_CHUNK = 128
GATE_PAD = 128
SUBLANES = 8

F32 = jnp.float32
BF16 = jnp.bfloat16

VMEM_CAP_BYTES = 58 * 1024 * 1024
VMEM_SLACK_BYTES = 8 * 1024 * 1024


def _params(sem, vmem_bytes):
    return pltpu.CompilerParams(
        dimension_semantics=sem,
        vmem_limit_bytes=min(int(vmem_bytes) + VMEM_SLACK_BYTES, VMEM_CAP_BYTES))


def _ln(x, g, b):
    mu = jnp.mean(x, axis=-1, keepdims=True)
    xc = x - mu
    var = jnp.mean(xc * xc, axis=-1, keepdims=True)
    return xc * lax.rsqrt(var + LN_EPS) * g + b


def _sigmoid(x):
    return 1.0 / (1.0 + jnp.exp(-x))


def _log_sigmoid(x):
    return jnp.minimum(x, 0.0) - jnp.log(1.0 + jnp.exp(-jnp.abs(x)))


def _const_spec(shape, index):
    return pl.BlockSpec(shape, lambda *_: index, pipeline_mode=pl.Buffered(1))


def _layer_vec(v):
    return v.reshape(v.shape[0], 1, v.shape[1])


def _vec_spec(n, layer):
    return _const_spec((None, 1, n), (layer, 0, 0))


def _mm_glu_body(x_ref, wa_ref, wg_ref, ba_ref, bg_ref, u_ref, wab_ref, wgb_ref):
    @pl.when(pl.program_id(1) == 0)
    def _():
        wab_ref[...] = wa_ref[...].astype(BF16)
        wgb_ref[...] = wg_ref[...].astype(BF16)

    xb = x_ref[...].astype(BF16)
    a = jnp.dot(xb, wab_ref[...], preferred_element_type=F32) + ba_ref[...]
    g = jnp.dot(xb, wgb_ref[...], preferred_element_type=F32) + bg_ref[...]
    u_ref[...] = a * _sigmoid(g)


def mm_glu(x, w, b, layer, tm, tn):
    m, k = x.shape
    n = w.shape[2] // 2
    nj = n // tn
    b3 = _layer_vec(b)
    vmem = 2 * tm * k * 4 + 4 * k * tn * 4 + 2 * k * tn * 2 + 2 * tm * tn * 4 + tm * k * 2 + 6 * tm * tn * 4
    return pl.pallas_call(
        _mm_glu_body,
        grid=(nj, m // tm),
        in_specs=[pl.BlockSpec((tm, k), lambda j, i: (i, 0)),
                  pl.BlockSpec((None, k, tn), lambda j, i: (layer, 0, j)),
                  pl.BlockSpec((None, k, tn), lambda j, i: (layer, 0, j + nj)),
                  pl.BlockSpec((None, 1, tn), lambda j, i: (layer, 0, j)),
                  pl.BlockSpec((None, 1, tn), lambda j, i: (layer, 0, j + nj))],
        out_specs=pl.BlockSpec((tm, tn), lambda j, i: (i, j)),
        out_shape=jax.ShapeDtypeStruct((m, n), F32),
        scratch_shapes=[pltpu.VMEM((k, tn), BF16), pltpu.VMEM((k, tn), BF16)],
        compiler_params=_params(("arbitrary", "arbitrary"), vmem),
        name="mm_glu",
    )(x, w, w, b3, b3)


CONV_ROWS = 64
CONV_HALO = 32
LANES = 128
GLU_SUB_ROWS = 256


def _conv_chunk(pad_ref, w_ref, b_ref, y_ref, row0, lane0):
    lanes = pl.ds(lane0, LANES)
    win = CONV_ROWS + SUBLANES
    acc = jnp.broadcast_to(b_ref[:, lanes], (CONV_ROWS, LANES))
    for r in range(SUBLANES):
        part = None
        for a in range((CONV_WIDTH - 1 - r) // SUBLANES + 1):
            d = SUBLANES * a + r
            start = row0 + CONV_HALO - SUBLANES * (a + 1)
            term = pad_ref[pl.ds(start, win), lanes] * w_ref[pl.ds(CONV_WIDTH - 1 - d, 1), lanes]
            part = term if part is None else part + term
        acc = acc + part[SUBLANES - r:SUBLANES - r + CONV_ROWS, :]
    y_ref[pl.ds(row0, CONV_ROWS), lanes] = acc


def _mm_glu_conv_body(tiles_per_seq, x_ref, wa_ref, wg_ref, ba_ref, bg_ref, wdw_ref, bdw_ref,
                      y_ref, tail_ref, wab_ref, wgb_ref, pad_ref):
    i = pl.program_id(1)
    tm, tn = y_ref.shape

    @pl.when(i == 0)
    def _():
        wab_ref[...] = wa_ref[...].astype(BF16)
        wgb_ref[...] = wg_ref[...].astype(BF16)

    @pl.when((i % tiles_per_seq) == 0)
    def _():
        pad_ref[pl.ds(0, CONV_HALO), :] = jnp.zeros((CONV_HALO, tn), F32)

    for r0 in range(0, tm, GLU_SUB_ROWS):
        xb = x_ref[pl.ds(r0, GLU_SUB_ROWS), :].astype(BF16)
        a = jnp.dot(xb, wab_ref[...], preferred_element_type=F32) + ba_ref[...]
        g = jnp.dot(xb, wgb_ref[...], preferred_element_type=F32) + bg_ref[...]
        pad_ref[pl.ds(CONV_HALO + r0, GLU_SUB_ROWS), :] = a * _sigmoid(g)
        for c0 in range(r0, r0 + GLU_SUB_ROWS, CONV_ROWS):
            for l0 in range(0, tn, LANES):
                _conv_chunk(pad_ref, wdw_ref, bdw_ref, y_ref, c0, l0)
    tail = pad_ref[pl.ds(tm, CONV_HALO), :]
    tail_ref[...] = tail
    pad_ref[pl.ds(0, CONV_HALO), :] = tail


def mm_glu_conv(x, w, b, wdw, bdw, layer, tm, tn, seq_len):
    m, k = x.shape
    n = w.shape[2] // 2
    nj = n // tn
    b3 = _layer_vec(b)
    vmem = (2 * tm * k * 4 + 4 * k * tn * 4 + 2 * k * tn * 2 + 2 * tm * tn * 4 + (tm + CONV_HALO) * tn * 4
            + GLU_SUB_ROWS * k * 2 + 6 * GLU_SUB_ROWS * tn * 4)
    return pl.pallas_call(
        functools.partial(_mm_glu_conv_body, seq_len // tm),
        grid=(nj, m // tm),
        in_specs=[pl.BlockSpec((tm, k), lambda j, i: (i, 0)),
                  pl.BlockSpec((None, k, tn), lambda j, i: (layer, 0, j)),
                  pl.BlockSpec((None, k, tn), lambda j, i: (layer, 0, j + nj)),
                  pl.BlockSpec((None, 1, tn), lambda j, i: (layer, 0, j)),
                  pl.BlockSpec((None, 1, tn), lambda j, i: (layer, 0, j + nj)),
                  pl.BlockSpec((None, CONV_WIDTH, tn), lambda j, i: (layer, 0, j)),
                  pl.BlockSpec((None, 1, tn), lambda j, i: (layer, 0, j))],
        out_specs=[pl.BlockSpec((tm, tn), lambda j, i: (i, j)),
                   pl.BlockSpec((None, CONV_HALO, tn), lambda j, i: (i, 0, j))],
        out_shape=[jax.ShapeDtypeStruct((m, n), F32),
                   jax.ShapeDtypeStruct((m // tm, CONV_HALO, n), F32)],
        scratch_shapes=[pltpu.VMEM((k, tn), BF16), pltpu.VMEM((k, tn), BF16),
                        pltpu.VMEM((tm + CONV_HALO, tn), F32)],
        compiler_params=_params(("arbitrary", "arbitrary"), vmem),
        name="mm_glu_conv",
    )(x, w, w, b3, b3, wdw, _layer_vec(bdw))


def _conv_sample_body(p_ref, u_ref, w_ref, b_ref, y_ref, np_ref):
    hist = p_ref.shape[0]
    u = u_ref[...]
    acc = u * w_ref[pl.ds(hist, 1), :] + b_ref[...]
    for j in range(hist):
        acc = acc + p_ref[j] * w_ref[pl.ds(j, 1), :]
    y_ref[...] = acc
    for j in range(hist - 1):
        np_ref[j] = p_ref[j + 1]
    np_ref[hist - 1] = u


def conv_sample(past_t, u, w, b, layer, tc):
    _, hist, bsz, c = past_t.shape
    return pl.pallas_call(
        _conv_sample_body,
        grid=(c // tc,),
        in_specs=[pl.BlockSpec((None, hist, bsz, tc), lambda i: (layer, 0, 0, i)),
                  pl.BlockSpec((bsz, tc), lambda i: (0, i)),
                  pl.BlockSpec((None, CONV_WIDTH, tc), lambda i: (layer, 0, i)),
                  pl.BlockSpec((None, 1, tc), lambda i: (layer, 0, i))],
        out_specs=[pl.BlockSpec((bsz, tc), lambda i: (0, i)),
                   pl.BlockSpec((None, hist, bsz, tc), lambda i: (0, 0, 0, i))],
        out_shape=[jax.ShapeDtypeStruct((bsz, c), F32),
                   jax.ShapeDtypeStruct((1, hist, bsz, c), F32)],
        compiler_params=_params(("arbitrary",), 5 * hist * bsz * tc * 4),
        name="conv_sample",
    )(past_t, u, w, _layer_vec(b))


ROW_SUB = 128


def _mm_ln_body(pre_norm, *refs):
    if pre_norm:
        xin_ref, res_ref, w_ref, b_ref, pg_ref, pb_ref, g_ref, be_ref, o_ref, ob_ref = refs
    else:
        xin_ref, res_ref, w_ref, b_ref, g_ref, be_ref, o_ref, ob_ref = refs
    tm = o_ref.shape[0]
    sub = min(ROW_SUB, tm)
    for r0 in range(0, tm, sub):
        rows = pl.ds(r0, sub)
        if pre_norm:
            t = _ln(xin_ref[rows, :].astype(F32), pg_ref[...], pb_ref[...])
            xb = (t * _sigmoid(t)).astype(BF16)
        else:
            xb = xin_ref[rows, :].astype(BF16)
        mix = jnp.dot(xb, w_ref[...], preferred_element_type=F32) + b_ref[...]
        out = _ln(ALPHA * res_ref[rows, :] + mix, g_ref[...], be_ref[...])
        o_ref[rows, :] = out
        ob_ref[rows, :] = out.astype(BF16)


def mm_ln(xin, resid, w, b, g, be, mix_layer, ln_layer, tm, pre=None):
    m, k = xin.shape
    n = w.shape[2]
    row = lambda i: (i, 0)
    in_specs = [pl.BlockSpec((tm, k), row), pl.BlockSpec((tm, n), row),
                _const_spec((None, k, n), (mix_layer, 0, 0)), _vec_spec(n, mix_layer)]
    args = [xin, resid, w, _layer_vec(b)]
    if pre is not None:
        in_specs += [_vec_spec(k, mix_layer), _vec_spec(k, mix_layer)]
        args += [_layer_vec(pre[0]), _layer_vec(pre[1])]
    in_specs += [_vec_spec(n, ln_layer), _vec_spec(n, ln_layer)]
    args += [_layer_vec(g), _layer_vec(be)]
    vmem = 2 * tm * k * xin.dtype.itemsize + 4 * tm * n * 4 + 2 * tm * n * 2 + k * n * 2 + 6 * tm * n * 4
    return pl.pallas_call(
        functools.partial(_mm_ln_body, pre is not None),
        grid=(m // tm,),
        in_specs=in_specs,
        out_specs=[pl.BlockSpec((tm, n), row), pl.BlockSpec((tm, n), row)],
        out_shape=[jax.ShapeDtypeStruct((m, n), F32), jax.ShapeDtypeStruct((m, n), BF16)],
        compiler_params=_params(("arbitrary",), vmem),
        name="mm_ln",
    )(*args)


FFN_TAIL = 8
FFN_SUB_ROWS = 256


def _ffn_up_body(fresh_seq, tiles_per_seq, sub_rows, *refs):
    if fresh_seq:
        (x_ref, wg_ref, wu_ref, wdw_ref, bdw_ref, h_ref, gt_ref, wgb_ref, wub_ref, gs_ref) = refs
    else:
        (x_ref, p2_ref, p1_ref, wg_ref, wu_ref, wdw_ref, bdw_ref, h_ref, gt_ref,
         wgb_ref, wub_ref) = refs
    i = pl.program_id(1)
    tm, tn = h_ref.shape

    @pl.when(i == 0)
    def _():
        wgb_ref[...] = wg_ref[...].astype(BF16)
        wub_ref[...] = wu_ref[...].astype(BF16)

    if fresh_seq:
        @pl.when((i % tiles_per_seq) == 0)
        def _():
            gs_ref[pl.ds(0, FFN_TAIL), :] = jnp.zeros((FFN_TAIL, tn), F32)

    w0 = wdw_ref[pl.ds(0, 1), :]
    w1 = wdw_ref[pl.ds(1, 1), :]
    w2 = wdw_ref[pl.ds(2, 1), :]
    for r0 in range(0, tm, sub_rows):
        xs = x_ref[pl.ds(r0, sub_rows), :]
        g = jnp.dot(xs, wgb_ref[...], preferred_element_type=F32)
        up = jnp.dot(xs, wub_ref[...], preferred_element_type=F32)
        if fresh_seq:
            gs_ref[pl.ds(FFN_TAIL + r0, sub_rows), :] = g
            g1 = gs_ref[pl.ds(FFN_TAIL + r0 - 1, sub_rows), :]
            g2 = gs_ref[pl.ds(FFN_TAIL + r0 - 2, sub_rows), :]
        else:
            g1 = p1_ref[pl.ds(r0, sub_rows), :]
            g2 = p2_ref[pl.ds(r0, sub_rows), :]
            gt_ref[pl.ds(r0, sub_rows), :] = g
        gc = w0 * g2 + w1 * g1 + w2 * g + bdw_ref[...]
        h_ref[pl.ds(r0, sub_rows), :] = (gc * _sigmoid(gc) * up).astype(BF16)
    if fresh_seq:
        tail = gs_ref[pl.ds(tm, FFN_TAIL), :]
        gt_ref[...] = tail
        gs_ref[pl.ds(0, FFN_TAIL), :] = tail


def ffn_up(xb, past, wg, wu, wdw, bdw, layer, tm, tn, seq_len):
    m, k = xb.shape
    dff = wg.shape[2]
    nj = dff // tn
    fresh = past is None
    sub = min(FFN_SUB_ROWS, tm)
    in_specs = [pl.BlockSpec((tm, k), lambda j, i: (i, 0))]
    args = [xb]
    if not fresh:
        in_specs += [pl.BlockSpec((tm, tn), lambda j, i: (i, j)),
                     pl.BlockSpec((tm, tn), lambda j, i: (i, j + nj))]
        args += [past, past]
    in_specs += [pl.BlockSpec((None, k, tn), lambda j, i: (layer, 0, j)),
                 pl.BlockSpec((None, k, tn), lambda j, i: (layer, 0, j)),
                 pl.BlockSpec((None, FFN_CONV_WIDTH, tn), lambda j, i: (layer, 0, j)),
                 pl.BlockSpec((None, 1, tn), lambda j, i: (layer, 0, j))]
    args += [wg, wu, wdw, _layer_vec(bdw)]
    tail_rows = FFN_TAIL if fresh else tm
    scratch = [pltpu.VMEM((k, tn), BF16), pltpu.VMEM((k, tn), BF16)]
    if fresh:
        scratch += [pltpu.VMEM((tm + FFN_TAIL, tn), F32)]
    vmem = (2 * tm * k * 2 + 4 * k * tn * 4 + 2 * k * tn * 2 + 2 * tm * tn * 2 + tm * tn * 4
            + 8 * sub * tn * 4 + 4 * tail_rows * tn * 4)
    return pl.pallas_call(
        functools.partial(_ffn_up_body, fresh, max(seq_len // tm, 1), sub),
        grid=(nj, m // tm),
        in_specs=in_specs,
        out_specs=[pl.BlockSpec((tm, tn), lambda j, i: (i, j)),
                   pl.BlockSpec((None, tail_rows, tn), lambda j, i: (i, 0, j))],
        out_shape=[jax.ShapeDtypeStruct((m, dff), BF16),
                   jax.ShapeDtypeStruct((m // tm, tail_rows, dff), F32)],
        scratch_shapes=scratch,
        compiler_params=_params(("arbitrary", "arbitrary"), vmem),
        name="ffn_up",
    )(*args)


def _ffn_down_body(h_ref, x_ref, wd_ref, bd_ref, s_ref, wdb_ref):
    @pl.when(pl.program_id(1) == 0)
    def _():
        wdb_ref[...] = wd_ref[...].astype(BF16)

    s_ref[...] = (ALPHA * x_ref[...] + bd_ref[...]
                  + jnp.dot(h_ref[...], wdb_ref[...], preferred_element_type=F32))


def ffn_down(h, x, wd, bd, layer, tm, tn):
    m, dff = h.shape
    n = x.shape[1]
    vmem = 2 * tm * dff * 2 + 2 * dff * tn * 4 + dff * tn * 2 + 6 * tm * tn * 4
    return pl.pallas_call(
        _ffn_down_body,
        grid=(n // tn, m // tm),
        in_specs=[pl.BlockSpec((tm, dff), lambda j, i: (i, 0)),
                  pl.BlockSpec((tm, tn), lambda j, i: (i, j)),
                  pl.BlockSpec((None, dff, tn), lambda j, i: (layer, 0, j)),
                  pl.BlockSpec((None, 1, tn), lambda j, i: (layer, 0, j))],
        out_specs=pl.BlockSpec((tm, tn), lambda j, i: (i, j)),
        out_shape=jax.ShapeDtypeStruct((m, n), F32),
        scratch_shapes=[pltpu.VMEM((dff, tn), BF16)],
        compiler_params=_params(("arbitrary", "arbitrary"), vmem),
        name="ffn_down",
    )(h, x, wd, _layer_vec(bd))


def _ple_body(s_ref, p_ref, lg_ref, lb_ref, wg_ref, wp_ref, g_ref, o_ref, ob_ref):
    tm = o_ref.shape[0]
    sub = min(ROW_SUB, tm)
    for r0 in range(0, tm, sub):
        rows = pl.ds(r0, sub)
        x = _ln(s_ref[rows, :], lg_ref[...], lb_ref[...])
        gate = _sigmoid(jnp.dot(x.astype(BF16), wg_ref[...], preferred_element_type=F32))
        e = jnp.dot(p_ref[rows, :].astype(BF16), wp_ref[...], preferred_element_type=F32)
        e = e * lax.rsqrt(jnp.mean(e * e, axis=-1, keepdims=True) + LN_EPS) * g_ref[...]
        out = x + gate * e
        o_ref[rows, :] = out
        ob_ref[rows, :] = out.astype(BF16)


def ple(s, p, lg, lb, wg, wp, g, layer, tm):
    m, k = s.shape
    dp = p.shape[2]
    row = lambda i: (i, 0)
    vmem = 4 * tm * k * 4 + 2 * tm * k * 2 + 2 * tm * dp * 4 + k * k * 2 + dp * k * 2 + 6 * tm * k * 4
    return pl.pallas_call(
        _ple_body,
        grid=(m // tm,),
        in_specs=[pl.BlockSpec((tm, k), row),
                  pl.BlockSpec((None, tm, dp), lambda i: (layer, i, 0)),
                  _vec_spec(k, layer), _vec_spec(k, layer),
                  _const_spec((None, k, k), (layer, 0, 0)),
                  _const_spec((None, dp, k), (layer, 0, 0)),
                  _vec_spec(k, layer)],
        out_specs=[pl.BlockSpec((tm, k), row), pl.BlockSpec((tm, k), row)],
        out_shape=[jax.ShapeDtypeStruct((m, k), F32), jax.ShapeDtypeStruct((m, k), BF16)],
        compiler_params=_params(("arbitrary",), vmem),
        name="ple",
    )(s, p, _layer_vec(lg), _layer_vec(lb), wg, wp, _layer_vec(g))


WT_CHUNK = 512


def _stage_transposed(wt_ref, wb_ref):
    n = wt_ref.shape[0]
    chunk = min(WT_CHUNK, n)
    for c0 in range(0, n, chunk):
        wb_ref[:, pl.ds(c0, chunk)] = wt_ref[pl.ds(c0, chunk), :].T.astype(BF16)


def _mm_qkv_body(q_tiles, x_ref, wt_ref, o_ref, wb_ref):
    @pl.when(pl.program_id(1) == 0)
    def _():
        _stage_transposed(wt_ref, wb_ref)

    z = jnp.dot(x_ref[...], wb_ref[...], preferred_element_type=F32)
    scale = jnp.where(pl.program_id(0) < q_tiles, D_QK ** -0.5, 1.0).astype(F32)
    o_ref[...] = (z * scale).astype(BF16)


def mm_qkv(xb, wt, layer, n, tm, tn):
    m, k = xb.shape
    vmem = 2 * tm * k * 2 + 2 * k * tn * 4 + k * tn * 2 + 2 * tm * tn * 2 + 3 * tm * tn * 4 + WT_CHUNK * k * 8
    return pl.pallas_call(
        functools.partial(_mm_qkv_body, (N_HEADS * D_QK) // tn),
        grid=(n // tn, m // tm),
        in_specs=[pl.BlockSpec((tm, k), lambda j, i: (i, 0)),
                  pl.BlockSpec((None, tn, k), lambda j, i: (layer, j, 0))],
        out_specs=pl.BlockSpec((tm, tn), lambda j, i: (i, j)),
        out_shape=jax.ShapeDtypeStruct((m, n), BF16),
        scratch_shapes=[pltpu.VMEM((k, tn), BF16)],
        compiler_params=_params(("arbitrary", "arbitrary"), vmem),
        name="mm_qkv",
    )(xb, wt)


def _mm_o_gates_body(x_ref, wot_ref, wgt_ref, o_ref, gt_ref, wob_ref, wgb_ref):
    @pl.when(pl.program_id(0) == 0)
    def _():
        _stage_transposed(wot_ref, wob_ref)
        lane = lax.broadcasted_iota(jnp.int32, wgb_ref.shape, 1)
        wgb_ref[...] = jnp.where(lane < 2 * N_HEADS, wgt_ref[...].T, 0.0).astype(BF16)

    xb = x_ref[...]
    o_ref[...] = jnp.dot(xb, wob_ref[...], preferred_element_type=F32)
    gt_ref[...] = jnp.dot(xb, wgb_ref[...], preferred_element_type=F32)


def mm_o_gates(xb, wt, layer, tm):
    m, k = xb.shape
    n = N_HEADS * D_V
    o_row = 2 * N_HEADS * D_QK + N_HEADS * D_V
    row = lambda i: (i, 0)
    vmem = (2 * tm * k * 2 + k * n * 4 + k * n * 2 + GATE_PAD * k * 6 + 2 * tm * (n + GATE_PAD) * 4
            + 2 * tm * n * 4 + WT_CHUNK * k * 8)
    return pl.pallas_call(
        _mm_o_gates_body,
        grid=(m // tm,),
        in_specs=[pl.BlockSpec((tm, k), row),
                  _const_spec((None, n, k), (layer, o_row // n, 0)),
                  _const_spec((None, GATE_PAD, k), (layer, (o_row + n) // GATE_PAD, 0))],
        out_specs=[pl.BlockSpec((tm, n), row), pl.BlockSpec((tm, GATE_PAD), row)],
        out_shape=[jax.ShapeDtypeStruct((m, n), F32), jax.ShapeDtypeStruct((m, GATE_PAD), F32)],
        scratch_shapes=[pltpu.VMEM((k, n), BF16), pltpu.VMEM((k, GATE_PAD), BF16)],
        compiler_params=_params(("arbitrary",), vmem),
        name="mm_o_gates",
    )(xb, wt, wt)


def _mlstm_chunk_body(q_ref, k_ref, v_ref, o_ref, gc_ref, gr_ref, brow_ref, bcol_ref, hng_ref,
                      out_ref, c_ref, n_ref, m_ref):
    length = q_ref.shape[0]

    @pl.when(pl.program_id(1) == 0)
    def _():
        c_ref[...] = jnp.zeros(c_ref.shape, F32)
        n_ref[...] = jnp.zeros(n_ref.shape, F32)
        m_ref[...] = jnp.zeros(m_ref.shape, F32)

    row = lax.broadcasted_iota(jnp.int32, (length, length), 0)
    col = lax.broadcasted_iota(jnp.int32, (length, length), 1)
    causal = row >= col
    tril = causal.astype(F32)
    triu = (row <= col).astype(F32)

    gcol = gc_ref[...] + brow_ref[...]
    bh_col = jnp.dot(tril, _log_sigmoid(gcol), precision=lax.Precision.HIGHEST,
                     preferred_element_type=F32)
    grow = gr_ref[...] + bcol_ref[...]
    li_row = grow[0:N_HEADS, :]
    bh_row = jnp.dot(_log_sigmoid(grow[N_HEADS:2 * N_HEADS, :]), triu,
                     precision=lax.Precision.HIGHEST, preferred_element_type=F32)
    a_row = li_row - bh_row

    for h in range(N_HEADS):
        qk = slice(h * D_QK, (h + 1) * D_QK)
        vs = slice(h * D_V, (h + 1) * D_V)
        bh_c = bh_col[:, N_HEADS + h:N_HEADS + h + 1]
        li_c = gcol[:, h:h + 1]
        m_old = m_ref[h][0:1, 0:1]
        d = jnp.where(causal, bh_c + a_row[h:h + 1, :], -jnp.inf)
        inter = bh_c + m_old
        m_tok = jnp.maximum(inter, jnp.max(d, axis=1, keepdims=True))
        w_intra = jnp.exp(d - m_tok)
        w_inter = jnp.exp(inter - m_tok)
        qh = q_ref[:, qk]
        kh = k_ref[:, qk]
        vh = v_ref[:, vs]
        c_old = c_ref[h]
        n_old = n_ref[h][0:1, :]
        s = lax.dot_general(qh, kh, (((1,), (1,)), ((), ())), preferred_element_type=F32) * w_intra
        num = (jnp.dot(s.astype(BF16), vh, preferred_element_type=F32)
               + jnp.dot(qh, c_old.astype(BF16), preferred_element_type=F32) * w_inter)
        qn = jnp.sum(qh.astype(F32) * n_old, axis=1, keepdims=True)
        den = jnp.sum(s, axis=1, keepdims=True) + qn * w_inter
        den = jnp.maximum(jnp.abs(den), jnp.exp(-m_tok))
        hv = num * (1.0 / den)
        mu = jnp.mean(hv, axis=1, keepdims=True)
        hc = hv - mu
        var = jnp.mean(hc * hc, axis=1, keepdims=True)
        hn = hc * lax.rsqrt(var + LN_EPS) * hng_ref[:, vs]
        out_ref[:, vs] = (_sigmoid(o_ref[:, vs]) * hn).astype(BF16)
        m_new = m_tok[length - 1:length, :]
        g_state = jnp.exp(inter[length - 1:length, :] - m_new)
        g_tok = jnp.exp(bh_c[length - 1:length, :] - bh_c + li_c - m_new)
        kg = kh.astype(F32) * g_tok
        c_ref[h] = g_state * c_old + lax.dot_general(
            kg.astype(BF16), vh, (((0,), (0,)), ((), ())), preferred_element_type=F32)
        n_new = g_state * n_old + jnp.sum(kg, axis=0, keepdims=True)
        n_ref[h] = jnp.broadcast_to(n_new, n_ref.shape[1:])
        m_ref[h] = jnp.broadcast_to(m_new, m_ref.shape[1:])


def mlstm_chunk(qkv, o, gates, gates_t, b_gates, hn_g, bsz, seq_len):
    length = MLSTM_CHUNK
    nch = seq_len // length
    hq = N_HEADS * D_QK
    hv = N_HEADS * D_V
    rows = lambda b, c: (b * nch + c, 0)
    brow = jnp.pad(b_gates.reshape(1, 2 * N_HEADS), ((0, 0), (0, GATE_PAD - 2 * N_HEADS)))
    bcol = b_gates.reshape(2 * N_HEADS, 1)
    state = lambda b, c: (b, 0, 0, 0)
    out, c_new, n_new, m_new = pl.pallas_call(
        _mlstm_chunk_body,
        grid=(bsz, nch),
        in_specs=[pl.BlockSpec((length, hq), rows),
                  pl.BlockSpec((length, hq), lambda b, c: (b * nch + c, 1)),
                  pl.BlockSpec((length, hv), lambda b, c: (b * nch + c, 1)),
                  pl.BlockSpec((length, hv), rows),
                  pl.BlockSpec((length, GATE_PAD), rows),
                  pl.BlockSpec((2 * N_HEADS, length), lambda b, c: (0, b * nch + c)),
                  pl.BlockSpec((1, GATE_PAD), lambda b, c: (0, 0)),
                  pl.BlockSpec((2 * N_HEADS, 1), lambda b, c: (0, 0)),
                  pl.BlockSpec((1, hv), lambda b, c: (0, 0))],
        out_specs=[pl.BlockSpec((length, hv), rows),
                   pl.BlockSpec((None, N_HEADS, D_QK, D_V), state),
                   pl.BlockSpec((None, N_HEADS, 8, D_QK), state),
                   pl.BlockSpec((None, N_HEADS, 8, 128), state)],
        out_shape=[jax.ShapeDtypeStruct((bsz * seq_len, hv), BF16),
                   jax.ShapeDtypeStruct((bsz, N_HEADS, D_QK, D_V), F32),
                   jax.ShapeDtypeStruct((bsz, N_HEADS, 8, D_QK), F32),
                   jax.ShapeDtypeStruct((bsz, N_HEADS, 8, 128), F32)],
        compiler_params=_params(("arbitrary", "arbitrary"), 24 << 20),
        name="mlstm_chunk",
    )(qkv, qkv, qkv, o, gates, gates_t, brow, bcol, hn_g.reshape(1, hv))
    return out, c_new, n_new[:, :, 0, :], m_new[:, :, 0, 0]


def _mlstm_step_body(q_ref, k_ref, kt_ref, v_ref, o_ref, gi_ref, gf_ref, c0_ref, n0_ref, m0_ref,
                     bi_ref, bf_ref, hng_ref, out_ref, c_ref, n_ref, m_ref, wv_ref, wi_ref):
    bb = q_ref.shape[0]
    li = gi_ref[...] + bi_ref[...]
    inter = _log_sigmoid(gf_ref[...] + bf_ref[...]) + m0_ref[...]
    m_tok = jnp.maximum(inter, li)
    w_intra = jnp.exp(li - m_tok)
    w_inter = jnp.exp(inter - m_tok)
    q = q_ref[...]
    k = k_ref[...]
    n_old = n0_ref[...]
    v = v_ref[...]
    s = jnp.sum(q * k, axis=-1, keepdims=True) * w_intra
    den = s + jnp.sum(q * n_old, axis=-1, keepdims=True) * w_inter
    den = jnp.maximum(jnp.abs(den), jnp.exp(-m_tok))
    n_ref[...] = w_inter * n_old + w_intra * k
    m_ref[...] = m_tok
    wv_ref[...] = w_intra * v
    wi_ref[...] = jnp.broadcast_to(w_inter, wi_ref.shape)

    head_row = lax.broadcasted_iota(jnp.int32, (N_HEADS, D_V), 0)
    qb16 = q.astype(BF16)
    qc_rows = []
    for b in range(bb):
        qc = jnp.zeros((N_HEADS, D_V), F32)
        for h in range(N_HEADS):
            c_old = c0_ref[b, h]
            all_heads = jnp.dot(qb16[b], c_old.astype(BF16), preferred_element_type=F32)
            qc = jnp.where(head_row == h, all_heads, qc)
            c_ref[b, h] = (wi_ref[b, pl.ds(h, 1), :] * c_old
                           + kt_ref[b, :, h:h + 1] * wv_ref[b, pl.ds(h, 1), :])
        qc_rows.append(qc)
    qc_all = jnp.stack(qc_rows, axis=0)
    hv = (s * v + qc_all * w_inter) * (1.0 / den)
    mu = jnp.mean(hv, axis=-1, keepdims=True)
    hc = hv - mu
    var = jnp.mean(hc * hc, axis=-1, keepdims=True)
    hn = hc * lax.rsqrt(var + LN_EPS) * hng_ref[...][None]
    out_ref[...] = _sigmoid(o_ref[...]) * hn


def mlstm_step(q, k, v, o, gates, c0, n0, m0, b_gates, hn_g, layer, bb):
    bsz = q.shape[0]
    kt = jnp.swapaxes(k, 1, 2)
    gi = gates[:, :N_HEADS, None]
    gf = gates[:, N_HEADS:2 * N_HEADS, None]
    m0 = m0[..., None]
    b3 = lambda i: (i, 0, 0)
    st3 = lambda i: (layer, i, 0, 0)
    bias = lambda v: v.reshape(N_HEADS, 1)
    out, c_new, n_new, m_new = pl.pallas_call(
        _mlstm_step_body,
        grid=(bsz // bb,),
        in_specs=[pl.BlockSpec((bb, N_HEADS, D_QK), b3), pl.BlockSpec((bb, N_HEADS, D_QK), b3),
                  pl.BlockSpec((bb, D_QK, N_HEADS), b3),
                  pl.BlockSpec((bb, N_HEADS, D_V), b3), pl.BlockSpec((bb, N_HEADS, D_V), b3),
                  pl.BlockSpec((bb, N_HEADS, 1), b3), pl.BlockSpec((bb, N_HEADS, 1), b3),
                  pl.BlockSpec((None, bb, N_HEADS, D_QK, D_V), lambda i: (layer, i, 0, 0, 0)),
                  pl.BlockSpec((None, bb, N_HEADS, D_QK), st3),
                  pl.BlockSpec((None, bb, N_HEADS, 1), st3),
                  _const_spec((N_HEADS, 1), (0, 0)), _const_spec((N_HEADS, 1), (0, 0)),
                  _const_spec((N_HEADS, D_V), (0, 0))],
        out_specs=[pl.BlockSpec((bb, N_HEADS, D_V), b3),
                   pl.BlockSpec((None, bb, N_HEADS, D_QK, D_V), lambda i: (0, i, 0, 0, 0)),
                   pl.BlockSpec((None, bb, N_HEADS, D_QK), lambda i: (0, i, 0, 0)),
                   pl.BlockSpec((None, bb, N_HEADS, 1), lambda i: (0, i, 0, 0))],
        out_shape=[jax.ShapeDtypeStruct((bsz, N_HEADS, D_V), F32),
                   jax.ShapeDtypeStruct((1, bsz, N_HEADS, D_QK, D_V), F32),
                   jax.ShapeDtypeStruct((1, bsz, N_HEADS, D_QK), F32),
                   jax.ShapeDtypeStruct((1, bsz, N_HEADS, 1), F32)],
        scratch_shapes=[pltpu.VMEM((bb, N_HEADS, D_V), F32), pltpu.VMEM((bb, N_HEADS, D_V), F32)],
        compiler_params=_params(("arbitrary",), 4 * bb * N_HEADS * D_QK * D_V * 4 + (8 << 20)),
        name="mlstm_step",
    )(q, k, kt, v, o, gi, gf, c0, n0, m0, bias(b_gates[0]), bias(b_gates[1]),
      hn_g.reshape(N_HEADS, D_V))
    return out, c_new, n_new, m_new[..., 0]


def _tiles(rows):
    big = rows >= 1024
    return dict(glu=1024 if big else rows, row=512 if big else rows, up=2048 if big else rows,
                down=512 if big else rows, qkv=1024 if big else rows)


def _trunk(x, p, fresh, bsz, seq_len, states, w):
    conv_state_t, c_st, n_st, m_st, ffn_state = states
    rows = x.shape[0]
    t = _tiles(rows)
    hq = N_HEADS * D_QK
    hv = N_HEADS * D_V
    new_ffn = []

    def ffn_block(x32, xb, i):
        past = None if fresh else ffn_state[i].reshape(rows, 2 * D_FF)
        h, gt = ffn_up(xb, past, w["ff_w_gate"], w["ff_w_up"], w["ff_w_dw"], w["ff_b_dw"], i,
                       t["up"], 512, seq_len)
        if fresh:
            tps = seq_len // t["up"]
            new_ffn.append(gt.reshape(bsz, tps, FFN_TAIL, D_FF)[:, tps - 1, FFN_TAIL - 2:, :])
        else:
            new_ffn.append(jnp.stack([ffn_state[i][:, 1, :], gt[0]], axis=1))
        s = ffn_down(h, x32, w["ff_w_down"], w["ff_b_down"], i, t["down"], 512)
        return ple(s, p, w["ln_ffn_g"], w["ln_ffn_b"], w["pl_w_gate"], w["pl_w_proj"], w["pl_g"],
                   i, t["row"])

    if fresh:
        y, tails = mm_glu_conv(x, w["cv_w_in"], w["cv_b_in"], w["cv_w_dw"], w["cv_b_dw"], 0,
                               t["glu"], 512, seq_len)
        tps = seq_len // t["glu"]
        hist = CONV_WIDTH - 1
        new_conv = tails.reshape(bsz, tps, CONV_HALO, D_MODEL)[:, tps - 1, CONV_HALO - hist:, :][None]
    else:
        u = mm_glu(x, w["cv_w_in"], w["cv_b_in"], 0, t["glu"], 512)
        y, new_conv_t = conv_sample(conv_state_t, u, w["cv_w_dw"], w["cv_b_dw"], 0, 256)
        new_conv = jnp.swapaxes(new_conv_t, 1, 2)
    x32, xb = mm_ln(y, x, w["cv_w_out"], w["cv_b_out"], w["ln_mix_g"], w["ln_mix_b"], 0, 0, t["row"],
                    pre=(w["cv_ln_g"], w["cv_ln_b"]))
    x32, xb = ffn_block(x32, xb, 0)

    qkv = mm_qkv(xb, w["ml_w_in_t"], 0, 2 * hq + hv, t["qkv"], 1024)
    o, gates = mm_o_gates(xb, w["ml_w_in_t"], 0, t["row"])
    if fresh:
        mix_in, c_new, n_new, m_new = mlstm_chunk(
            qkv, o, gates, gates[:, :2 * N_HEADS].T, w["ml_b_gates"][0], w["ml_hn_g"][0], bsz, seq_len)
        c_new, n_new, m_new = c_new[None], n_new[None], m_new[None]
    else:
        qf = qkv[:, :hq].astype(F32).reshape(bsz, N_HEADS, D_QK)
        kf = qkv[:, hq:2 * hq].astype(F32).reshape(bsz, N_HEADS, D_QK)
        vf = qkv[:, 2 * hq:].astype(F32).reshape(bsz, N_HEADS, D_V)
        mix3, c_new, n_new, m_new = mlstm_step(
            qf, kf, vf, o.reshape(bsz, N_HEADS, D_V), gates, c_st, n_st, m_st,
            w["ml_b_gates"][0], w["ml_hn_g"][0], 0, 4)
        mix_in = mix3.reshape(bsz, hv)
    x32, xb = mm_ln(mix_in, x32, w["ml_w_out"], w["ml_b_out"], w["ln_mix_g"], w["ln_mix_b"], 0, 1,
                    t["row"])
    x32, _ = ffn_block(x32, xb, 1)
    return x32, new_conv, c_new, n_new, m_new, jnp.stack(new_ffn)


def kernel(x_prompt, x_sample, p_prompt, p_sample, state_conv, state_mlstm_c, state_mlstm_n,
           state_mlstm_m, state_ffn_conv,
           cv_w_in, cv_b_in, cv_w_dw, cv_b_dw, cv_ln_g, cv_ln_b, cv_w_out, cv_b_out,
           ml_w_in, ml_b_gates, ml_hn_g, ml_w_out, ml_b_out,
           ln_mix_g, ln_mix_b, ln_ffn_g, ln_ffn_b,
           ff_w_gate, ff_w_up, ff_w_dw, ff_b_dw, ff_w_down, ff_b_down,
           pl_w_proj, pl_g, pl_w_gate):
    bp, sp, _ = x_prompt.shape
    bs, ss, _ = x_sample.shape
    w = dict(
        cv_w_in=cv_w_in, cv_b_in=cv_b_in, cv_w_dw=cv_w_dw, cv_b_dw=cv_b_dw,
        cv_ln_g=cv_ln_g, cv_ln_b=cv_ln_b, cv_w_out=cv_w_out.astype(BF16), cv_b_out=cv_b_out,
        ml_w_in_t=jnp.swapaxes(ml_w_in, 1, 2),
        ml_b_gates=ml_b_gates, ml_hn_g=ml_hn_g, ml_w_out=ml_w_out.astype(BF16), ml_b_out=ml_b_out,
        ln_mix_g=ln_mix_g, ln_mix_b=ln_mix_b, ln_ffn_g=ln_ffn_g, ln_ffn_b=ln_ffn_b,
        ff_w_gate=ff_w_gate, ff_w_up=ff_w_up, ff_w_dw=ff_w_dw,
        ff_b_dw=ff_b_dw, ff_w_down=ff_w_down, ff_b_down=ff_b_down,
        pl_w_proj=pl_w_proj.astype(BF16), pl_g=pl_g, pl_w_gate=pl_w_gate.astype(BF16),
    )
    yp, conv_p, c_p, n_p, m_p, ffn_p = _trunk(
        x_prompt.reshape(bp * sp, D_MODEL), p_prompt.reshape(DEPTH, bp * sp, D_PLE), True, bp, sp,
        (None, None, None, None, None), w)
    ys, conv_s, c_s, n_s, m_s, ffn_s = _trunk(
        x_sample.reshape(bs * ss, D_MODEL), p_sample.reshape(DEPTH, bs * ss, D_PLE), False, bs, ss,
        (jnp.swapaxes(state_conv, 1, 2), state_mlstm_c, state_mlstm_n, state_mlstm_m, state_ffn_conv), w)
    return (yp.reshape(bp, sp, D_MODEL), ys.reshape(bs, ss, D_MODEL), conv_p, conv_s,
            c_p, n_p, m_p, c_s, n_s, m_s, ffn_p, ffn_s)
```

```python
import functools

import jax
import jax.numpy as jnp
from jax import lax
from jax.experimental import pallas as pl
from jax.experimental.pallas import tpu as pltpu

D_MODEL = 2048
D_FF = 5632
D_PLE = 256
N_HEADS = 8
D_QK = 128
D_V = 256
CONV_WIDTH = 31
FFN_CONV_WIDTH = 3
DEPTH = 2
ALPHA = (2.0 * DEPTH) ** 0.25
LN_EPS = 1e-5
MLSTM_CHUNK = 128
GATE_PAD = 128
SUBLANES = 8

F32 = jnp.float32
BF16 = jnp.bfloat16

VMEM_CAP_BYTES = 58 * 1024 * 1024
VMEM_SLACK_BYTES = 8 * 1024 * 1024


def _params(sem, vmem_bytes):
    return pltpu.CompilerParams(
        dimension_semantics=sem,
        vmem_limit_bytes=min(int(vmem_bytes) + VMEM_SLACK_BYTES, VMEM_CAP_BYTES))


def _ln(x, g, b):
    mu = jnp.mean(x, axis=-1, keepdims=True)
    xc = x - mu
    var = jnp.mean(xc * xc, axis=-1, keepdims=True)
    return xc * lax.rsqrt(var + LN_EPS) * g + b


def _sigmoid(x):
    return 1.0 / (1.0 + jnp.exp(-x))


def _log_sigmoid(x):
    return jnp.minimum(x, 0.0) - jnp.log(1.0 + jnp.exp(-jnp.abs(x)))


def _row_scan(x, op, identity):
    rows, cols = x.shape
    shift = 1
    while shift < rows:
        shifted = jnp.concatenate([jnp.full((shift, cols), identity, x.dtype), x[:rows - shift, :]], axis=0)
        x = op(x, shifted)
        shift *= 2
    return x


def _const_spec(shape, index):
    return pl.BlockSpec(shape, lambda *_: index, pipeline_mode=pl.Buffered(1))


def _layer_vec(v):
    return v.reshape(v.shape[0], 1, v.shape[1])


def _vec_spec(n, layer):
    return _const_spec((None, 1, n), (layer, 0, 0))


def _mm_glu_body(x_ref, wa_ref, wg_ref, ba_ref, bg_ref, u_ref, wab_ref, wgb_ref):
    @pl.when(pl.program_id(1) == 0)
    def _():
        wab_ref[...] = wa_ref[...].astype(BF16)
        wgb_ref[...] = wg_ref[...].astype(BF16)

    xb = x_ref[...].astype(BF16)
    a = jnp.dot(xb, wab_ref[...], preferred_element_type=F32) + ba_ref[...]
    g = jnp.dot(xb, wgb_ref[...], preferred_element_type=F32) + bg_ref[...]
    u_ref[...] = a * _sigmoid(g)


def mm_glu(x, w, b, layer, tm, tn):
    m, k = x.shape
    n = w.shape[2] // 2
    nj = n // tn
    b3 = _layer_vec(b)
    vmem = 2 * tm * k * 4 + 4 * k * tn * 4 + 2 * k * tn * 2 + 2 * tm * tn * 4 + tm * k * 2 + 6 * tm * tn * 4
    return pl.pallas_call(
        _mm_glu_body,
        grid=(nj, m // tm),
        in_specs=[pl.BlockSpec((tm, k), lambda j, i: (i, 0)),
                  pl.BlockSpec((None, k, tn), lambda j, i: (layer, 0, j)),
                  pl.BlockSpec((None, k, tn), lambda j, i: (layer, 0, j + nj)),
                  pl.BlockSpec((None, 1, tn), lambda j, i: (layer, 0, j)),
                  pl.BlockSpec((None, 1, tn), lambda j, i: (layer, 0, j + nj))],
        out_specs=pl.BlockSpec((tm, tn), lambda j, i: (i, j)),
        out_shape=jax.ShapeDtypeStruct((m, n), F32),
        scratch_shapes=[pltpu.VMEM((k, tn), BF16), pltpu.VMEM((k, tn), BF16)],
        compiler_params=_params(("arbitrary", "arbitrary"), vmem),
        name="mm_glu",
    )(x, w, w, b3, b3)


CONV_ROWS = 64
CONV_HALO = 32
LANES = 128
GLU_SUB_ROWS = 256


def _conv_chunk(pad_ref, w_ref, b_ref, y_ref, row0, lane0):
    lanes = pl.ds(lane0, LANES)
    win = CONV_ROWS + SUBLANES
    acc = jnp.broadcast_to(b_ref[:, lanes], (CONV_ROWS, LANES))
    for r in range(SUBLANES):
        part = None
        for a in range((CONV_WIDTH - 1 - r) // SUBLANES + 1):
            d = SUBLANES * a + r
            start = row0 + CONV_HALO - SUBLANES * (a + 1)
            term = pad_ref[pl.ds(start, win), lanes] * w_ref[pl.ds(CONV_WIDTH - 1 - d, 1), lanes]
            part = term if part is None else part + term
        acc = acc + part[SUBLANES - r:SUBLANES - r + CONV_ROWS, :]
    y_ref[pl.ds(row0, CONV_ROWS), lanes] = acc


def _mm_glu_conv_body(tiles_per_seq, x_ref, wa_ref, wg_ref, ba_ref, bg_ref, wdw_ref, bdw_ref,
                      y_ref, tail_ref, wab_ref, wgb_ref, pad_ref):
    i = pl.program_id(1)
    tm, tn = y_ref.shape

    @pl.when(i == 0)
    def _():
        wab_ref[...] = wa_ref[...].astype(BF16)
        wgb_ref[...] = wg_ref[...].astype(BF16)

    @pl.when((i % tiles_per_seq) == 0)
    def _():
        pad_ref[pl.ds(0, CONV_HALO), :] = jnp.zeros((CONV_HALO, tn), F32)

    for r0 in range(0, tm, GLU_SUB_ROWS):
        xb = x_ref[pl.ds(r0, GLU_SUB_ROWS), :].astype(BF16)
        a = jnp.dot(xb, wab_ref[...], preferred_element_type=F32) + ba_ref[...]
        g = jnp.dot(xb, wgb_ref[...], preferred_element_type=F32) + bg_ref[...]
        pad_ref[pl.ds(CONV_HALO + r0, GLU_SUB_ROWS), :] = a * _sigmoid(g)
        for c0 in range(r0, r0 + GLU_SUB_ROWS, CONV_ROWS):
            for l0 in range(0, tn, LANES):
                _conv_chunk(pad_ref, wdw_ref, bdw_ref, y_ref, c0, l0)
    tail = pad_ref[pl.ds(tm, CONV_HALO), :]
    tail_ref[...] = tail
    pad_ref[pl.ds(0, CONV_HALO), :] = tail


def mm_glu_conv(x, w, b, wdw, bdw, layer, tm, tn, seq_len):
    m, k = x.shape
    n = w.shape[2] // 2
    nj = n // tn
    b3 = _layer_vec(b)
    vmem = (2 * tm * k * 4 + 4 * k * tn * 4 + 2 * k * tn * 2 + 2 * tm * tn * 4 + (tm + CONV_HALO) * tn * 4
            + GLU_SUB_ROWS * k * 2 + 6 * GLU_SUB_ROWS * tn * 4)
    return pl.pallas_call(
        functools.partial(_mm_glu_conv_body, seq_len // tm),
        grid=(nj, m // tm),
        in_specs=[pl.BlockSpec((tm, k), lambda j, i: (i, 0)),
                  pl.BlockSpec((None, k, tn), lambda j, i: (layer, 0, j)),
                  pl.BlockSpec((None, k, tn), lambda j, i: (layer, 0, j + nj)),
                  pl.BlockSpec((None, 1, tn), lambda j, i: (layer, 0, j)),
                  pl.BlockSpec((None, 1, tn), lambda j, i: (layer, 0, j + nj)),
                  pl.BlockSpec((None, CONV_WIDTH, tn), lambda j, i: (layer, 0, j)),
                  pl.BlockSpec((None, 1, tn), lambda j, i: (layer, 0, j))],
        out_specs=[pl.BlockSpec((tm, tn), lambda j, i: (i, j)),
                   pl.BlockSpec((None, CONV_HALO, tn), lambda j, i: (i, 0, j))],
        out_shape=[jax.ShapeDtypeStruct((m, n), F32),
                   jax.ShapeDtypeStruct((m // tm, CONV_HALO, n), F32)],
        scratch_shapes=[pltpu.VMEM((k, tn), BF16), pltpu.VMEM((k, tn), BF16),
                        pltpu.VMEM((tm + CONV_HALO, tn), F32)],
        compiler_params=_params(("arbitrary", "arbitrary"), vmem),
        name="mm_glu_conv",
    )(x, w, w, b3, b3, wdw, _layer_vec(bdw))


def _conv_sample_body(p_ref, u_ref, w_ref, b_ref, y_ref, np_ref):
    hist = p_ref.shape[0]
    u = u_ref[...]
    acc = u * w_ref[pl.ds(hist, 1), :] + b_ref[...]
    for j in range(hist):
        acc = acc + p_ref[j] * w_ref[pl.ds(j, 1), :]
    y_ref[...] = acc
    for j in range(hist - 1):
        np_ref[j] = p_ref[j + 1]
    np_ref[hist - 1] = u


def conv_sample(past_t, u, w, b, layer, tc):
    _, hist, bsz, c = past_t.shape
    return pl.pallas_call(
        _conv_sample_body,
        grid=(c // tc,),
        in_specs=[pl.BlockSpec((None, hist, bsz, tc), lambda i: (layer, 0, 0, i)),
                  pl.BlockSpec((bsz, tc), lambda i: (0, i)),
                  pl.BlockSpec((None, CONV_WIDTH, tc), lambda i: (layer, 0, i)),
                  pl.BlockSpec((None, 1, tc), lambda i: (layer, 0, i))],
        out_specs=[pl.BlockSpec((bsz, tc), lambda i: (0, i)),
                   pl.BlockSpec((None, hist, bsz, tc), lambda i: (0, 0, 0, i))],
        out_shape=[jax.ShapeDtypeStruct((bsz, c), F32),
                   jax.ShapeDtypeStruct((1, hist, bsz, c), F32)],
        compiler_params=_params(("arbitrary",), 5 * hist * bsz * tc * 4),
        name="conv_sample",
    )(past_t, u, w, _layer_vec(b))


ROW_SUB = 256


def _mm_ln_body(pre_norm, *refs):
    if pre_norm:
        xin_ref, res_ref, w_ref, b_ref, pg_ref, pb_ref, g_ref, be_ref, o_ref, ob_ref = refs
    else:
        xin_ref, res_ref, w_ref, b_ref, g_ref, be_ref, o_ref, ob_ref = refs
    tm = o_ref.shape[0]
    sub = min(ROW_SUB, tm)
    for r0 in range(0, tm, sub):
        rows = pl.ds(r0, sub)
        if pre_norm:
            t = _ln(xin_ref[rows, :].astype(F32), pg_ref[...], pb_ref[...])
            xb = (t * _sigmoid(t)).astype(BF16)
        else:
            xb = xin_ref[rows, :].astype(BF16)
        mix = jnp.dot(xb, w_ref[...], preferred_element_type=F32) + b_ref[...]
        out = _ln(ALPHA * res_ref[rows, :] + mix, g_ref[...], be_ref[...])
        o_ref[rows, :] = out
        ob_ref[rows, :] = out.astype(BF16)


def mm_ln(xin, resid, w, b, g, be, mix_layer, ln_layer, tm, pre=None):
    m, k = xin.shape
    n = w.shape[2]
    row = lambda i: (i, 0)
    in_specs = [pl.BlockSpec((tm, k), row), pl.BlockSpec((tm, n), row),
                _const_spec((None, k, n), (mix_layer, 0, 0)), _vec_spec(n, mix_layer)]
    args = [xin, resid, w, _layer_vec(b)]
    if pre is not None:
        in_specs += [_vec_spec(k, mix_layer), _vec_spec(k, mix_layer)]
        args += [_layer_vec(pre[0]), _layer_vec(pre[1])]
    in_specs += [_vec_spec(n, ln_layer), _vec_spec(n, ln_layer)]
    args += [_layer_vec(g), _layer_vec(be)]
    vmem = 2 * tm * k * xin.dtype.itemsize + 4 * tm * n * 4 + 2 * tm * n * 2 + k * n * 2 + 6 * tm * n * 4
    return pl.pallas_call(
        functools.partial(_mm_ln_body, pre is not None),
        grid=(m // tm,),
        in_specs=in_specs,
        out_specs=[pl.BlockSpec((tm, n), row), pl.BlockSpec((tm, n), row)],
        out_shape=[jax.ShapeDtypeStruct((m, n), F32), jax.ShapeDtypeStruct((m, n), BF16)],
        compiler_params=_params(("arbitrary",), vmem),
        name="mm_ln",
    )(*args)


FFN_TAIL = 8
FFN_SUB_ROWS = 256


def _ffn_hidden(g, g1, g2, up, wdw_ref, bdw_ref):
    gc = (wdw_ref[pl.ds(0, 1), :] * g2 + wdw_ref[pl.ds(1, 1), :] * g1 + wdw_ref[pl.ds(2, 1), :] * g
          + bdw_ref[...])
    return (gc * _sigmoid(gc) * up).astype(BF16)


def _ffn_up_body(tiles_per_seq, sub_rows, x_ref, xs_ref, p2_ref, p1_ref, wg_ref, wu_ref, wdw_ref,
                 bdw_ref, h_ref, gt_ref, hs_ref, gts_ref, wgb_ref, wub_ref, gs_ref):
    i = pl.program_id(1)
    tm, tn = h_ref.shape

    @pl.when(i == 0)
    def _():
        wgb_ref[...] = wg_ref[...].astype(BF16)
        wub_ref[...] = wu_ref[...].astype(BF16)

    @pl.when((i % tiles_per_seq) == 0)
    def _():
        gs_ref[pl.ds(0, FFN_TAIL), :] = jnp.zeros((FFN_TAIL, tn), F32)

    for r0 in range(0, tm, sub_rows):
        xr = x_ref[pl.ds(r0, sub_rows), :]
        g = jnp.dot(xr, wgb_ref[...], preferred_element_type=F32)
        up = jnp.dot(xr, wub_ref[...], preferred_element_type=F32)
        gs_ref[pl.ds(FFN_TAIL + r0, sub_rows), :] = g
        g1 = gs_ref[pl.ds(FFN_TAIL + r0 - 1, sub_rows), :]
        g2 = gs_ref[pl.ds(FFN_TAIL + r0 - 2, sub_rows), :]
        h_ref[pl.ds(r0, sub_rows), :] = _ffn_hidden(g, g1, g2, up, wdw_ref, bdw_ref)
    tail = gs_ref[pl.ds(tm, FFN_TAIL), :]
    gt_ref[...] = tail
    gs_ref[pl.ds(0, FFN_TAIL), :] = tail

    @pl.when(i == pl.num_programs(1) - 1)
    def _():
        xs = xs_ref[...]
        g = jnp.dot(xs, wgb_ref[...], preferred_element_type=F32)
        up = jnp.dot(xs, wub_ref[...], preferred_element_type=F32)
        gts_ref[...] = g
        hs_ref[...] = _ffn_hidden(g, p1_ref[...], p2_ref[...], up, wdw_ref, bdw_ref)


def ffn_up(xb, xb_s, past_s, wg, wu, wdw, bdw, layer, tm, tn, seq_len):
    m, k = xb.shape
    ms = xb_s.shape[0]
    dff = wg.shape[2]
    nj = dff // tn
    sub = min(FFN_SUB_ROWS, tm)
    vmem = (2 * tm * k * 2 + ms * k * 2 + 4 * k * tn * 4 + 2 * k * tn * 2 + 2 * tm * tn * 2 + tm * tn * 4
            + 8 * sub * tn * 4 + 16 * ms * tn * 4)
    return pl.pallas_call(
        functools.partial(_ffn_up_body, seq_len // tm, sub),
        grid=(nj, m // tm),
        in_specs=[pl.BlockSpec((tm, k), lambda j, i: (i, 0)),
                  _const_spec((ms, k), (0, 0)),
                  pl.BlockSpec((ms, tn), lambda j, i: (0, j)),
                  pl.BlockSpec((ms, tn), lambda j, i: (0, j + nj)),
                  pl.BlockSpec((None, k, tn), lambda j, i: (layer, 0, j)),
                  pl.BlockSpec((None, k, tn), lambda j, i: (layer, 0, j)),
                  pl.BlockSpec((None, FFN_CONV_WIDTH, tn), lambda j, i: (layer, 0, j)),
                  pl.BlockSpec((None, 1, tn), lambda j, i: (layer, 0, j))],
        out_specs=[pl.BlockSpec((tm, tn), lambda j, i: (i, j)),
                   pl.BlockSpec((None, FFN_TAIL, tn), lambda j, i: (i, 0, j)),
                   pl.BlockSpec((ms, tn), lambda j, i: (0, j)),
                   pl.BlockSpec((ms, tn), lambda j, i: (0, j))],
        out_shape=[jax.ShapeDtypeStruct((m, dff), BF16),
                   jax.ShapeDtypeStruct((m // tm, FFN_TAIL, dff), F32),
                   jax.ShapeDtypeStruct((ms, dff), BF16),
                   jax.ShapeDtypeStruct((ms, dff), F32)],
        scratch_shapes=[pltpu.VMEM((k, tn), BF16), pltpu.VMEM((k, tn), BF16),
                        pltpu.VMEM((tm + FFN_TAIL, tn), F32)],
        compiler_params=_params(("arbitrary", "arbitrary"), vmem),
        name="ffn_up",
    )(xb, xb_s, past_s, past_s, wg, wu, wdw, _layer_vec(bdw))


def _ffn_down_body(h_ref, x_ref, hs_ref, xs_ref, wd_ref, bd_ref, s_ref, ss_ref, wdb_ref):
    @pl.when(pl.program_id(1) == 0)
    def _():
        wdb_ref[...] = wd_ref[...].astype(BF16)

    s_ref[...] = (ALPHA * x_ref[...] + bd_ref[...]
                  + jnp.dot(h_ref[...], wdb_ref[...], preferred_element_type=F32))

    @pl.when(pl.program_id(1) == pl.num_programs(1) - 1)
    def _():
        ss_ref[...] = (ALPHA * xs_ref[...] + bd_ref[...]
                       + jnp.dot(hs_ref[...], wdb_ref[...], preferred_element_type=F32))


def ffn_down(h, x, h_s, x_s, wd, bd, layer, tm, tn):
    m, dff = h.shape
    ms = h_s.shape[0]
    n = x.shape[1]
    vmem = 2 * tm * dff * 2 + ms * dff * 2 + 2 * dff * tn * 4 + dff * tn * 2 + 6 * tm * tn * 4 + 6 * ms * tn * 4
    return pl.pallas_call(
        _ffn_down_body,
        grid=(n // tn, m // tm),
        in_specs=[pl.BlockSpec((tm, dff), lambda j, i: (i, 0)),
                  pl.BlockSpec((tm, tn), lambda j, i: (i, j)),
                  _const_spec((ms, dff), (0, 0)),
                  pl.BlockSpec((ms, tn), lambda j, i: (0, j)),
                  pl.BlockSpec((None, dff, tn), lambda j, i: (layer, 0, j)),
                  pl.BlockSpec((None, 1, tn), lambda j, i: (layer, 0, j))],
        out_specs=[pl.BlockSpec((tm, tn), lambda j, i: (i, j)),
                   pl.BlockSpec((ms, tn), lambda j, i: (0, j))],
        out_shape=[jax.ShapeDtypeStruct((m, n), F32), jax.ShapeDtypeStruct((ms, n), F32)],
        scratch_shapes=[pltpu.VMEM((dff, tn), BF16)],
        compiler_params=_params(("arbitrary", "arbitrary"), vmem),
        name="ffn_down",
    )(h, x, h_s, x_s, wd, _layer_vec(bd))


def _ple_body(s_ref, p_ref, lg_ref, lb_ref, wg_ref, wp_ref, g_ref, o_ref, ob_ref):
    tm = o_ref.shape[0]
    sub = min(ROW_SUB, tm)
    for r0 in range(0, tm, sub):
        rows = pl.ds(r0, sub)
        x = _ln(s_ref[rows, :], lg_ref[...], lb_ref[...])
        gate = _sigmoid(jnp.dot(x.astype(BF16), wg_ref[...], preferred_element_type=F32))
        e = jnp.dot(p_ref[rows, :].astype(BF16), wp_ref[...], preferred_element_type=F32)
        e = e * lax.rsqrt(jnp.mean(e * e, axis=-1, keepdims=True) + LN_EPS) * g_ref[...]
        out = x + gate * e
        o_ref[rows, :] = out
        ob_ref[rows, :] = out.astype(BF16)


def ple(s, p, lg, lb, wg, wp, g, layer, tm):
    m, k = s.shape
    dp = p.shape[2]
    row = lambda i: (i, 0)
    vmem = 4 * tm * k * 4 + 2 * tm * k * 2 + 2 * tm * dp * 4 + k * k * 2 + dp * k * 2 + 6 * tm * k * 4
    return pl.pallas_call(
        _ple_body,
        grid=(m // tm,),
        in_specs=[pl.BlockSpec((tm, k), row),
                  pl.BlockSpec((None, tm, dp), lambda i: (layer, i, 0)),
                  _vec_spec(k, layer), _vec_spec(k, layer),
                  _const_spec((None, k, k), (layer, 0, 0)),
                  _const_spec((None, dp, k), (layer, 0, 0)),
                  _vec_spec(k, layer)],
        out_specs=[pl.BlockSpec((tm, k), row), pl.BlockSpec((tm, k), row)],
        out_shape=[jax.ShapeDtypeStruct((m, k), F32), jax.ShapeDtypeStruct((m, k), BF16)],
        compiler_params=_params(("arbitrary",), vmem),
        name="ple",
    )(s, p, _layer_vec(lg), _layer_vec(lb), wg, wp, _layer_vec(g))


WT_CHUNK = 512


def _stage_transposed(wt_ref, wb_ref):
    n = wt_ref.shape[0]
    chunk = min(WT_CHUNK, n)
    for c0 in range(0, n, chunk):
        wb_ref[:, pl.ds(c0, chunk)] = wt_ref[pl.ds(c0, chunk), :].T.astype(BF16)


def _mm_qkv_body(q_tiles, x_ref, wt_ref, o_ref, wb_ref):
    @pl.when(pl.program_id(1) == 0)
    def _():
        _stage_transposed(wt_ref, wb_ref)

    z = jnp.dot(x_ref[...], wb_ref[...], preferred_element_type=F32)
    scale = jnp.where(pl.program_id(0) < q_tiles, D_QK ** -0.5, 1.0).astype(F32)
    o_ref[...] = (z * scale).astype(BF16)


def mm_qkv(xb, wt, layer, n, tm, tn):
    m, k = xb.shape
    vmem = 2 * tm * k * 2 + 2 * k * tn * 4 + k * tn * 2 + 2 * tm * tn * 2 + 3 * tm * tn * 4 + WT_CHUNK * k * 8
    return pl.pallas_call(
        functools.partial(_mm_qkv_body, (N_HEADS * D_QK) // tn),
        grid=(n // tn, m // tm),
        in_specs=[pl.BlockSpec((tm, k), lambda j, i: (i, 0)),
                  pl.BlockSpec((None, tn, k), lambda j, i: (layer, j, 0))],
        out_specs=pl.BlockSpec((tm, tn), lambda j, i: (i, j)),
        out_shape=jax.ShapeDtypeStruct((m, n), BF16),
        scratch_shapes=[pltpu.VMEM((k, tn), BF16)],
        compiler_params=_params(("arbitrary", "arbitrary"), vmem),
        name="mm_qkv",
    )(xb, wt)


def _mm_o_gates_body(x_ref, wot_ref, wgt_ref, o_ref, gt_ref, wob_ref, wgb_ref):
    @pl.when(pl.program_id(0) == 0)
    def _():
        _stage_transposed(wot_ref, wob_ref)
        wg = wgt_ref[...].T
        lane = lax.broadcasted_iota(jnp.int32, wg.shape, 1)
        first = lane < N_HEADS
        wgb_ref[:, pl.ds(0, GATE_PAD)] = jnp.where(first, wg, 0.0).astype(BF16)
        wgb_ref[:, pl.ds(GATE_PAD, GATE_PAD)] = jnp.where(
            first, pltpu.roll(wg, GATE_PAD - N_HEADS, axis=1), 0.0).astype(BF16)

    xb = x_ref[...]
    o_ref[...] = jnp.dot(xb, wob_ref[...], preferred_element_type=F32)
    gt_ref[...] = jnp.dot(xb, wgb_ref[...], preferred_element_type=F32)


def mm_o_gates(xb, wt, layer, tm):
    m, k = xb.shape
    n = N_HEADS * D_V
    o_row = 2 * N_HEADS * D_QK + N_HEADS * D_V
    row = lambda i: (i, 0)
    vmem = (2 * tm * k * 2 + k * n * 4 + k * n * 2 + GATE_PAD * k * 6 + 2 * tm * (n + GATE_PAD) * 4
            + 2 * tm * n * 4 + WT_CHUNK * k * 8)
    return pl.pallas_call(
        _mm_o_gates_body,
        grid=(m // tm,),
        in_specs=[pl.BlockSpec((tm, k), row),
                  _const_spec((None, n, k), (layer, o_row // n, 0)),
                  _const_spec((None, GATE_PAD, k), (layer, (o_row + n) // GATE_PAD, 0))],
        out_specs=[pl.BlockSpec((tm, n), row), pl.BlockSpec((tm, 2 * GATE_PAD), row)],
        out_shape=[jax.ShapeDtypeStruct((m, n), F32), jax.ShapeDtypeStruct((m, 2 * GATE_PAD), F32)],
        scratch_shapes=[pltpu.VMEM((k, n), BF16), pltpu.VMEM((k, 2 * GATE_PAD), BF16)],
        compiler_params=_params(("arbitrary",), vmem),
        name="mm_o_gates",
    )(xb, wt, wt)


def _mlstm_chunk_body(q_ref, k_ref, v_ref, o_ref, gi_ref, gf_ref, gr_ref, bi_ref, bf_ref, bcol_ref,
                      hng_ref, out_ref, c_ref, m_ref, s_ref, tot_ref):
    length = q_ref.shape[0]

    @pl.when(pl.program_id(1) == 0)
    def _():
        c_ref[...] = jnp.zeros(c_ref.shape, F32)
        m_ref[...] = jnp.zeros(m_ref.shape, F32)

    row = lax.broadcasted_iota(jnp.int32, (length, length), 0)
    col = lax.broadcasted_iota(jnp.int32, (length, length), 1)
    causal = row >= col
    triu = (row <= col).astype(F32)

    li_col = gi_ref[...] + bi_ref[...]
    bh_col = _row_scan(_log_sigmoid(gf_ref[...] + bf_ref[...]), jnp.add, 0.0)
    cm_col = _row_scan(li_col - bh_col, jnp.maximum, -jnp.inf)
    grow = gr_ref[...] + bcol_ref[...]
    li_row = grow[0:N_HEADS, :]
    bh_row = jnp.dot(_log_sigmoid(grow[N_HEADS:2 * N_HEADS, :]), triu,
                     precision=lax.Precision.HIGHEST, preferred_element_type=F32)
    a_row = li_row - bh_row
    inter = bh_col + m_ref[0:1, :]
    m_tok = jnp.maximum(inter, bh_col + cm_col)
    w_inter = jnp.exp(inter - m_tok)
    emt = jnp.exp(-m_tok)
    e_col = bh_col - m_tok
    m_new = m_tok[length - 1:length, :]
    g_state = w_inter[length - 1:length, :]
    g_tok = jnp.exp(bh_col[length - 1:length, :] - bh_col + li_col - m_new)
    m_ref[...] = jnp.broadcast_to(m_new, m_ref.shape)
    ones = jnp.ones((length, LANES), BF16)

    for h in range(N_HEADS):
        qk = slice(h * D_QK, (h + 1) * D_QK)
        w_intra = jnp.exp(jnp.where(causal, e_col[:, h:h + 1] + a_row[h:h + 1, :], -jnp.inf))
        s = lax.dot_general(q_ref[:, qk], k_ref[:, qk], (((1,), (1,)), ((), ())),
                            preferred_element_type=F32) * w_intra
        s_ref[h] = s.astype(BF16)

    for h in range(N_HEADS):
        qk = slice(h * D_QK, (h + 1) * D_QK)
        vaug = jnp.concatenate([v_ref[:, h * D_V:(h + 1) * D_V], ones], axis=1)
        c_old = c_ref[h]
        tot_ref[h] = (jnp.dot(s_ref[h], vaug, preferred_element_type=F32)
                      + jnp.dot(q_ref[:, qk], c_old.astype(BF16), preferred_element_type=F32)
                      * w_inter[:, h:h + 1])
        kg = (k_ref[:, qk].astype(F32) * g_tok[:, h:h + 1]).astype(BF16)
        c_ref[h] = g_state[:, h:h + 1] * c_old + lax.dot_general(
            kg, vaug, (((0,), (0,)), ((), ())), preferred_element_type=F32)

    for h in range(N_HEADS):
        tot = tot_ref[h]
        den = jnp.maximum(jnp.abs(tot[:, D_V:D_V + LANES]), emt[:, h:h + 1])
        inv = 1.0 / den
        halves = [tot[:, c0:c0 + LANES] * inv for c0 in range(0, D_V, LANES)]
        moments = jnp.concatenate([sum(halves), sum(x * x for x in halves)], axis=0).astype(BF16)
        stat = jnp.dot(moments, jnp.ones((LANES, LANES), BF16), preferred_element_type=F32) * (1.0 / D_V)
        mu = stat[0:length, :]
        rstd = lax.rsqrt(stat[length:2 * length, :] - mu * mu + LN_EPS)
        for idx, x in enumerate(halves):
            cols = pl.ds(h * D_V + idx * LANES, LANES)
            out_ref[:, cols] = (_sigmoid(o_ref[:, cols]) * ((x - mu) * rstd) * hng_ref[:, cols]).astype(BF16)


FORGET_PAD_BIAS = 30.0


def mlstm_chunk(qkv, o, gates, b_gates, hn_g, bsz, seq_len):
    length = MLSTM_CHUNK
    nch = seq_len // length
    hq = N_HEADS * D_QK
    hv = N_HEADS * D_V
    rows = lambda b, c: (b * nch + c, 0)
    pad = GATE_PAD - N_HEADS
    bi_row = jnp.pad(b_gates[0].reshape(1, N_HEADS), ((0, 0), (0, pad)))
    bf_row = jnp.pad(b_gates[1].reshape(1, N_HEADS), ((0, 0), (0, pad)), constant_values=FORGET_PAD_BIAS)
    bcol = b_gates.reshape(2 * N_HEADS, 1)
    gates_t = jnp.concatenate([gates[:, :N_HEADS], gates[:, GATE_PAD:GATE_PAD + N_HEADS]], axis=1).T
    const2 = lambda b, c: (0, 0)
    out, c_aug, m_new = pl.pallas_call(
        _mlstm_chunk_body,
        grid=(bsz, nch),
        in_specs=[pl.BlockSpec((length, hq), rows),
                  pl.BlockSpec((length, hq), lambda b, c: (b * nch + c, 1)),
                  pl.BlockSpec((length, hv), lambda b, c: (b * nch + c, 1)),
                  pl.BlockSpec((length, hv), rows),
                  pl.BlockSpec((length, GATE_PAD), rows),
                  pl.BlockSpec((length, GATE_PAD), lambda b, c: (b * nch + c, 1)),
                  pl.BlockSpec((2 * N_HEADS, length), lambda b, c: (0, b * nch + c)),
                  pl.BlockSpec((1, GATE_PAD), const2),
                  pl.BlockSpec((1, GATE_PAD), const2),
                  pl.BlockSpec((2 * N_HEADS, 1), const2),
                  pl.BlockSpec((1, hv), const2)],
        out_specs=[pl.BlockSpec((length, hv), rows),
                   pl.BlockSpec((None, N_HEADS, D_QK, D_V + LANES), lambda b, c: (b, 0, 0, 0)),
                   pl.BlockSpec((None, SUBLANES, GATE_PAD), lambda b, c: (b, 0, 0))],
        out_shape=[jax.ShapeDtypeStruct((bsz * seq_len, hv), BF16),
                   jax.ShapeDtypeStruct((bsz, N_HEADS, D_QK, D_V + LANES), F32),
                   jax.ShapeDtypeStruct((bsz, SUBLANES, GATE_PAD), F32)],
        scratch_shapes=[pltpu.VMEM((N_HEADS, length, length), BF16),
                        pltpu.VMEM((N_HEADS, length, D_V + LANES), F32)],
        compiler_params=_params(("arbitrary", "arbitrary"), 24 << 20),
        name="mlstm_chunk",
    )(qkv, qkv, qkv, o, gates, gates, gates_t, bi_row, bf_row, bcol, hn_g.reshape(1, hv))
    return out, c_aug[..., :D_V], c_aug[..., D_V], m_new[:, 0, :N_HEADS]


def _mlstm_step_body(q_ref, k_ref, kt_ref, v_ref, o_ref, gi_ref, gf_ref, c0_ref, n0_ref, m0_ref,
                     bi_ref, bf_ref, hng_ref, out_ref, c_ref, n_ref, m_ref, wv_ref, wi_ref):
    bb = q_ref.shape[0]
    li = gi_ref[...] + bi_ref[...]
    inter = _log_sigmoid(gf_ref[...] + bf_ref[...]) + m0_ref[...]
    m_tok = jnp.maximum(inter, li)
    w_intra = jnp.exp(li - m_tok)
    w_inter = jnp.exp(inter - m_tok)
    q = q_ref[...]
    k = k_ref[...]
    n_old = n0_ref[...]
    v = v_ref[...]
    s = jnp.sum(q * k, axis=-1, keepdims=True) * w_intra
    den = s + jnp.sum(q * n_old, axis=-1, keepdims=True) * w_inter
    den = jnp.maximum(jnp.abs(den), jnp.exp(-m_tok))
    n_ref[...] = w_inter * n_old + w_intra * k
    m_ref[...] = m_tok
    wv_ref[...] = w_intra * v
    wi_ref[...] = jnp.broadcast_to(w_inter, wi_ref.shape)

    head_row = lax.broadcasted_iota(jnp.int32, (N_HEADS, D_V), 0)
    qb16 = q.astype(BF16)
    qc_rows = []
    for b in range(bb):
        qc = jnp.zeros((N_HEADS, D_V), F32)
        for h in range(N_HEADS):
            c_old = c0_ref[b, h]
            all_heads = jnp.dot(qb16[b], c_old.astype(BF16), preferred_element_type=F32)
            qc = jnp.where(head_row == h, all_heads, qc)
            c_ref[b, h] = (wi_ref[b, pl.ds(h, 1), :] * c_old
                           + kt_ref[b, :, h:h + 1] * wv_ref[b, pl.ds(h, 1), :])
        qc_rows.append(qc)
    qc_all = jnp.stack(qc_rows, axis=0)
    hv = (s * v + qc_all * w_inter) * (1.0 / den)
    mu = jnp.mean(hv, axis=-1, keepdims=True)
    hc = hv - mu
    var = jnp.mean(hc * hc, axis=-1, keepdims=True)
    hn = hc * lax.rsqrt(var + LN_EPS) * hng_ref[...][None]
    out_ref[...] = _sigmoid(o_ref[...]) * hn


def mlstm_step(q, k, v, o, gates, c0, n0, m0, b_gates, hn_g, layer, bb):
    bsz = q.shape[0]
    kt = jnp.swapaxes(k, 1, 2)
    gi = gates[:, :N_HEADS, None]
    gf = gates[:, GATE_PAD:GATE_PAD + N_HEADS, None]
    m0 = m0[..., None]
    b3 = lambda i: (i, 0, 0)
    st3 = lambda i: (layer, i, 0, 0)
    bias = lambda v: v.reshape(N_HEADS, 1)
    out, c_new, n_new, m_new = pl.pallas_call(
        _mlstm_step_body,
        grid=(bsz // bb,),
        in_specs=[pl.BlockSpec((bb, N_HEADS, D_QK), b3), pl.BlockSpec((bb, N_HEADS, D_QK), b3),
                  pl.BlockSpec((bb, D_QK, N_HEADS), b3),
                  pl.BlockSpec((bb, N_HEADS, D_V), b3), pl.BlockSpec((bb, N_HEADS, D_V), b3),
                  pl.BlockSpec((bb, N_HEADS, 1), b3), pl.BlockSpec((bb, N_HEADS, 1), b3),
                  pl.BlockSpec((None, bb, N_HEADS, D_QK, D_V), lambda i: (layer, i, 0, 0, 0)),
                  pl.BlockSpec((None, bb, N_HEADS, D_QK), st3),
                  pl.BlockSpec((None, bb, N_HEADS, 1), st3),
                  _const_spec((N_HEADS, 1), (0, 0)), _const_spec((N_HEADS, 1), (0, 0)),
                  _const_spec((N_HEADS, D_V), (0, 0))],
        out_specs=[pl.BlockSpec((bb, N_HEADS, D_V), b3),
                   pl.BlockSpec((None, bb, N_HEADS, D_QK, D_V), lambda i: (0, i, 0, 0, 0)),
                   pl.BlockSpec((None, bb, N_HEADS, D_QK), lambda i: (0, i, 0, 0)),
                   pl.BlockSpec((None, bb, N_HEADS, 1), lambda i: (0, i, 0, 0))],
        out_shape=[jax.ShapeDtypeStruct((bsz, N_HEADS, D_V), F32),
                   jax.ShapeDtypeStruct((1, bsz, N_HEADS, D_QK, D_V), F32),
                   jax.ShapeDtypeStruct((1, bsz, N_HEADS, D_QK), F32),
                   jax.ShapeDtypeStruct((1, bsz, N_HEADS, 1), F32)],
        scratch_shapes=[pltpu.VMEM((bb, N_HEADS, D_V), F32), pltpu.VMEM((bb, N_HEADS, D_V), F32)],
        compiler_params=_params(("arbitrary",), 4 * bb * N_HEADS * D_QK * D_V * 4 + (8 << 20)),
        name="mlstm_step",
    )(q, k, kt, v, o, gi, gf, c0, n0, m0, bias(b_gates[0]), bias(b_gates[1]),
      hn_g.reshape(N_HEADS, D_V))
    return out, c_new, n_new, m_new[..., 0]


def _tiles(rows):
    big = rows >= 1024
    return dict(glu=1024 if big else rows, row=512 if big else rows, up=2048 if big else rows,
                down=512 if big else rows, qkv=1024 if big else rows)


def _trunk(xp, xs, pp, ps, bp, sp, states, w):
    conv_state_t, c_st, n_st, m_st, ffn_state = states
    bs = xs.shape[0]
    tp, ts = _tiles(xp.shape[0]), _tiles(bs)
    hq = N_HEADS * D_QK
    hv = N_HEADS * D_V
    new_ffn_p, new_ffn_s = [], []

    def ffn_block(p32, pb, s32, sb, i):
        hp, gtp, hs, gts = ffn_up(pb, sb, ffn_state[i].reshape(bs, 2 * D_FF), w["ff_w_gate"], w["ff_w_up"],
                                  w["ff_w_dw"], w["ff_b_dw"], i, tp["up"], 512, sp)
        tps = sp // tp["up"]
        new_ffn_p.append(gtp.reshape(bp, tps, FFN_TAIL, D_FF)[:, tps - 1, FFN_TAIL - 2:, :])
        new_ffn_s.append(jnp.stack([ffn_state[i][:, 1, :], gts], axis=1))
        sump, sums = ffn_down(hp, p32, hs, s32, w["ff_w_down"], w["ff_b_down"], i, tp["down"], 512)
        norm = (w["ln_ffn_g"], w["ln_ffn_b"], w["pl_w_gate"], w["pl_w_proj"], w["pl_g"], i)
        return ple(sump, pp, *norm, tp["row"]) + ple(sums, ps, *norm, ts["row"])

    yp, tails = mm_glu_conv(xp, w["cv_w_in"], w["cv_b_in"], w["cv_w_dw"], w["cv_b_dw"], 0,
                            tp["glu"], 512, sp)
    tps = sp // tp["glu"]
    hist = CONV_WIDTH - 1
    conv_p = tails.reshape(bp, tps, CONV_HALO, D_MODEL)[:, tps - 1, CONV_HALO - hist:, :][None]
    us = mm_glu(xs, w["cv_w_in"], w["cv_b_in"], 0, ts["glu"], 512)
    ys, conv_s_t = conv_sample(conv_state_t, us, w["cv_w_dw"], w["cv_b_dw"], 0, 256)
    conv_s = jnp.swapaxes(conv_s_t, 1, 2)
    mix = (w["cv_w_out"], w["cv_b_out"], w["ln_mix_g"], w["ln_mix_b"], 0, 0)
    pre = (w["cv_ln_g"], w["cv_ln_b"])
    p32, pb = mm_ln(yp, xp, *mix, tp["row"], pre=pre)
    s32, sb = mm_ln(ys, xs, *mix, ts["row"], pre=pre)
    p32, pb, s32, sb = ffn_block(p32, pb, s32, sb, 0)

    qkv_p = mm_qkv(pb, w["ml_w_in_t"], 0, 2 * hq + hv, tp["qkv"], 1024)
    o_p, gates_p = mm_o_gates(pb, w["ml_w_in_t"], 0, tp["row"])
    mix_p, c_p, n_p, m_p = mlstm_chunk(qkv_p, o_p, gates_p, w["ml_b_gates"][0], w["ml_hn_g"][0], bp, sp)
    qkv_s = mm_qkv(sb, w["ml_w_in_t"], 0, 2 * hq + hv, ts["qkv"], 1024)
    o_s, gates_s = mm_o_gates(sb, w["ml_w_in_t"], 0, ts["row"])
    qf = qkv_s[:, :hq].astype(F32).reshape(bs, N_HEADS, D_QK)
    kf = qkv_s[:, hq:2 * hq].astype(F32).reshape(bs, N_HEADS, D_QK)
    vf = qkv_s[:, 2 * hq:].astype(F32).reshape(bs, N_HEADS, D_V)
    mix_s, c_s, n_s, m_s = mlstm_step(qf, kf, vf, o_s.reshape(bs, N_HEADS, D_V), gates_s, c_st, n_st, m_st,
                                      w["ml_b_gates"][0], w["ml_hn_g"][0], 0, 4)
    mix = (w["ml_w_out"], w["ml_b_out"], w["ln_mix_g"], w["ln_mix_b"], 0, 1)
    p32, pb = mm_ln(mix_p, p32, *mix, tp["row"])
    s32, sb = mm_ln(mix_s.reshape(bs, hv), s32, *mix, ts["row"])
    p32, _, s32, _ = ffn_block(p32, pb, s32, sb, 1)
    return (p32, s32, conv_p, conv_s, c_p[None], n_p[None], m_p[None], c_s, n_s, m_s,
            jnp.stack(new_ffn_p), jnp.stack(new_ffn_s))


def kernel(x_prompt, x_sample, p_prompt, p_sample, state_conv, state_mlstm_c, state_mlstm_n,
           state_mlstm_m, state_ffn_conv,
           cv_w_in, cv_b_in, cv_w_dw, cv_b_dw, cv_ln_g, cv_ln_b, cv_w_out, cv_b_out,
           ml_w_in, ml_b_gates, ml_hn_g, ml_w_out, ml_b_out,
           ln_mix_g, ln_mix_b, ln_ffn_g, ln_ffn_b,
           ff_w_gate, ff_w_up, ff_w_dw, ff_b_dw, ff_w_down, ff_b_down,
           pl_w_proj, pl_g, pl_w_gate):
    bp, sp, _ = x_prompt.shape
    bs, ss, _ = x_sample.shape
    w = dict(
        cv_w_in=cv_w_in, cv_b_in=cv_b_in, cv_w_dw=cv_w_dw, cv_b_dw=cv_b_dw,
        cv_ln_g=cv_ln_g, cv_ln_b=cv_ln_b, cv_w_out=cv_w_out.astype(BF16), cv_b_out=cv_b_out,
        ml_w_in_t=jnp.swapaxes(ml_w_in, 1, 2),
        ml_b_gates=ml_b_gates, ml_hn_g=ml_hn_g, ml_w_out=ml_w_out.astype(BF16), ml_b_out=ml_b_out,
        ln_mix_g=ln_mix_g, ln_mix_b=ln_mix_b, ln_ffn_g=ln_ffn_g, ln_ffn_b=ln_ffn_b,
        ff_w_gate=ff_w_gate, ff_w_up=ff_w_up, ff_w_dw=ff_w_dw,
        ff_b_dw=ff_b_dw, ff_w_down=ff_w_down, ff_b_down=ff_b_down,
        pl_w_proj=pl_w_proj.astype(BF16), pl_g=pl_g, pl_w_gate=pl_w_gate.astype(BF16),
    )
    yp, ys, conv_p, conv_s, c_p, n_p, m_p, c_s, n_s, m_s, ffn_p, ffn_s = _trunk(
        x_prompt.reshape(bp * sp, D_MODEL), x_sample.reshape(bs * ss, D_MODEL),
        p_prompt.reshape(DEPTH, bp * sp, D_PLE), p_sample.reshape(DEPTH, bs * ss, D_PLE), bp, sp,
        (jnp.swapaxes(state_conv, 1, 2), state_mlstm_c, state_mlstm_n, state_mlstm_m, state_ffn_conv), w)
    return (yp.reshape(bp, sp, D_MODEL), ys.reshape(bs, ss, D_MODEL), conv_p, conv_s,
            c_p, n_p, m_p, c_s, n_s, m_s, ffn_p, ffn_s)
```

```python
import functools

import jax
import jax.numpy as jnp
from jax import lax
from jax.experimental import pallas as pl
from jax.experimental.pallas import tpu as pltpu

D_MODEL = 2048
D_FF = 5632
D_PLE = 256
N_HEADS = 8
D_QK = 128
D_V = 256
CONV_WIDTH = 31
FFN_CONV_WIDTH = 3
DEPTH = 2
ALPHA = (2.0 * DEPTH) ** 0.25
LN_EPS = 1e-5
MLSTM_CHUNK = 128
GATE_PAD = 128
SUBLANES = 8

F32 = jnp.float32
BF16 = jnp.bfloat16

VMEM_CAP_BYTES = 58 * 1024 * 1024
VMEM_SLACK_BYTES = 8 * 1024 * 1024


def _params(sem, vmem_bytes):
    return pltpu.CompilerParams(
        dimension_semantics=sem,
        vmem_limit_bytes=min(int(vmem_bytes) + VMEM_SLACK_BYTES, VMEM_CAP_BYTES))


def _ln(x, g, b):
    mu = jnp.mean(x, axis=-1, keepdims=True)
    xc = x - mu
    var = jnp.mean(xc * xc, axis=-1, keepdims=True)
    return xc * lax.rsqrt(var + LN_EPS) * g + b


def _sigmoid(x):
    return 1.0 / (1.0 + jnp.exp(-x))


def _log_sigmoid(x):
    return jnp.minimum(x, 0.0) - jnp.log(1.0 + jnp.exp(-jnp.abs(x)))


def _row_scan(x, op, identity):
    rows, cols = x.shape
    shift = 1
    while shift < rows:
        shifted = jnp.concatenate([jnp.full((shift, cols), identity, x.dtype), x[:rows - shift, :]], axis=0)
        x = op(x, shifted)
        shift *= 2
    return x


def _const_spec(shape, index):
    return pl.BlockSpec(shape, lambda *_: index, pipeline_mode=pl.Buffered(1))


def _layer_vec(v):
    return v.reshape(v.shape[0], 1, v.shape[1])


def _vec_spec(n, layer):
    return _const_spec((None, 1, n), (layer, 0, 0))


CONV_ROWS = 64
CONV_HALO = 32
LANES = 128
GLU_SUB_ROWS = 256


def _conv_chunk(pad_ref, w_ref, b_ref, y_ref, row0, lane0):
    lanes = pl.ds(lane0, LANES)
    win = CONV_ROWS + SUBLANES
    acc = jnp.broadcast_to(b_ref[:, lanes], (CONV_ROWS, LANES))
    for r in range(SUBLANES):
        part = None
        for a in range((CONV_WIDTH - 1 - r) // SUBLANES + 1):
            d = SUBLANES * a + r
            start = row0 + CONV_HALO - SUBLANES * (a + 1)
            term = pad_ref[pl.ds(start, win), lanes] * w_ref[pl.ds(CONV_WIDTH - 1 - d, 1), lanes]
            part = term if part is None else part + term
        acc = acc + part[SUBLANES - r:SUBLANES - r + CONV_ROWS, :]
    y_ref[pl.ds(row0, CONV_ROWS), lanes] = acc


def _glu(xb, wab_ref, wgb_ref, ba_ref, bg_ref):
    a = jnp.dot(xb, wab_ref[...], preferred_element_type=F32) + ba_ref[...]
    g = jnp.dot(xb, wgb_ref[...], preferred_element_type=F32) + bg_ref[...]
    return a * _sigmoid(g)


def _mm_glu_conv_body(tiles_per_seq, x_ref, xs_ref, wa_ref, wg_ref, ba_ref, bg_ref, wdw_ref, bdw_ref,
                      y_ref, tail_ref, us_ref, wab_ref, wgb_ref, pad_ref):
    i = pl.program_id(1)
    tm, tn = y_ref.shape

    @pl.when(i == 0)
    def _():
        wab_ref[...] = wa_ref[...].astype(BF16)
        wgb_ref[...] = wg_ref[...].astype(BF16)

    @pl.when((i % tiles_per_seq) == 0)
    def _():
        pad_ref[pl.ds(0, CONV_HALO), :] = jnp.zeros((CONV_HALO, tn), F32)

    for r0 in range(0, tm, GLU_SUB_ROWS):
        xb = x_ref[pl.ds(r0, GLU_SUB_ROWS), :].astype(BF16)
        pad_ref[pl.ds(CONV_HALO + r0, GLU_SUB_ROWS), :] = _glu(xb, wab_ref, wgb_ref, ba_ref, bg_ref)
        for c0 in range(r0, r0 + GLU_SUB_ROWS, CONV_ROWS):
            for l0 in range(0, tn, LANES):
                _conv_chunk(pad_ref, wdw_ref, bdw_ref, y_ref, c0, l0)
    tail = pad_ref[pl.ds(tm, CONV_HALO), :]
    tail_ref[...] = tail
    pad_ref[pl.ds(0, CONV_HALO), :] = tail

    @pl.when(i == pl.num_programs(1) - 1)
    def _():
        us_ref[...] = _glu(xs_ref[...].astype(BF16), wab_ref, wgb_ref, ba_ref, bg_ref)


def mm_glu_conv(x, x_s, w, b, wdw, bdw, layer, tm, tn, seq_len):
    m, k = x.shape
    ms = x_s.shape[0]
    n = w.shape[2] // 2
    nj = n // tn
    b3 = _layer_vec(b)
    vmem = (2 * tm * k * 4 + ms * k * 4 + 4 * k * tn * 4 + 2 * k * tn * 2 + 2 * tm * tn * 4
            + (tm + CONV_HALO) * tn * 4 + GLU_SUB_ROWS * k * 2 + 6 * GLU_SUB_ROWS * tn * 4 + 8 * ms * tn * 4)
    return pl.pallas_call(
        functools.partial(_mm_glu_conv_body, seq_len // tm),
        grid=(nj, m // tm),
        in_specs=[pl.BlockSpec((tm, k), lambda j, i: (i, 0)),
                  _const_spec((ms, k), (0, 0)),
                  pl.BlockSpec((None, k, tn), lambda j, i: (layer, 0, j)),
                  pl.BlockSpec((None, k, tn), lambda j, i: (layer, 0, j + nj)),
                  pl.BlockSpec((None, 1, tn), lambda j, i: (layer, 0, j)),
                  pl.BlockSpec((None, 1, tn), lambda j, i: (layer, 0, j + nj)),
                  pl.BlockSpec((None, CONV_WIDTH, tn), lambda j, i: (layer, 0, j)),
                  pl.BlockSpec((None, 1, tn), lambda j, i: (layer, 0, j))],
        out_specs=[pl.BlockSpec((tm, tn), lambda j, i: (i, j)),
                   pl.BlockSpec((None, CONV_HALO, tn), lambda j, i: (i, 0, j)),
                   pl.BlockSpec((ms, tn), lambda j, i: (0, j))],
        out_shape=[jax.ShapeDtypeStruct((m, n), F32),
                   jax.ShapeDtypeStruct((m // tm, CONV_HALO, n), F32),
                   jax.ShapeDtypeStruct((ms, n), F32)],
        scratch_shapes=[pltpu.VMEM((k, tn), BF16), pltpu.VMEM((k, tn), BF16),
                        pltpu.VMEM((tm + CONV_HALO, tn), F32)],
        compiler_params=_params(("arbitrary", "arbitrary"), vmem),
        name="mm_glu_conv",
    )(x, x_s, w, w, b3, b3, wdw, _layer_vec(bdw))


def _conv_sample_body(p_ref, u_ref, w_ref, b_ref, y_ref, np_ref):
    hist = p_ref.shape[0]
    u = u_ref[...]
    acc = u * w_ref[pl.ds(hist, 1), :] + b_ref[...]
    for j in range(hist):
        acc = acc + p_ref[j] * w_ref[pl.ds(j, 1), :]
    y_ref[...] = acc
    for j in range(hist - 1):
        np_ref[j] = p_ref[j + 1]
    np_ref[hist - 1] = u


def conv_sample(past_t, u, w, b, layer, tc):
    _, hist, bsz, c = past_t.shape
    return pl.pallas_call(
        _conv_sample_body,
        grid=(c // tc,),
        in_specs=[pl.BlockSpec((None, hist, bsz, tc), lambda i: (layer, 0, 0, i)),
                  pl.BlockSpec((bsz, tc), lambda i: (0, i)),
                  pl.BlockSpec((None, CONV_WIDTH, tc), lambda i: (layer, 0, i)),
                  pl.BlockSpec((None, 1, tc), lambda i: (layer, 0, i))],
        out_specs=[pl.BlockSpec((bsz, tc), lambda i: (0, i)),
                   pl.BlockSpec((None, hist, bsz, tc), lambda i: (0, 0, 0, i))],
        out_shape=[jax.ShapeDtypeStruct((bsz, c), F32),
                   jax.ShapeDtypeStruct((1, hist, bsz, c), F32)],
        compiler_params=_params(("arbitrary",), 5 * hist * bsz * tc * 4),
        name="conv_sample",
    )(past_t, u, w, _layer_vec(b))


ROW_SUB = 256


def _mm_ln_body(pre_norm, *refs):
    if pre_norm:
        xin_ref, res_ref, w_ref, b_ref, pg_ref, pb_ref, g_ref, be_ref, o_ref, ob_ref = refs
    else:
        xin_ref, res_ref, w_ref, b_ref, g_ref, be_ref, o_ref, ob_ref = refs
    tm = o_ref.shape[0]
    sub = min(ROW_SUB, tm)
    for r0 in range(0, tm, sub):
        rows = pl.ds(r0, sub)
        if pre_norm:
            t = _ln(xin_ref[rows, :].astype(F32), pg_ref[...], pb_ref[...])
            xb = (t * _sigmoid(t)).astype(BF16)
        else:
            xb = xin_ref[rows, :].astype(BF16)
        mix = jnp.dot(xb, w_ref[...], preferred_element_type=F32) + b_ref[...]
        out = _ln(ALPHA * res_ref[rows, :] + mix, g_ref[...], be_ref[...])
        o_ref[rows, :] = out
        ob_ref[rows, :] = out.astype(BF16)


def mm_ln(xin, resid, w, b, g, be, mix_layer, ln_layer, tm, pre=None):
    m, k = xin.shape
    n = w.shape[2]
    row = lambda i: (i, 0)
    in_specs = [pl.BlockSpec((tm, k), row), pl.BlockSpec((tm, n), row),
                _const_spec((None, k, n), (mix_layer, 0, 0)), _vec_spec(n, mix_layer)]
    args = [xin, resid, w, _layer_vec(b)]
    if pre is not None:
        in_specs += [_vec_spec(k, mix_layer), _vec_spec(k, mix_layer)]
        args += [_layer_vec(pre[0]), _layer_vec(pre[1])]
    in_specs += [_vec_spec(n, ln_layer), _vec_spec(n, ln_layer)]
    args += [_layer_vec(g), _layer_vec(be)]
    vmem = 2 * tm * k * xin.dtype.itemsize + 4 * tm * n * 4 + 2 * tm * n * 2 + k * n * 2 + 6 * tm * n * 4
    return pl.pallas_call(
        functools.partial(_mm_ln_body, pre is not None),
        grid=(m // tm,),
        in_specs=in_specs,
        out_specs=[pl.BlockSpec((tm, n), row), pl.BlockSpec((tm, n), row)],
        out_shape=[jax.ShapeDtypeStruct((m, n), F32), jax.ShapeDtypeStruct((m, n), BF16)],
        compiler_params=_params(("arbitrary",), vmem),
        name="mm_ln",
    )(*args)


FFN_TAIL = 8
FFN_SUB_ROWS = 256


def _ffn_hidden(g, g1, g2, up, wdw_ref, bdw_ref):
    gc = (wdw_ref[pl.ds(0, 1), :] * g2 + wdw_ref[pl.ds(1, 1), :] * g1 + wdw_ref[pl.ds(2, 1), :] * g
          + bdw_ref[...])
    return (gc * _sigmoid(gc) * up).astype(BF16)


def _ffn_up_body(tiles_per_seq, sub_rows, x_ref, xs_ref, p2_ref, p1_ref, wg_ref, wu_ref, wdw_ref,
                 bdw_ref, h_ref, gt_ref, hs_ref, gts_ref, wgb_ref, wub_ref, gs_ref):
    i = pl.program_id(1)
    tm, tn = h_ref.shape

    @pl.when(i == 0)
    def _():
        wgb_ref[...] = wg_ref[...].astype(BF16)
        wub_ref[...] = wu_ref[...].astype(BF16)

    @pl.when((i % tiles_per_seq) == 0)
    def _():
        gs_ref[pl.ds(0, FFN_TAIL), :] = jnp.zeros((FFN_TAIL, tn), F32)

    for r0 in range(0, tm, sub_rows):
        xr = x_ref[pl.ds(r0, sub_rows), :]
        g = jnp.dot(xr, wgb_ref[...], preferred_element_type=F32)
        up = jnp.dot(xr, wub_ref[...], preferred_element_type=F32)
        gs_ref[pl.ds(FFN_TAIL + r0, sub_rows), :] = g
        g1 = gs_ref[pl.ds(FFN_TAIL + r0 - 1, sub_rows), :]
        g2 = gs_ref[pl.ds(FFN_TAIL + r0 - 2, sub_rows), :]
        h_ref[pl.ds(r0, sub_rows), :] = _ffn_hidden(g, g1, g2, up, wdw_ref, bdw_ref)
    tail = gs_ref[pl.ds(tm, FFN_TAIL), :]
    gt_ref[...] = tail
    gs_ref[pl.ds(0, FFN_TAIL), :] = tail

    @pl.when(i == pl.num_programs(1) - 1)
    def _():
        xs = xs_ref[...]
        g = jnp.dot(xs, wgb_ref[...], preferred_element_type=F32)
        up = jnp.dot(xs, wub_ref[...], preferred_element_type=F32)
        gts_ref[...] = g
        hs_ref[...] = _ffn_hidden(g, p1_ref[...], p2_ref[...], up, wdw_ref, bdw_ref)


def ffn_up(xb, xb_s, past_s, wg, wu, wdw, bdw, layer, tm, tn, seq_len):
    m, k = xb.shape
    ms = xb_s.shape[0]
    dff = wg.shape[2]
    nj = dff // tn
    sub = min(FFN_SUB_ROWS, tm)
    vmem = (2 * tm * k * 2 + ms * k * 2 + 4 * k * tn * 4 + 2 * k * tn * 2 + 2 * tm * tn * 2 + tm * tn * 4
            + 8 * sub * tn * 4 + 16 * ms * tn * 4)
    return pl.pallas_call(
        functools.partial(_ffn_up_body, seq_len // tm, sub),
        grid=(nj, m // tm),
        in_specs=[pl.BlockSpec((tm, k), lambda j, i: (i, 0)),
                  _const_spec((ms, k), (0, 0)),
                  pl.BlockSpec((ms, tn), lambda j, i: (0, j)),
                  pl.BlockSpec((ms, tn), lambda j, i: (0, j + nj)),
                  pl.BlockSpec((None, k, tn), lambda j, i: (layer, 0, j)),
                  pl.BlockSpec((None, k, tn), lambda j, i: (layer, 0, j)),
                  pl.BlockSpec((None, FFN_CONV_WIDTH, tn), lambda j, i: (layer, 0, j)),
                  pl.BlockSpec((None, 1, tn), lambda j, i: (layer, 0, j))],
        out_specs=[pl.BlockSpec((tm, tn), lambda j, i: (i, j)),
                   pl.BlockSpec((None, FFN_TAIL, tn), lambda j, i: (i, 0, j)),
                   pl.BlockSpec((ms, tn), lambda j, i: (0, j)),
                   pl.BlockSpec((ms, tn), lambda j, i: (0, j))],
        out_shape=[jax.ShapeDtypeStruct((m, dff), BF16),
                   jax.ShapeDtypeStruct((m // tm, FFN_TAIL, dff), F32),
                   jax.ShapeDtypeStruct((ms, dff), BF16),
                   jax.ShapeDtypeStruct((ms, dff), F32)],
        scratch_shapes=[pltpu.VMEM((k, tn), BF16), pltpu.VMEM((k, tn), BF16),
                        pltpu.VMEM((tm + FFN_TAIL, tn), F32)],
        compiler_params=_params(("arbitrary", "arbitrary"), vmem),
        name="ffn_up",
    )(xb, xb_s, past_s, past_s, wg, wu, wdw, _layer_vec(bdw))


def _ffn_down_body(h_ref, hs_ref, wd_ref, bd_ref, f_ref, fs_ref, wdb_ref):
    @pl.when(pl.program_id(1) == 0)
    def _():
        wdb_ref[...] = wd_ref[...].astype(BF16)

    f_ref[...] = bd_ref[...] + jnp.dot(h_ref[...], wdb_ref[...], preferred_element_type=F32)

    @pl.when(pl.program_id(1) == pl.num_programs(1) - 1)
    def _():
        fs_ref[...] = bd_ref[...] + jnp.dot(hs_ref[...], wdb_ref[...], preferred_element_type=F32)


def ffn_down(h, h_s, wd, bd, layer, tm, tn):
    m, dff = h.shape
    ms = h_s.shape[0]
    n = wd.shape[2]
    vmem = 2 * tm * dff * 2 + ms * dff * 2 + 2 * dff * tn * 4 + dff * tn * 2 + 4 * tm * tn * 4 + 4 * ms * tn * 4
    return pl.pallas_call(
        _ffn_down_body,
        grid=(n // tn, m // tm),
        in_specs=[pl.BlockSpec((tm, dff), lambda j, i: (i, 0)),
                  _const_spec((ms, dff), (0, 0)),
                  pl.BlockSpec((None, dff, tn), lambda j, i: (layer, 0, j)),
                  pl.BlockSpec((None, 1, tn), lambda j, i: (layer, 0, j))],
        out_specs=[pl.BlockSpec((tm, tn), lambda j, i: (i, j)),
                   pl.BlockSpec((ms, tn), lambda j, i: (0, j))],
        out_shape=[jax.ShapeDtypeStruct((m, n), F32), jax.ShapeDtypeStruct((ms, n), F32)],
        scratch_shapes=[pltpu.VMEM((dff, tn), BF16)],
        compiler_params=_params(("arbitrary", "arbitrary"), vmem),
        name="ffn_down",
    )(h, h_s, wd, _layer_vec(bd))


def _ple_body(f_ref, x_ref, p_ref, lg_ref, lb_ref, wg_ref, wp_ref, g_ref, o_ref, ob_ref):
    tm = o_ref.shape[0]
    sub = min(ROW_SUB, tm)
    for r0 in range(0, tm, sub):
        rows = pl.ds(r0, sub)
        x = _ln(ALPHA * x_ref[rows, :] + f_ref[rows, :], lg_ref[...], lb_ref[...])
        gate = _sigmoid(jnp.dot(x.astype(BF16), wg_ref[...], preferred_element_type=F32))
        e = jnp.dot(p_ref[rows, :].astype(BF16), wp_ref[...], preferred_element_type=F32)
        e = e * lax.rsqrt(jnp.mean(e * e, axis=-1, keepdims=True) + LN_EPS) * g_ref[...]
        out = x + gate * e
        o_ref[rows, :] = out
        ob_ref[rows, :] = out.astype(BF16)


def ple(f, x, p, lg, lb, wg, wp, g, layer, tm):
    m, k = f.shape
    dp = p.shape[2]
    row = lambda i: (i, 0)
    vmem = 6 * tm * k * 4 + 2 * tm * k * 2 + 2 * tm * dp * 4 + k * k * 2 + dp * k * 2 + 6 * ROW_SUB * k * 4
    return pl.pallas_call(
        _ple_body,
        grid=(m // tm,),
        in_specs=[pl.BlockSpec((tm, k), row), pl.BlockSpec((tm, k), row),
                  pl.BlockSpec((None, tm, dp), lambda i: (layer, i, 0)),
                  _vec_spec(k, layer), _vec_spec(k, layer),
                  _const_spec((None, k, k), (layer, 0, 0)),
                  _const_spec((None, dp, k), (layer, 0, 0)),
                  _vec_spec(k, layer)],
        out_specs=[pl.BlockSpec((tm, k), row), pl.BlockSpec((tm, k), row)],
        out_shape=[jax.ShapeDtypeStruct((m, k), F32), jax.ShapeDtypeStruct((m, k), BF16)],
        compiler_params=_params(("arbitrary",), vmem),
        name="ple",
    )(f, x, p, _layer_vec(lg), _layer_vec(lb), wg, wp, _layer_vec(g))


WT_CHUNK = 512


def _stage_transposed(wt_ref, wb_ref):
    n = wt_ref.shape[0]
    chunk = min(WT_CHUNK, n)
    for c0 in range(0, n, chunk):
        wb_ref[:, pl.ds(c0, chunk)] = wt_ref[pl.ds(c0, chunk), :].T.astype(BF16)


def _mm_qkv_body(q_tiles, x_ref, xs_ref, wt_ref, o_ref, os_ref, wb_ref):
    @pl.when(pl.program_id(1) == 0)
    def _():
        _stage_transposed(wt_ref, wb_ref)

    scale = jnp.where(pl.program_id(0) < q_tiles, D_QK ** -0.5, 1.0).astype(F32)
    z = jnp.dot(x_ref[...], wb_ref[...], preferred_element_type=F32)
    o_ref[...] = (z * scale).astype(BF16)

    @pl.when(pl.program_id(1) == pl.num_programs(1) - 1)
    def _():
        zs = jnp.dot(xs_ref[...], wb_ref[...], preferred_element_type=F32)
        os_ref[...] = (zs * scale).astype(BF16)


def mm_qkv(xb, xb_s, wt, layer, n, tm, tn):
    m, k = xb.shape
    ms = xb_s.shape[0]
    vmem = (2 * tm * k * 2 + ms * k * 2 + 2 * k * tn * 4 + k * tn * 2 + 2 * tm * tn * 2 + 3 * tm * tn * 4
            + WT_CHUNK * k * 8 + 6 * ms * tn * 4)
    return pl.pallas_call(
        functools.partial(_mm_qkv_body, (N_HEADS * D_QK) // tn),
        grid=(n // tn, m // tm),
        in_specs=[pl.BlockSpec((tm, k), lambda j, i: (i, 0)),
                  _const_spec((ms, k), (0, 0)),
                  pl.BlockSpec((None, tn, k), lambda j, i: (layer, j, 0))],
        out_specs=[pl.BlockSpec((tm, tn), lambda j, i: (i, j)),
                   pl.BlockSpec((ms, tn), lambda j, i: (0, j))],
        out_shape=[jax.ShapeDtypeStruct((m, n), BF16), jax.ShapeDtypeStruct((ms, n), BF16)],
        scratch_shapes=[pltpu.VMEM((k, tn), BF16)],
        compiler_params=_params(("arbitrary", "arbitrary"), vmem),
        name="mm_qkv",
    )(xb, xb_s, wt)


def _mm_o_gates_body(x_ref, xs_ref, wot_ref, wgt_ref, o_ref, gt_ref, os_ref, gts_ref, wob_ref, wgb_ref):
    @pl.when(pl.program_id(0) == 0)
    def _():
        _stage_transposed(wot_ref, wob_ref)
        wg = wgt_ref[...].T
        lane = lax.broadcasted_iota(jnp.int32, wg.shape, 1)
        first = lane < N_HEADS
        wgb_ref[:, pl.ds(0, GATE_PAD)] = jnp.where(first, wg, 0.0).astype(BF16)
        wgb_ref[:, pl.ds(GATE_PAD, GATE_PAD)] = jnp.where(
            first, pltpu.roll(wg, GATE_PAD - N_HEADS, axis=1), 0.0).astype(BF16)

    xb = x_ref[...]
    o_ref[...] = jnp.dot(xb, wob_ref[...], preferred_element_type=F32)
    gt_ref[...] = jnp.dot(xb, wgb_ref[...], preferred_element_type=F32)

    @pl.when(pl.program_id(0) == pl.num_programs(0) - 1)
    def _():
        xs = xs_ref[...]
        os_ref[...] = jnp.dot(xs, wob_ref[...], preferred_element_type=F32)
        gts_ref[...] = jnp.dot(xs, wgb_ref[...], preferred_element_type=F32)


def mm_o_gates(xb, xb_s, wt, layer, tm):
    m, k = xb.shape
    ms = xb_s.shape[0]
    n = N_HEADS * D_V
    o_row = 2 * N_HEADS * D_QK + N_HEADS * D_V
    row = lambda i: (i, 0)
    vmem = (2 * tm * k * 2 + ms * k * 2 + k * n * 4 + k * n * 2 + GATE_PAD * k * 6
            + 2 * (tm + ms) * (n + 2 * GATE_PAD) * 4 + 2 * tm * n * 4 + WT_CHUNK * k * 8)
    return pl.pallas_call(
        _mm_o_gates_body,
        grid=(m // tm,),
        in_specs=[pl.BlockSpec((tm, k), row),
                  _const_spec((ms, k), (0, 0)),
                  _const_spec((None, n, k), (layer, o_row // n, 0)),
                  _const_spec((None, GATE_PAD, k), (layer, (o_row + n) // GATE_PAD, 0))],
        out_specs=[pl.BlockSpec((tm, n), row), pl.BlockSpec((tm, 2 * GATE_PAD), row),
                   pl.BlockSpec((ms, n), lambda i: (0, 0)), pl.BlockSpec((ms, 2 * GATE_PAD), lambda i: (0, 0))],
        out_shape=[jax.ShapeDtypeStruct((m, n), F32), jax.ShapeDtypeStruct((m, 2 * GATE_PAD), F32),
                   jax.ShapeDtypeStruct((ms, n), F32), jax.ShapeDtypeStruct((ms, 2 * GATE_PAD), F32)],
        scratch_shapes=[pltpu.VMEM((k, n), BF16), pltpu.VMEM((k, 2 * GATE_PAD), BF16)],
        compiler_params=_params(("arbitrary",), vmem),
        name="mm_o_gates",
    )(xb, xb_s, wt, wt)


def _mlstm_chunk_body(q_ref, k_ref, v_ref, o_ref, gi_ref, gf_ref, gr_ref, bi_ref, bf_ref, bcol_ref,
                      hng_ref, out_ref, c_ref, m_ref, s_ref, tot_ref):
    length = q_ref.shape[0]

    @pl.when(pl.program_id(1) == 0)
    def _():
        c_ref[...] = jnp.zeros(c_ref.shape, F32)
        m_ref[...] = jnp.zeros(m_ref.shape, F32)

    row = lax.broadcasted_iota(jnp.int32, (length, length), 0)
    col = lax.broadcasted_iota(jnp.int32, (length, length), 1)
    causal = row >= col
    triu = (row <= col).astype(F32)

    li_col = gi_ref[...] + bi_ref[...]
    bh_col = _row_scan(_log_sigmoid(gf_ref[...] + bf_ref[...]), jnp.add, 0.0)
    cm_col = _row_scan(li_col - bh_col, jnp.maximum, -jnp.inf)
    grow = gr_ref[...] + bcol_ref[...]
    li_row = grow[0:N_HEADS, :]
    bh_row = jnp.dot(_log_sigmoid(grow[N_HEADS:2 * N_HEADS, :]), triu,
                     precision=lax.Precision.HIGHEST, preferred_element_type=F32)
    a_row = li_row - bh_row
    inter = bh_col + m_ref[0:1, :]
    m_tok = jnp.maximum(inter, bh_col + cm_col)
    w_inter = jnp.exp(inter - m_tok)
    emt = jnp.exp(-m_tok)
    e_col = bh_col - m_tok
    m_new = m_tok[length - 1:length, :]
    g_state = w_inter[length - 1:length, :]
    g_tok = jnp.exp(bh_col[length - 1:length, :] - bh_col + li_col - m_new)
    m_ref[...] = jnp.broadcast_to(m_new, m_ref.shape)
    ones = jnp.ones((length, LANES), BF16)

    for h in range(N_HEADS):
        qk = slice(h * D_QK, (h + 1) * D_QK)
        w_intra = jnp.exp(jnp.where(causal, e_col[:, h:h + 1] + a_row[h:h + 1, :], -jnp.inf))
        s = lax.dot_general(q_ref[:, qk], k_ref[:, qk], (((1,), (1,)), ((), ())),
                            preferred_element_type=F32) * w_intra
        s_ref[h] = s.astype(BF16)

    for h in range(N_HEADS):
        qk = slice(h * D_QK, (h + 1) * D_QK)
        vaug = jnp.concatenate([v_ref[:, h * D_V:(h + 1) * D_V], ones], axis=1)
        c_old = c_ref[h]
        tot_ref[h] = (jnp.dot(s_ref[h], vaug, preferred_element_type=F32)
                      + jnp.dot(q_ref[:, qk], c_old.astype(BF16), preferred_element_type=F32)
                      * w_inter[:, h:h + 1])
        kg = (k_ref[:, qk].astype(F32) * g_tok[:, h:h + 1]).astype(BF16)
        c_ref[h] = g_state[:, h:h + 1] * c_old + lax.dot_general(
            kg, vaug, (((0,), (0,)), ((), ())), preferred_element_type=F32)

    for h in range(N_HEADS):
        tot = tot_ref[h]
        den = jnp.maximum(jnp.abs(tot[:, D_V:D_V + LANES]), emt[:, h:h + 1])
        inv = 1.0 / den
        halves = [tot[:, c0:c0 + LANES] * inv for c0 in range(0, D_V, LANES)]
        moments = jnp.concatenate([sum(halves), sum(x * x for x in halves)], axis=0).astype(BF16)
        stat = jnp.dot(moments, jnp.ones((LANES, LANES), BF16), preferred_element_type=F32) * (1.0 / D_V)
        mu = stat[0:length, :]
        rstd = lax.rsqrt(stat[length:2 * length, :] - mu * mu + LN_EPS)
        for idx, x in enumerate(halves):
            cols = pl.ds(h * D_V + idx * LANES, LANES)
            out_ref[:, cols] = (_sigmoid(o_ref[:, cols]) * ((x - mu) * rstd) * hng_ref[:, cols]).astype(BF16)


FORGET_PAD_BIAS = 30.0


def mlstm_chunk(qkv, o, gates, b_gates, hn_g, bsz, seq_len):
    length = MLSTM_CHUNK
    nch = seq_len // length
    hq = N_HEADS * D_QK
    hv = N_HEADS * D_V
    rows = lambda b, c: (b * nch + c, 0)
    pad = GATE_PAD - N_HEADS
    bi_row = jnp.pad(b_gates[0].reshape(1, N_HEADS), ((0, 0), (0, pad)))
    bf_row = jnp.pad(b_gates[1].reshape(1, N_HEADS), ((0, 0), (0, pad)), constant_values=FORGET_PAD_BIAS)
    bcol = b_gates.reshape(2 * N_HEADS, 1)
    gates_t = jnp.concatenate([gates[:, :N_HEADS], gates[:, GATE_PAD:GATE_PAD + N_HEADS]], axis=1).T
    const2 = lambda b, c: (0, 0)
    out, c_aug, m_new = pl.pallas_call(
        _mlstm_chunk_body,
        grid=(bsz, nch),
        in_specs=[pl.BlockSpec((length, hq), rows),
                  pl.BlockSpec((length, hq), lambda b, c: (b * nch + c, 1)),
                  pl.BlockSpec((length, hv), lambda b, c: (b * nch + c, 1)),
                  pl.BlockSpec((length, hv), rows),
                  pl.BlockSpec((length, GATE_PAD), rows),
                  pl.BlockSpec((length, GATE_PAD), lambda b, c: (b * nch + c, 1)),
                  pl.BlockSpec((2 * N_HEADS, length), lambda b, c: (0, b * nch + c)),
                  pl.BlockSpec((1, GATE_PAD), const2),
                  pl.BlockSpec((1, GATE_PAD), const2),
                  pl.BlockSpec((2 * N_HEADS, 1), const2),
                  pl.BlockSpec((1, hv), const2)],
        out_specs=[pl.BlockSpec((length, hv), rows),
                   pl.BlockSpec((None, N_HEADS, D_QK, D_V + LANES), lambda b, c: (b, 0, 0, 0)),
                   pl.BlockSpec((None, SUBLANES, GATE_PAD), lambda b, c: (b, 0, 0))],
        out_shape=[jax.ShapeDtypeStruct((bsz * seq_len, hv), BF16),
                   jax.ShapeDtypeStruct((bsz, N_HEADS, D_QK, D_V + LANES), F32),
                   jax.ShapeDtypeStruct((bsz, SUBLANES, GATE_PAD), F32)],
        scratch_shapes=[pltpu.VMEM((N_HEADS, length, length), BF16),
                        pltpu.VMEM((N_HEADS, length, D_V + LANES), F32)],
        compiler_params=_params(("arbitrary", "arbitrary"), 24 << 20),
        name="mlstm_chunk",
    )(qkv, qkv, qkv, o, gates, gates, gates_t, bi_row, bf_row, bcol, hn_g.reshape(1, hv))
    return out, c_aug[..., :D_V], c_aug[..., D_V], m_new[:, 0, :N_HEADS]


def _mlstm_step_body(q_ref, k_ref, kt_ref, v_ref, o_ref, gi_ref, gf_ref, c0_ref, n0_ref, m0_ref,
                     bi_ref, bf_ref, hng_ref, out_ref, c_ref, n_ref, m_ref, wv_ref, wi_ref):
    bb = q_ref.shape[0]
    li = gi_ref[...] + bi_ref[...]
    inter = _log_sigmoid(gf_ref[...] + bf_ref[...]) + m0_ref[...]
    m_tok = jnp.maximum(inter, li)
    w_intra = jnp.exp(li - m_tok)
    w_inter = jnp.exp(inter - m_tok)
    q = q_ref[...]
    k = k_ref[...]
    n_old = n0_ref[...]
    v = v_ref[...]
    s = jnp.sum(q * k, axis=-1, keepdims=True) * w_intra
    den = s + jnp.sum(q * n_old, axis=-1, keepdims=True) * w_inter
    den = jnp.maximum(jnp.abs(den), jnp.exp(-m_tok))
    n_ref[...] = w_inter * n_old + w_intra * k
    m_ref[...] = m_tok
    wv_ref[...] = w_intra * v
    wi_ref[...] = jnp.broadcast_to(w_inter, wi_ref.shape)

    head_row = lax.broadcasted_iota(jnp.int32, (N_HEADS, D_V), 0)
    qb16 = q.astype(BF16)
    qc_rows = []
    for b in range(bb):
        qc = jnp.zeros((N_HEADS, D_V), F32)
        for h in range(N_HEADS):
            c_old = c0_ref[b, h]
            all_heads = jnp.dot(qb16[b], c_old.astype(BF16), preferred_element_type=F32)
            qc = jnp.where(head_row == h, all_heads, qc)
            c_ref[b, h] = (wi_ref[b, pl.ds(h, 1), :] * c_old
                           + kt_ref[b, :, h:h + 1] * wv_ref[b, pl.ds(h, 1), :])
        qc_rows.append(qc)
    qc_all = jnp.stack(qc_rows, axis=0)
    hv = (s * v + qc_all * w_inter) * (1.0 / den)
    mu = jnp.mean(hv, axis=-1, keepdims=True)
    hc = hv - mu
    var = jnp.mean(hc * hc, axis=-1, keepdims=True)
    hn = hc * lax.rsqrt(var + LN_EPS) * hng_ref[...][None]
    out_ref[...] = _sigmoid(o_ref[...]) * hn


def mlstm_step(q, k, v, o, gates, c0, n0, m0, b_gates, hn_g, layer, bb):
    bsz = q.shape[0]
    kt = jnp.swapaxes(k, 1, 2)
    gi = gates[:, :N_HEADS, None]
    gf = gates[:, GATE_PAD:GATE_PAD + N_HEADS, None]
    m0 = m0[..., None]
    b3 = lambda i: (i, 0, 0)
    st3 = lambda i: (layer, i, 0, 0)
    bias = lambda v: v.reshape(N_HEADS, 1)
    out, c_new, n_new, m_new = pl.pallas_call(
        _mlstm_step_body,
        grid=(bsz // bb,),
        in_specs=[pl.BlockSpec((bb, N_HEADS, D_QK), b3), pl.BlockSpec((bb, N_HEADS, D_QK), b3),
                  pl.BlockSpec((bb, D_QK, N_HEADS), b3),
                  pl.BlockSpec((bb, N_HEADS, D_V), b3), pl.BlockSpec((bb, N_HEADS, D_V), b3),
                  pl.BlockSpec((bb, N_HEADS, 1), b3), pl.BlockSpec((bb, N_HEADS, 1), b3),
                  pl.BlockSpec((None, bb, N_HEADS, D_QK, D_V), lambda i: (layer, i, 0, 0, 0)),
                  pl.BlockSpec((None, bb, N_HEADS, D_QK), st3),
                  pl.BlockSpec((None, bb, N_HEADS, 1), st3),
                  _const_spec((N_HEADS, 1), (0, 0)), _const_spec((N_HEADS, 1), (0, 0)),
                  _const_spec((N_HEADS, D_V), (0, 0))],
        out_specs=[pl.BlockSpec((bb, N_HEADS, D_V), b3),
                   pl.BlockSpec((None, bb, N_HEADS, D_QK, D_V), lambda i: (0, i, 0, 0, 0)),
                   pl.BlockSpec((None, bb, N_HEADS, D_QK), lambda i: (0, i, 0, 0)),
                   pl.BlockSpec((None, bb, N_HEADS, 1), lambda i: (0, i, 0, 0))],
        out_shape=[jax.ShapeDtypeStruct((bsz, N_HEADS, D_V), F32),
                   jax.ShapeDtypeStruct((1, bsz, N_HEADS, D_QK, D_V), F32),
                   jax.ShapeDtypeStruct((1, bsz, N_HEADS, D_QK), F32),
                   jax.ShapeDtypeStruct((1, bsz, N_HEADS, 1), F32)],
        scratch_shapes=[pltpu.VMEM((bb, N_HEADS, D_V), F32), pltpu.VMEM((bb, N_HEADS, D_V), F32)],
        compiler_params=_params(("arbitrary",), 4 * bb * N_HEADS * D_QK * D_V * 4 + (8 << 20)),
        name="mlstm_step",
    )(q, k, kt, v, o, gi, gf, c0, n0, m0, bias(b_gates[0]), bias(b_gates[1]),
      hn_g.reshape(N_HEADS, D_V))
    return out, c_new, n_new, m_new[..., 0]


def _tiles(rows):
    big = rows >= 1024
    return dict(glu=1024 if big else rows, row=512 if big else rows, up=2048 if big else rows,
                down=512 if big else rows, qkv=1024 if big else rows)


def _trunk(xp, xs, pp, ps, bp, sp, states, w):
    conv_state_t, c_st, n_st, m_st, ffn_state = states
    bs = xs.shape[0]
    tp, ts = _tiles(xp.shape[0]), _tiles(bs)
    hq = N_HEADS * D_QK
    hv = N_HEADS * D_V
    new_ffn_p, new_ffn_s = [], []

    def ffn_block(p32, pb, s32, sb, i):
        hp, gtp, hs, gts = ffn_up(pb, sb, ffn_state[i].reshape(bs, 2 * D_FF), w["ff_w_gate"], w["ff_w_up"],
                                  w["ff_w_dw"], w["ff_b_dw"], i, tp["up"], 512, sp)
        tps = sp // tp["up"]
        new_ffn_p.append(gtp.reshape(bp, tps, FFN_TAIL, D_FF)[:, tps - 1, FFN_TAIL - 2:, :])
        new_ffn_s.append(jnp.stack([ffn_state[i][:, 1, :], gts], axis=1))
        fp, fs = ffn_down(hp, hs, w["ff_w_down"], w["ff_b_down"], i, tp["down"], 512)
        norm = (w["ln_ffn_g"], w["ln_ffn_b"], w["pl_w_gate"], w["pl_w_proj"], w["pl_g"], i)
        return ple(fp, p32, pp, *norm, tp["row"]) + ple(fs, s32, ps, *norm, ts["row"])

    yp, tails, us = mm_glu_conv(xp, xs, w["cv_w_in"], w["cv_b_in"], w["cv_w_dw"], w["cv_b_dw"], 0,
                                tp["glu"], 512, sp)
    tps = sp // tp["glu"]
    hist = CONV_WIDTH - 1
    conv_p = tails.reshape(bp, tps, CONV_HALO, D_MODEL)[:, tps - 1, CONV_HALO - hist:, :][None]
    ys, conv_s_t = conv_sample(conv_state_t, us, w["cv_w_dw"], w["cv_b_dw"], 0, 256)
    conv_s = jnp.swapaxes(conv_s_t, 1, 2)
    mix = (w["cv_w_out"], w["cv_b_out"], w["ln_mix_g"], w["ln_mix_b"], 0, 0)
    pre = (w["cv_ln_g"], w["cv_ln_b"])
    p32, pb = mm_ln(yp, xp, *mix, tp["row"], pre=pre)
    s32, sb = mm_ln(ys, xs, *mix, ts["row"], pre=pre)
    p32, pb, s32, sb = ffn_block(p32, pb, s32, sb, 0)

    qkv_p, qkv_s = mm_qkv(pb, sb, w["ml_w_in_t"], 0, 2 * hq + hv, tp["qkv"], 1024)
    o_p, gates_p, o_s, gates_s = mm_o_gates(pb, sb, w["ml_w_in_t"], 0, tp["row"])
    mix_p, c_p, n_p, m_p = mlstm_chunk(qkv_p, o_p, gates_p, w["ml_b_gates"][0], w["ml_hn_g"][0], bp, sp)
    qf = qkv_s[:, :hq].astype(F32).reshape(bs, N_HEADS, D_QK)
    kf = qkv_s[:, hq:2 * hq].astype(F32).reshape(bs, N_HEADS, D_QK)
    vf = qkv_s[:, 2 * hq:].astype(F32).reshape(bs, N_HEADS, D_V)
    mix_s, c_s, n_s, m_s = mlstm_step(qf, kf, vf, o_s.reshape(bs, N_HEADS, D_V), gates_s, c_st, n_st, m_st,
                                      w["ml_b_gates"][0], w["ml_hn_g"][0], 0, 4)
    mix = (w["ml_w_out"], w["ml_b_out"], w["ln_mix_g"], w["ln_mix_b"], 0, 1)
    p32, pb = mm_ln(mix_p, p32, *mix, tp["row"])
    s32, sb = mm_ln(mix_s.reshape(bs, hv), s32, *mix, ts["row"])
    p32, _, s32, _ = ffn_block(p32, pb, s32, sb, 1)
    return (p32, s32, conv_p, conv_s, c_p[None], n_p[None], m_p[None], c_s, n_s, m_s,
            jnp.stack(new_ffn_p), jnp.stack(new_ffn_s))


def kernel(x_prompt, x_sample, p_prompt, p_sample, state_conv, state_mlstm_c, state_mlstm_n,
           state_mlstm_m, state_ffn_conv,
           cv_w_in, cv_b_in, cv_w_dw, cv_b_dw, cv_ln_g, cv_ln_b, cv_w_out, cv_b_out,
           ml_w_in, ml_b_gates, ml_hn_g, ml_w_out, ml_b_out,
           ln_mix_g, ln_mix_b, ln_ffn_g, ln_ffn_b,
           ff_w_gate, ff_w_up, ff_w_dw, ff_b_dw, ff_w_down, ff_b_down,
           pl_w_proj, pl_g, pl_w_gate):
    bp, sp, _ = x_prompt.shape
    bs, ss, _ = x_sample.shape
    w = dict(
        cv_w_in=cv_w_in, cv_b_in=cv_b_in, cv_w_dw=cv_w_dw, cv_b_dw=cv_b_dw,
        cv_ln_g=cv_ln_g, cv_ln_b=cv_ln_b, cv_w_out=cv_w_out.astype(BF16), cv_b_out=cv_b_out,
        ml_w_in_t=jnp.swapaxes(ml_w_in, 1, 2),
        ml_b_gates=ml_b_gates, ml_hn_g=ml_hn_g, ml_w_out=ml_w_out.astype(BF16), ml_b_out=ml_b_out,
        ln_mix_g=ln_mix_g, ln_mix_b=ln_mix_b, ln_ffn_g=ln_ffn_g, ln_ffn_b=ln_ffn_b,
        ff_w_gate=ff_w_gate, ff_w_up=ff_w_up, ff_w_dw=ff_w_dw,
        ff_b_dw=ff_b_dw, ff_w_down=ff_w_down, ff_b_down=ff_b_down,
        pl_w_proj=pl_w_proj.astype(BF16), pl_g=pl_g, pl_w_gate=pl_w_gate.astype(BF16),
    )
    yp, ys, conv_p, conv_s, c_p, n_p, m_p, c_s, n_s, m_s, ffn_p, ffn_s = _trunk(
        x_prompt.reshape(bp * sp, D_MODEL), x_sample.reshape(bs * ss, D_MODEL),
        p_prompt.reshape(DEPTH, bp * sp, D_PLE), p_sample.reshape(DEPTH, bs * ss, D_PLE), bp, sp,
        (jnp.swapaxes(state_conv, 1, 2), state_mlstm_c, state_mlstm_n, state_mlstm_m, state_ffn_conv), w)
    return (yp.reshape(bp, sp, D_MODEL), ys.reshape(bs, ss, D_MODEL), conv_p, conv_s,
            c_p, n_p, m_p, c_s, n_s, m_s, ffn_p, ffn_s)
```

```python
import functools

import jax
import jax.numpy as jnp
from jax import lax
from jax.experimental import pallas as pl
from jax.experimental.pallas import tpu as pltpu

D_MODEL = 2048
D_FF = 5632
D_PLE = 256
N_HEADS = 8
D_QK = 128
D_V = 256
CONV_WIDTH = 31
FFN_CONV_WIDTH = 3
DEPTH = 2
ALPHA = (2.0 * DEPTH) ** 0.25
LN_EPS = 1e-5
MLSTM_CHUNK = 128
GATE_PAD = 128
SUBLANES = 8

F32 = jnp.float32
BF16 = jnp.bfloat16

VMEM_CAP_BYTES = 58 * 1024 * 1024
VMEM_SLACK_BYTES = 8 * 1024 * 1024


def _params(sem, vmem_bytes):
    return pltpu.CompilerParams(
        dimension_semantics=sem,
        vmem_limit_bytes=min(int(vmem_bytes) + VMEM_SLACK_BYTES, VMEM_CAP_BYTES))


def _ln(x, g, b):
    mu = jnp.mean(x, axis=-1, keepdims=True)
    xc = x - mu
    var = jnp.mean(xc * xc, axis=-1, keepdims=True)
    return xc * lax.rsqrt(var + LN_EPS) * g + b


def _sigmoid(x):
    return 1.0 / (1.0 + jnp.exp(-x))


def _log_sigmoid(x):
    return jnp.minimum(x, 0.0) - jnp.log(1.0 + jnp.exp(-jnp.abs(x)))


def _row_scan(x, op, identity):
    rows, cols = x.shape
    shift = 1
    while shift < rows:
        shifted = jnp.concatenate([jnp.full((shift, cols), identity, x.dtype), x[:rows - shift, :]], axis=0)
        x = op(x, shifted)
        shift *= 2
    return x


def _const_spec(shape, index):
    return pl.BlockSpec(shape, lambda *_: index, pipeline_mode=pl.Buffered(1))


def _layer_vec(v):
    return v.reshape(v.shape[0], 1, v.shape[1])


def _vec_spec(n, layer):
    return _const_spec((None, 1, n), (layer, 0, 0))


CONV_ROWS = 64
CONV_HALO = 32
LANES = 128
GLU_SUB_ROWS = 256


def _conv_chunk(pad_ref, w_ref, b_ref, y_ref, row0, lane0):
    lanes = pl.ds(lane0, LANES)
    win = CONV_ROWS + SUBLANES
    acc = jnp.broadcast_to(b_ref[:, lanes], (CONV_ROWS, LANES))
    for r in range(SUBLANES):
        part = None
        for a in range((CONV_WIDTH - 1 - r) // SUBLANES + 1):
            d = SUBLANES * a + r
            start = row0 + CONV_HALO - SUBLANES * (a + 1)
            term = pad_ref[pl.ds(start, win), lanes] * w_ref[pl.ds(CONV_WIDTH - 1 - d, 1), lanes]
            part = term if part is None else part + term
        acc = acc + part[SUBLANES - r:SUBLANES - r + CONV_ROWS, :]
    y_ref[pl.ds(row0, CONV_ROWS), lanes] = acc


def _glu(xb, wab_ref, wgb_ref, ba_ref, bg_ref):
    a = jnp.dot(xb, wab_ref[...], preferred_element_type=F32) + ba_ref[...]
    g = jnp.dot(xb, wgb_ref[...], preferred_element_type=F32) + bg_ref[...]
    return a * _sigmoid(g)


def _mm_glu_conv_body(tiles_per_seq, x_ref, xs_ref, wa_ref, wg_ref, ba_ref, bg_ref, wdw_ref, bdw_ref,
                      y_ref, tail_ref, us_ref, wab_ref, wgb_ref, pad_ref):
    i = pl.program_id(1)
    tm, tn = y_ref.shape

    @pl.when(i == 0)
    def _():
        wab_ref[...] = wa_ref[...].astype(BF16)
        wgb_ref[...] = wg_ref[...].astype(BF16)

    @pl.when((i % tiles_per_seq) == 0)
    def _():
        pad_ref[pl.ds(0, CONV_HALO), :] = jnp.zeros((CONV_HALO, tn), F32)

    for r0 in range(0, tm, GLU_SUB_ROWS):
        xb = x_ref[pl.ds(r0, GLU_SUB_ROWS), :].astype(BF16)
        pad_ref[pl.ds(CONV_HALO + r0, GLU_SUB_ROWS), :] = _glu(xb, wab_ref, wgb_ref, ba_ref, bg_ref)
        for c0 in range(r0, r0 + GLU_SUB_ROWS, CONV_ROWS):
            for l0 in range(0, tn, LANES):
                _conv_chunk(pad_ref, wdw_ref, bdw_ref, y_ref, c0, l0)
    tail = pad_ref[pl.ds(tm, CONV_HALO), :]
    tail_ref[...] = tail
    pad_ref[pl.ds(0, CONV_HALO), :] = tail

    @pl.when(i == pl.num_programs(1) - 1)
    def _():
        us_ref[...] = _glu(xs_ref[...].astype(BF16), wab_ref, wgb_ref, ba_ref, bg_ref)


def mm_glu_conv(x, x_s, w, b, wdw, bdw, layer, tm, tn, seq_len):
    m, k = x.shape
    ms = x_s.shape[0]
    n = w.shape[2] // 2
    nj = n // tn
    b3 = _layer_vec(b)
    vmem = (2 * tm * k * 4 + ms * k * 4 + 4 * k * tn * 4 + 2 * k * tn * 2 + 2 * tm * tn * 4
            + (tm + CONV_HALO) * tn * 4 + GLU_SUB_ROWS * k * 2 + 6 * GLU_SUB_ROWS * tn * 4 + 8 * ms * tn * 4)
    return pl.pallas_call(
        functools.partial(_mm_glu_conv_body, seq_len // tm),
        grid=(nj, m // tm),
        in_specs=[pl.BlockSpec((tm, k), lambda j, i: (i, 0)),
                  _const_spec((ms, k), (0, 0)),
                  pl.BlockSpec((None, k, tn), lambda j, i: (layer, 0, j)),
                  pl.BlockSpec((None, k, tn), lambda j, i: (layer, 0, j + nj)),
                  pl.BlockSpec((None, 1, tn), lambda j, i: (layer, 0, j)),
                  pl.BlockSpec((None, 1, tn), lambda j, i: (layer, 0, j + nj)),
                  pl.BlockSpec((None, CONV_WIDTH, tn), lambda j, i: (layer, 0, j)),
                  pl.BlockSpec((None, 1, tn), lambda j, i: (layer, 0, j))],
        out_specs=[pl.BlockSpec((tm, tn), lambda j, i: (i, j)),
                   pl.BlockSpec((None, CONV_HALO, tn), lambda j, i: (i, 0, j)),
                   pl.BlockSpec((ms, tn), lambda j, i: (0, j))],
        out_shape=[jax.ShapeDtypeStruct((m, n), F32),
                   jax.ShapeDtypeStruct((m // tm, CONV_HALO, n), F32),
                   jax.ShapeDtypeStruct((ms, n), F32)],
        scratch_shapes=[pltpu.VMEM((k, tn), BF16), pltpu.VMEM((k, tn), BF16),
                        pltpu.VMEM((tm + CONV_HALO, tn), F32)],
        compiler_params=_params(("arbitrary", "arbitrary"), vmem),
        name="mm_glu_conv",
    )(x, x_s, w, w, b3, b3, wdw, _layer_vec(bdw))


def _conv_sample_body(p_ref, u_ref, w_ref, b_ref, y_ref, np_ref):
    hist = p_ref.shape[0]
    u = u_ref[...]
    acc = u * w_ref[pl.ds(hist, 1), :] + b_ref[...]
    for j in range(hist):
        acc = acc + p_ref[j] * w_ref[pl.ds(j, 1), :]
    y_ref[...] = acc
    for j in range(hist - 1):
        np_ref[j] = p_ref[j + 1]
    np_ref[hist - 1] = u


def conv_sample(past_t, u, w, b, layer, tc):
    _, hist, bsz, c = past_t.shape
    return pl.pallas_call(
        _conv_sample_body,
        grid=(c // tc,),
        in_specs=[pl.BlockSpec((None, hist, bsz, tc), lambda i: (layer, 0, 0, i)),
                  pl.BlockSpec((bsz, tc), lambda i: (0, i)),
                  pl.BlockSpec((None, CONV_WIDTH, tc), lambda i: (layer, 0, i)),
                  pl.BlockSpec((None, 1, tc), lambda i: (layer, 0, i))],
        out_specs=[pl.BlockSpec((bsz, tc), lambda i: (0, i)),
                   pl.BlockSpec((None, hist, bsz, tc), lambda i: (0, 0, 0, i))],
        out_shape=[jax.ShapeDtypeStruct((bsz, c), F32),
                   jax.ShapeDtypeStruct((1, hist, bsz, c), F32)],
        compiler_params=_params(("arbitrary",), 5 * hist * bsz * tc * 4),
        name="conv_sample",
    )(past_t, u, w, _layer_vec(b))


ROW_SUB = 256


def _mm_ln_body(pre_norm, *refs):
    if pre_norm:
        (xin_ref, res_ref, xins_ref, ress_ref, w_ref, b_ref, pg_ref, pb_ref, g_ref, be_ref,
         o_ref, ob_ref, os_ref, obs_ref) = refs
    else:
        (xin_ref, res_ref, xins_ref, ress_ref, w_ref, b_ref, g_ref, be_ref,
         o_ref, ob_ref, os_ref, obs_ref) = refs

    def rows_out(xin, res):
        if pre_norm:
            t = _ln(xin.astype(F32), pg_ref[...], pb_ref[...])
            xb = (t * _sigmoid(t)).astype(BF16)
        else:
            xb = xin.astype(BF16)
        mix = jnp.dot(xb, w_ref[...], preferred_element_type=F32) + b_ref[...]
        return _ln(ALPHA * res + mix, g_ref[...], be_ref[...])

    tm = o_ref.shape[0]
    sub = min(ROW_SUB, tm)
    for r0 in range(0, tm, sub):
        rows = pl.ds(r0, sub)
        out = rows_out(xin_ref[rows, :], res_ref[rows, :])
        o_ref[rows, :] = out
        ob_ref[rows, :] = out.astype(BF16)

    @pl.when(pl.program_id(0) == pl.num_programs(0) - 1)
    def _():
        out = rows_out(xins_ref[...], ress_ref[...])
        os_ref[...] = out
        obs_ref[...] = out.astype(BF16)


def mm_ln(xin, resid, xin_s, resid_s, w, b, g, be, mix_layer, ln_layer, tm, pre=None):
    m, k = xin.shape
    ms = xin_s.shape[0]
    n = w.shape[2]
    row = lambda i: (i, 0)
    fixed = lambda i: (0, 0)
    in_specs = [pl.BlockSpec((tm, k), row), pl.BlockSpec((tm, n), row),
                _const_spec((ms, k), (0, 0)), _const_spec((ms, n), (0, 0)),
                _const_spec((None, k, n), (mix_layer, 0, 0)), _vec_spec(n, mix_layer)]
    args = [xin, resid, xin_s, resid_s, w, _layer_vec(b)]
    if pre is not None:
        in_specs += [_vec_spec(k, mix_layer), _vec_spec(k, mix_layer)]
        args += [_layer_vec(pre[0]), _layer_vec(pre[1])]
    in_specs += [_vec_spec(n, ln_layer), _vec_spec(n, ln_layer)]
    args += [_layer_vec(g), _layer_vec(be)]
    vmem = (2 * tm * k * xin.dtype.itemsize + 4 * tm * n * 4 + 2 * tm * n * 2 + k * n * 2 + 6 * ROW_SUB * n * 4
            + 12 * ms * n * 4)
    return pl.pallas_call(
        functools.partial(_mm_ln_body, pre is not None),
        grid=(m // tm,),
        in_specs=in_specs,
        out_specs=[pl.BlockSpec((tm, n), row), pl.BlockSpec((tm, n), row),
                   pl.BlockSpec((ms, n), fixed), pl.BlockSpec((ms, n), fixed)],
        out_shape=[jax.ShapeDtypeStruct((m, n), F32), jax.ShapeDtypeStruct((m, n), BF16),
                   jax.ShapeDtypeStruct((ms, n), F32), jax.ShapeDtypeStruct((ms, n), BF16)],
        compiler_params=_params(("arbitrary",), vmem),
        name="mm_ln",
    )(*args)


FFN_TAIL = 8
FFN_SUB_ROWS = 256


def _ffn_hidden(g, g1, g2, up, wdw_ref, bdw_ref):
    gc = (wdw_ref[pl.ds(0, 1), :] * g2 + wdw_ref[pl.ds(1, 1), :] * g1 + wdw_ref[pl.ds(2, 1), :] * g
          + bdw_ref[...])
    return (gc * _sigmoid(gc) * up).astype(BF16)


def _ffn_up_body(tiles_per_seq, sub_rows, x_ref, xs_ref, p2_ref, p1_ref, wg_ref, wu_ref, wdw_ref,
                 bdw_ref, h_ref, gt_ref, hs_ref, gts_ref, wgb_ref, wub_ref, gs_ref):
    i = pl.program_id(1)
    tm, tn = h_ref.shape

    @pl.when(i == 0)
    def _():
        wgb_ref[...] = wg_ref[...].astype(BF16)
        wub_ref[...] = wu_ref[...].astype(BF16)

    @pl.when((i % tiles_per_seq) == 0)
    def _():
        gs_ref[pl.ds(0, FFN_TAIL), :] = jnp.zeros((FFN_TAIL, tn), F32)

    for r0 in range(0, tm, sub_rows):
        xr = x_ref[pl.ds(r0, sub_rows), :]
        g = jnp.dot(xr, wgb_ref[...], preferred_element_type=F32)
        up = jnp.dot(xr, wub_ref[...], preferred_element_type=F32)
        gs_ref[pl.ds(FFN_TAIL + r0, sub_rows), :] = g
        g1 = gs_ref[pl.ds(FFN_TAIL + r0 - 1, sub_rows), :]
        g2 = gs_ref[pl.ds(FFN_TAIL + r0 - 2, sub_rows), :]
        h_ref[pl.ds(r0, sub_rows), :] = _ffn_hidden(g, g1, g2, up, wdw_ref, bdw_ref)
    tail = gs_ref[pl.ds(tm, FFN_TAIL), :]
    gt_ref[...] = tail
    gs_ref[pl.ds(0, FFN_TAIL), :] = tail

    @pl.when(i == pl.num_programs(1) - 1)
    def _():
        xs = xs_ref[...]
        g = jnp.dot(xs, wgb_ref[...], preferred_element_type=F32)
        up = jnp.dot(xs, wub_ref[...], preferred_element_type=F32)
        gts_ref[...] = g
        hs_ref[...] = _ffn_hidden(g, p1_ref[...], p2_ref[...], up, wdw_ref, bdw_ref)


def ffn_up(xb, xb_s, past_s, wg, wu, wdw, bdw, layer, tm, tn, seq_len):
    m, k = xb.shape
    ms = xb_s.shape[0]
    dff = wg.shape[2]
    nj = dff // tn
    sub = min(FFN_SUB_ROWS, tm)
    vmem = (2 * tm * k * 2 + ms * k * 2 + 4 * k * tn * 4 + 2 * k * tn * 2 + 2 * tm * tn * 2 + tm * tn * 4
            + 8 * sub * tn * 4 + 16 * ms * tn * 4)
    return pl.pallas_call(
        functools.partial(_ffn_up_body, seq_len // tm, sub),
        grid=(nj, m // tm),
        in_specs=[pl.BlockSpec((tm, k), lambda j, i: (i, 0)),
                  _const_spec((ms, k), (0, 0)),
                  pl.BlockSpec((ms, tn), lambda j, i: (0, j)),
                  pl.BlockSpec((ms, tn), lambda j, i: (0, j + nj)),
                  pl.BlockSpec((None, k, tn), lambda j, i: (layer, 0, j)),
                  pl.BlockSpec((None, k, tn), lambda j, i: (layer, 0, j)),
                  pl.BlockSpec((None, FFN_CONV_WIDTH, tn), lambda j, i: (layer, 0, j)),
                  pl.BlockSpec((None, 1, tn), lambda j, i: (layer, 0, j))],
        out_specs=[pl.BlockSpec((tm, tn), lambda j, i: (i, j)),
                   pl.BlockSpec((None, FFN_TAIL, tn), lambda j, i: (i, 0, j)),
                   pl.BlockSpec((ms, tn), lambda j, i: (0, j)),
                   pl.BlockSpec((ms, tn), lambda j, i: (0, j))],
        out_shape=[jax.ShapeDtypeStruct((m, dff), BF16),
                   jax.ShapeDtypeStruct((m // tm, FFN_TAIL, dff), F32),
                   jax.ShapeDtypeStruct((ms, dff), BF16),
                   jax.ShapeDtypeStruct((ms, dff), F32)],
        scratch_shapes=[pltpu.VMEM((k, tn), BF16), pltpu.VMEM((k, tn), BF16),
                        pltpu.VMEM((tm + FFN_TAIL, tn), F32)],
        compiler_params=_params(("arbitrary", "arbitrary"), vmem),
        name="ffn_up",
    )(xb, xb_s, past_s, past_s, wg, wu, wdw, _layer_vec(bdw))


def _ffn_down_body(h_ref, hs_ref, wd_ref, bd_ref, f_ref, fs_ref, wdb_ref):
    @pl.when(pl.program_id(1) == 0)
    def _():
        wdb_ref[...] = wd_ref[...].astype(BF16)

    f_ref[...] = bd_ref[...] + jnp.dot(h_ref[...], wdb_ref[...], preferred_element_type=F32)

    @pl.when(pl.program_id(1) == pl.num_programs(1) - 1)
    def _():
        fs_ref[...] = bd_ref[...] + jnp.dot(hs_ref[...], wdb_ref[...], preferred_element_type=F32)


def ffn_down(h, h_s, wd, bd, layer, tm, tn):
    m, dff = h.shape
    ms = h_s.shape[0]
    n = wd.shape[2]
    vmem = 2 * tm * dff * 2 + ms * dff * 2 + 2 * dff * tn * 4 + dff * tn * 2 + 4 * tm * tn * 4 + 4 * ms * tn * 4
    return pl.pallas_call(
        _ffn_down_body,
        grid=(n // tn, m // tm),
        in_specs=[pl.BlockSpec((tm, dff), lambda j, i: (i, 0)),
                  _const_spec((ms, dff), (0, 0)),
                  pl.BlockSpec((None, dff, tn), lambda j, i: (layer, 0, j)),
                  pl.BlockSpec((None, 1, tn), lambda j, i: (layer, 0, j))],
        out_specs=[pl.BlockSpec((tm, tn), lambda j, i: (i, j)),
                   pl.BlockSpec((ms, tn), lambda j, i: (0, j))],
        out_shape=[jax.ShapeDtypeStruct((m, n), F32), jax.ShapeDtypeStruct((ms, n), F32)],
        scratch_shapes=[pltpu.VMEM((dff, tn), BF16)],
        compiler_params=_params(("arbitrary", "arbitrary"), vmem),
        name="ffn_down",
    )(h, h_s, wd, _layer_vec(bd))


def _ple_body(emit_bf16, f_ref, x_ref, p_ref, fs_ref, xs_ref, ps_ref, lg_ref, lb_ref, wg_ref, wp_ref, g_ref,
              o_ref, os_ref, *bf16_refs):
    def rows_out(f, x, p):
        xn = _ln(ALPHA * x + f, lg_ref[...], lb_ref[...])
        gate = _sigmoid(jnp.dot(xn.astype(BF16), wg_ref[...], preferred_element_type=F32))
        e = jnp.dot(p.astype(BF16), wp_ref[...], preferred_element_type=F32)
        e = e * lax.rsqrt(jnp.mean(e * e, axis=-1, keepdims=True) + LN_EPS) * g_ref[...]
        return xn + gate * e

    tm = o_ref.shape[0]
    sub = min(ROW_SUB, tm)
    for r0 in range(0, tm, sub):
        rows = pl.ds(r0, sub)
        out = rows_out(f_ref[rows, :], x_ref[rows, :], p_ref[rows, :])
        o_ref[rows, :] = out
        if emit_bf16:
            bf16_refs[0][rows, :] = out.astype(BF16)

    @pl.when(pl.program_id(0) == pl.num_programs(0) - 1)
    def _():
        out = rows_out(fs_ref[...], xs_ref[...], ps_ref[...])
        os_ref[...] = out
        if emit_bf16:
            bf16_refs[1][...] = out.astype(BF16)


def ple(f, x, p, f_s, x_s, p_s, lg, lb, wg, wp, g, layer, tm, emit_bf16):
    m, k = f.shape
    ms = f_s.shape[0]
    dp = p.shape[2]
    row = lambda i: (i, 0)
    fixed = lambda i: (0, 0)
    out_specs = [pl.BlockSpec((tm, k), row), pl.BlockSpec((ms, k), fixed)]
    out_shape = [jax.ShapeDtypeStruct((m, k), F32), jax.ShapeDtypeStruct((ms, k), F32)]
    if emit_bf16:
        out_specs += [pl.BlockSpec((tm, k), row), pl.BlockSpec((ms, k), fixed)]
        out_shape += [jax.ShapeDtypeStruct((m, k), BF16), jax.ShapeDtypeStruct((ms, k), BF16)]
    vmem = (6 * tm * k * 4 + 2 * tm * k * 2 + 2 * tm * dp * 4 + k * k * 2 + dp * k * 2 + 6 * ROW_SUB * k * 4
            + 12 * ms * k * 4)
    return pl.pallas_call(
        functools.partial(_ple_body, emit_bf16),
        grid=(m // tm,),
        in_specs=[pl.BlockSpec((tm, k), row), pl.BlockSpec((tm, k), row),
                  pl.BlockSpec((None, tm, dp), lambda i: (layer, i, 0)),
                  _const_spec((ms, k), (0, 0)), _const_spec((ms, k), (0, 0)),
                  _const_spec((None, ms, dp), (layer, 0, 0)),
                  _vec_spec(k, layer), _vec_spec(k, layer),
                  _const_spec((None, k, k), (layer, 0, 0)),
                  _const_spec((None, dp, k), (layer, 0, 0)),
                  _vec_spec(k, layer)],
        out_specs=out_specs,
        out_shape=out_shape,
        compiler_params=_params(("arbitrary",), vmem),
        name="ple",
    )(f, x, p, f_s, x_s, p_s, _layer_vec(lg), _layer_vec(lb), wg, wp, _layer_vec(g))


WT_CHUNK = 512


def _stage_transposed(wt_ref, wb_ref):
    n = wt_ref.shape[0]
    chunk = min(WT_CHUNK, n)
    for c0 in range(0, n, chunk):
        wb_ref[:, pl.ds(c0, chunk)] = wt_ref[pl.ds(c0, chunk), :].T.astype(BF16)


def _mm_qkv_body(q_tiles, x_ref, xs_ref, wt_ref, o_ref, os_ref, wb_ref):
    @pl.when(pl.program_id(1) == 0)
    def _():
        _stage_transposed(wt_ref, wb_ref)

    scale = jnp.where(pl.program_id(0) < q_tiles, D_QK ** -0.5, 1.0).astype(F32)
    z = jnp.dot(x_ref[...], wb_ref[...], preferred_element_type=F32)
    o_ref[...] = (z * scale).astype(BF16)

    @pl.when(pl.program_id(1) == pl.num_programs(1) - 1)
    def _():
        zs = jnp.dot(xs_ref[...], wb_ref[...], preferred_element_type=F32)
        os_ref[...] = (zs * scale).astype(BF16)


def mm_qkv(xb, xb_s, wt, layer, n, tm, tn):
    m, k = xb.shape
    ms = xb_s.shape[0]
    vmem = (2 * tm * k * 2 + ms * k * 2 + 2 * k * tn * 4 + k * tn * 2 + 2 * tm * tn * 2 + 3 * tm * tn * 4
            + WT_CHUNK * k * 8 + 6 * ms * tn * 4)
    return pl.pallas_call(
        functools.partial(_mm_qkv_body, (N_HEADS * D_QK) // tn),
        grid=(n // tn, m // tm),
        in_specs=[pl.BlockSpec((tm, k), lambda j, i: (i, 0)),
                  _const_spec((ms, k), (0, 0)),
                  pl.BlockSpec((None, tn, k), lambda j, i: (layer, j, 0))],
        out_specs=[pl.BlockSpec((tm, tn), lambda j, i: (i, j)),
                   pl.BlockSpec((ms, tn), lambda j, i: (0, j))],
        out_shape=[jax.ShapeDtypeStruct((m, n), BF16), jax.ShapeDtypeStruct((ms, n), BF16)],
        scratch_shapes=[pltpu.VMEM((k, tn), BF16)],
        compiler_params=_params(("arbitrary", "arbitrary"), vmem),
        name="mm_qkv",
    )(xb, xb_s, wt)


def _mm_o_gates_body(x_ref, xs_ref, wot_ref, wgt_ref, o_ref, gt_ref, os_ref, gts_ref, wob_ref, wgb_ref):
    @pl.when(pl.program_id(0) == 0)
    def _():
        _stage_transposed(wot_ref, wob_ref)
        wg = wgt_ref[...].T
        lane = lax.broadcasted_iota(jnp.int32, wg.shape, 1)
        first = lane < N_HEADS
        wgb_ref[:, pl.ds(0, GATE_PAD)] = jnp.where(first, wg, 0.0).astype(BF16)
        wgb_ref[:, pl.ds(GATE_PAD, GATE_PAD)] = jnp.where(
            first, pltpu.roll(wg, GATE_PAD - N_HEADS, axis=1), 0.0).astype(BF16)

    xb = x_ref[...]
    o_ref[...] = jnp.dot(xb, wob_ref[...], preferred_element_type=F32)
    gt_ref[...] = jnp.dot(xb, wgb_ref[...], preferred_element_type=F32)

    @pl.when(pl.program_id(0) == pl.num_programs(0) - 1)
    def _():
        xs = xs_ref[...]
        os_ref[...] = jnp.dot(xs, wob_ref[...], preferred_element_type=F32)
        gts_ref[...] = jnp.dot(xs, wgb_ref[...], preferred_element_type=F32)


def mm_o_gates(xb, xb_s, wt, layer, tm):
    m, k = xb.shape
    ms = xb_s.shape[0]
    n = N_HEADS * D_V
    o_row = 2 * N_HEADS * D_QK + N_HEADS * D_V
    row = lambda i: (i, 0)
    vmem = (2 * tm * k * 2 + ms * k * 2 + k * n * 4 + k * n * 2 + GATE_PAD * k * 6
            + 2 * (tm + ms) * (n + 2 * GATE_PAD) * 4 + 2 * tm * n * 4 + WT_CHUNK * k * 8)
    return pl.pallas_call(
        _mm_o_gates_body,
        grid=(m // tm,),
        in_specs=[pl.BlockSpec((tm, k), row),
                  _const_spec((ms, k), (0, 0)),
                  _const_spec((None, n, k), (layer, o_row // n, 0)),
                  _const_spec((None, GATE_PAD, k), (layer, (o_row + n) // GATE_PAD, 0))],
        out_specs=[pl.BlockSpec((tm, n), row), pl.BlockSpec((tm, 2 * GATE_PAD), row),
                   pl.BlockSpec((ms, n), lambda i: (0, 0)), pl.BlockSpec((ms, 2 * GATE_PAD), lambda i: (0, 0))],
        out_shape=[jax.ShapeDtypeStruct((m, n), F32), jax.ShapeDtypeStruct((m, 2 * GATE_PAD), F32),
                   jax.ShapeDtypeStruct((ms, n), F32), jax.ShapeDtypeStruct((ms, 2 * GATE_PAD), F32)],
        scratch_shapes=[pltpu.VMEM((k, n), BF16), pltpu.VMEM((k, 2 * GATE_PAD), BF16)],
        compiler_params=_params(("arbitrary",), vmem),
        name="mm_o_gates",
    )(xb, xb_s, wt, wt)


def _mlstm_chunk_body(q_ref, k_ref, v_ref, o_ref, gi_ref, gf_ref, gr_ref, bi_ref, bf_ref, bcol_ref,
                      hng_ref, out_ref, c_ref, m_ref, s_ref, tot_ref):
    length = q_ref.shape[0]

    @pl.when(pl.program_id(1) == 0)
    def _():
        c_ref[...] = jnp.zeros(c_ref.shape, F32)
        m_ref[...] = jnp.zeros(m_ref.shape, F32)

    row = lax.broadcasted_iota(jnp.int32, (length, length), 0)
    col = lax.broadcasted_iota(jnp.int32, (length, length), 1)
    causal = row >= col
    triu = (row <= col).astype(F32)

    li_col = gi_ref[...] + bi_ref[...]
    bh_col = _row_scan(_log_sigmoid(gf_ref[...] + bf_ref[...]), jnp.add, 0.0)
    cm_col = _row_scan(li_col - bh_col, jnp.maximum, -jnp.inf)
    grow = gr_ref[...] + bcol_ref[...]
    li_row = grow[0:N_HEADS, :]
    bh_row = jnp.dot(_log_sigmoid(grow[N_HEADS:2 * N_HEADS, :]), triu,
                     precision=lax.Precision.HIGHEST, preferred_element_type=F32)
    a_row = li_row - bh_row
    inter = bh_col + m_ref[0:1, :]
    m_tok = jnp.maximum(inter, bh_col + cm_col)
    w_inter = jnp.exp(inter - m_tok)
    emt = jnp.exp(-m_tok)
    e_col = bh_col - m_tok
    m_new = m_tok[length - 1:length, :]
    g_state = w_inter[length - 1:length, :]
    g_tok = jnp.exp(bh_col[length - 1:length, :] - bh_col + li_col - m_new)
    m_ref[...] = jnp.broadcast_to(m_new, m_ref.shape)
    ones = jnp.ones((length, LANES), BF16)

    for h in range(N_HEADS):
        qk = slice(h * D_QK, (h + 1) * D_QK)
        w_intra = jnp.exp(jnp.where(causal, e_col[:, h:h + 1] + a_row[h:h + 1, :], -jnp.inf))
        s = lax.dot_general(q_ref[:, qk], k_ref[:, qk], (((1,), (1,)), ((), ())),
                            preferred_element_type=F32) * w_intra
        s_ref[h] = s.astype(BF16)

    for h in range(N_HEADS):
        qk = slice(h * D_QK, (h + 1) * D_QK)
        vaug = jnp.concatenate([v_ref[:, h * D_V:(h + 1) * D_V], ones], axis=1)
        c_old = c_ref[h]
        tot_ref[h] = (jnp.dot(s_ref[h], vaug, preferred_element_type=F32)
                      + jnp.dot(q_ref[:, qk], c_old.astype(BF16), preferred_element_type=F32)
                      * w_inter[:, h:h + 1])
        kg = (k_ref[:, qk].astype(F32) * g_tok[:, h:h + 1]).astype(BF16)
        c_ref[h] = g_state[:, h:h + 1] * c_old + lax.dot_general(
            kg, vaug, (((0,), (0,)), ((), ())), preferred_element_type=F32)

    for h in range(N_HEADS):
        tot = tot_ref[h]
        den = jnp.maximum(jnp.abs(tot[:, D_V:D_V + LANES]), emt[:, h:h + 1])
        inv = 1.0 / den
        halves = [tot[:, c0:c0 + LANES] * inv for c0 in range(0, D_V, LANES)]
        moments = jnp.concatenate([sum(halves), sum(x * x for x in halves)], axis=0).astype(BF16)
        stat = jnp.dot(moments, jnp.ones((LANES, LANES), BF16), preferred_element_type=F32) * (1.0 / D_V)
        mu = stat[0:length, :]
        rstd = lax.rsqrt(stat[length:2 * length, :] - mu * mu + LN_EPS)
        for idx, x in enumerate(halves):
            cols = pl.ds(h * D_V + idx * LANES, LANES)
            out_ref[:, cols] = (_sigmoid(o_ref[:, cols]) * ((x - mu) * rstd) * hng_ref[:, cols]).astype(BF16)


FORGET_PAD_BIAS = 30.0


def mlstm_chunk(qkv, o, gates, b_gates, hn_g, bsz, seq_len):
    length = MLSTM_CHUNK
    nch = seq_len // length
    hq = N_HEADS * D_QK
    hv = N_HEADS * D_V
    rows = lambda b, c: (b * nch + c, 0)
    pad = GATE_PAD - N_HEADS
    bi_row = jnp.pad(b_gates[0].reshape(1, N_HEADS), ((0, 0), (0, pad)))
    bf_row = jnp.pad(b_gates[1].reshape(1, N_HEADS), ((0, 0), (0, pad)), constant_values=FORGET_PAD_BIAS)
    bcol = b_gates.reshape(2 * N_HEADS, 1)
    gates_t = jnp.concatenate([gates[:, :N_HEADS], gates[:, GATE_PAD:GATE_PAD + N_HEADS]], axis=1).T
    const2 = lambda b, c: (0, 0)
    out, c_aug, m_new = pl.pallas_call(
        _mlstm_chunk_body,
        grid=(bsz, nch),
        in_specs=[pl.BlockSpec((length, hq), rows),
                  pl.BlockSpec((length, hq), lambda b, c: (b * nch + c, 1)),
                  pl.BlockSpec((length, hv), lambda b, c: (b * nch + c, 1)),
                  pl.BlockSpec((length, hv), rows),
                  pl.BlockSpec((length, GATE_PAD), rows),
                  pl.BlockSpec((length, GATE_PAD), lambda b, c: (b * nch + c, 1)),
                  pl.BlockSpec((2 * N_HEADS, length), lambda b, c: (0, b * nch + c)),
                  pl.BlockSpec((1, GATE_PAD), const2),
                  pl.BlockSpec((1, GATE_PAD), const2),
                  pl.BlockSpec((2 * N_HEADS, 1), const2),
                  pl.BlockSpec((1, hv), const2)],
        out_specs=[pl.BlockSpec((length, hv), rows),
                   pl.BlockSpec((None, N_HEADS, D_QK, D_V + LANES), lambda b, c: (b, 0, 0, 0)),
                   pl.BlockSpec((None, SUBLANES, GATE_PAD), lambda b, c: (b, 0, 0))],
        out_shape=[jax.ShapeDtypeStruct((bsz * seq_len, hv), BF16),
                   jax.ShapeDtypeStruct((bsz, N_HEADS, D_QK, D_V + LANES), F32),
                   jax.ShapeDtypeStruct((bsz, SUBLANES, GATE_PAD), F32)],
        scratch_shapes=[pltpu.VMEM((N_HEADS, length, length), BF16),
                        pltpu.VMEM((N_HEADS, length, D_V + LANES), F32)],
        compiler_params=_params(("arbitrary", "arbitrary"), 24 << 20),
        name="mlstm_chunk",
    )(qkv, qkv, qkv, o, gates, gates, gates_t, bi_row, bf_row, bcol, hn_g.reshape(1, hv))
    return out, c_aug[..., :D_V], c_aug[..., D_V], m_new[:, 0, :N_HEADS]


def _mlstm_step_body(q_ref, k_ref, kt_ref, v_ref, o_ref, gi_ref, gf_ref, c0_ref, n0_ref, m0_ref,
                     bi_ref, bf_ref, hng_ref, out_ref, c_ref, n_ref, m_ref, wv_ref, wi_ref):
    bb = q_ref.shape[0]
    li = gi_ref[...] + bi_ref[...]
    inter = _log_sigmoid(gf_ref[...] + bf_ref[...]) + m0_ref[...]
    m_tok = jnp.maximum(inter, li)
    w_intra = jnp.exp(li - m_tok)
    w_inter = jnp.exp(inter - m_tok)
    q = q_ref[...]
    k = k_ref[...]
    n_old = n0_ref[...]
    v = v_ref[...]
    s = jnp.sum(q * k, axis=-1, keepdims=True) * w_intra
    den = s + jnp.sum(q * n_old, axis=-1, keepdims=True) * w_inter
    den = jnp.maximum(jnp.abs(den), jnp.exp(-m_tok))
    n_ref[...] = w_inter * n_old + w_intra * k
    m_ref[...] = m_tok
    wv_ref[...] = w_intra * v
    wi_ref[...] = jnp.broadcast_to(w_inter, wi_ref.shape)

    head_row = lax.broadcasted_iota(jnp.int32, (N_HEADS, D_V), 0)
    qb16 = q.astype(BF16)
    qc_rows = []
    for b in range(bb):
        qc = jnp.zeros((N_HEADS, D_V), F32)
        for h in range(N_HEADS):
            c_old = c0_ref[b, h]
            all_heads = jnp.dot(qb16[b], c_old.astype(BF16), preferred_element_type=F32)
            qc = jnp.where(head_row == h, all_heads, qc)
            c_ref[b, h] = (wi_ref[b, pl.ds(h, 1), :] * c_old
                           + kt_ref[b, :, h:h + 1] * wv_ref[b, pl.ds(h, 1), :])
        qc_rows.append(qc)
    qc_all = jnp.stack(qc_rows, axis=0)
    hv = (s * v + qc_all * w_inter) * (1.0 / den)
    mu = jnp.mean(hv, axis=-1, keepdims=True)
    hc = hv - mu
    var = jnp.mean(hc * hc, axis=-1, keepdims=True)
    hn = hc * lax.rsqrt(var + LN_EPS) * hng_ref[...][None]
    out_ref[...] = _sigmoid(o_ref[...]) * hn


def mlstm_step(q, k, v, o, gates, c0, n0, m0, b_gates, hn_g, layer, bb):
    bsz = q.shape[0]
    kt = jnp.swapaxes(k, 1, 2)
    gi = gates[:, :N_HEADS, None]
    gf = gates[:, GATE_PAD:GATE_PAD + N_HEADS, None]
    m0 = m0[..., None]
    b3 = lambda i: (i, 0, 0)
    st3 = lambda i: (layer, i, 0, 0)
    bias = lambda v: v.reshape(N_HEADS, 1)
    out, c_new, n_new, m_new = pl.pallas_call(
        _mlstm_step_body,
        grid=(bsz // bb,),
        in_specs=[pl.BlockSpec((bb, N_HEADS, D_QK), b3), pl.BlockSpec((bb, N_HEADS, D_QK), b3),
                  pl.BlockSpec((bb, D_QK, N_HEADS), b3),
                  pl.BlockSpec((bb, N_HEADS, D_V), b3), pl.BlockSpec((bb, N_HEADS, D_V), b3),
                  pl.BlockSpec((bb, N_HEADS, 1), b3), pl.BlockSpec((bb, N_HEADS, 1), b3),
                  pl.BlockSpec((None, bb, N_HEADS, D_QK, D_V), lambda i: (layer, i, 0, 0, 0)),
                  pl.BlockSpec((None, bb, N_HEADS, D_QK), st3),
                  pl.BlockSpec((None, bb, N_HEADS, 1), st3),
                  _const_spec((N_HEADS, 1), (0, 0)), _const_spec((N_HEADS, 1), (0, 0)),
                  _const_spec((N_HEADS, D_V), (0, 0))],
        out_specs=[pl.BlockSpec((bb, N_HEADS, D_V), b3),
                   pl.BlockSpec((None, bb, N_HEADS, D_QK, D_V), lambda i: (0, i, 0, 0, 0)),
                   pl.BlockSpec((None, bb, N_HEADS, D_QK), lambda i: (0, i, 0, 0)),
                   pl.BlockSpec((None, bb, N_HEADS, 1), lambda i: (0, i, 0, 0))],
        out_shape=[jax.ShapeDtypeStruct((bsz, N_HEADS, D_V), F32),
                   jax.ShapeDtypeStruct((1, bsz, N_HEADS, D_QK, D_V), F32),
                   jax.ShapeDtypeStruct((1, bsz, N_HEADS, D_QK), F32),
                   jax.ShapeDtypeStruct((1, bsz, N_HEADS, 1), F32)],
        scratch_shapes=[pltpu.VMEM((bb, N_HEADS, D_V), F32), pltpu.VMEM((bb, N_HEADS, D_V), F32)],
        compiler_params=_params(("arbitrary",), 4 * bb * N_HEADS * D_QK * D_V * 4 + (8 << 20)),
        name="mlstm_step",
    )(q, k, kt, v, o, gi, gf, c0, n0, m0, bias(b_gates[0]), bias(b_gates[1]),
      hn_g.reshape(N_HEADS, D_V))
    return out, c_new, n_new, m_new[..., 0]


TM_GLU = 1024
TM_ROW = 512
TM_FFN_UP = 2048
TM_FFN_DOWN = 512
TM_QKV = 2048
TN_STRIPE = 512
TN_QKV = 1024
TC_CONV_SAMPLE = 256
BB_MLSTM_STEP = 8


def _trunk(xp, xs, pp, ps, bp, sp, states, w):
    conv_state_t, c_st, n_st, m_st, ffn_state = states
    bs = xs.shape[0]
    hq = N_HEADS * D_QK
    hv = N_HEADS * D_V
    new_ffn_p, new_ffn_s = [], []

    def ffn_block(p32, pb, s32, sb, i, emit_bf16):
        hp, gtp, hs, gts = ffn_up(pb, sb, ffn_state[i].reshape(bs, 2 * D_FF), w["ff_w_gate"], w["ff_w_up"],
                                  w["ff_w_dw"], w["ff_b_dw"], i, TM_FFN_UP, TN_STRIPE, sp)
        tps = sp // TM_FFN_UP
        new_ffn_p.append(gtp.reshape(bp, tps, FFN_TAIL, D_FF)[:, tps - 1, FFN_TAIL - 2:, :])
        new_ffn_s.append(jnp.stack([ffn_state[i][:, 1, :], gts], axis=1))
        fp, fs = ffn_down(hp, hs, w["ff_w_down"], w["ff_b_down"], i, TM_FFN_DOWN, TN_STRIPE)
        return ple(fp, p32, pp, fs, s32, ps, w["ln_ffn_g"], w["ln_ffn_b"], w["pl_w_gate"], w["pl_w_proj"],
                   w["pl_g"], i, TM_ROW, emit_bf16)

    yp, tails, us = mm_glu_conv(xp, xs, w["cv_w_in"], w["cv_b_in"], w["cv_w_dw"], w["cv_b_dw"], 0,
                                TM_GLU, TN_STRIPE, sp)
    tps = sp // TM_GLU
    hist = CONV_WIDTH - 1
    conv_p = tails.reshape(bp, tps, CONV_HALO, D_MODEL)[:, tps - 1, CONV_HALO - hist:, :][None]
    ys, conv_s_t = conv_sample(conv_state_t, us, w["cv_w_dw"], w["cv_b_dw"], 0, TC_CONV_SAMPLE)
    conv_s = jnp.swapaxes(conv_s_t, 1, 2)
    p32, pb, s32, sb = mm_ln(yp, xp, ys, xs, w["cv_w_out"], w["cv_b_out"], w["ln_mix_g"], w["ln_mix_b"], 0, 0,
                             TM_ROW, pre=(w["cv_ln_g"], w["cv_ln_b"]))
    p32, s32, pb, sb = ffn_block(p32, pb, s32, sb, 0, True)

    qkv_p, qkv_s = mm_qkv(pb, sb, w["ml_w_in_t"], 0, 2 * hq + hv, TM_QKV, TN_QKV)
    o_p, gates_p, o_s, gates_s = mm_o_gates(pb, sb, w["ml_w_in_t"], 0, TM_ROW)
    mix_p, c_p, n_p, m_p = mlstm_chunk(qkv_p, o_p, gates_p, w["ml_b_gates"][0], w["ml_hn_g"][0], bp, sp)
    qf = qkv_s[:, :hq].astype(F32).reshape(bs, N_HEADS, D_QK)
    kf = qkv_s[:, hq:2 * hq].astype(F32).reshape(bs, N_HEADS, D_QK)
    vf = qkv_s[:, 2 * hq:].astype(F32).reshape(bs, N_HEADS, D_V)
    mix_s, c_s, n_s, m_s = mlstm_step(qf, kf, vf, o_s.reshape(bs, N_HEADS, D_V), gates_s, c_st, n_st, m_st,
                                      w["ml_b_gates"][0], w["ml_hn_g"][0], 0, BB_MLSTM_STEP)
    p32, pb, s32, sb = mm_ln(mix_p, p32, mix_s.reshape(bs, hv), s32, w["ml_w_out"], w["ml_b_out"],
                             w["ln_mix_g"], w["ln_mix_b"], 0, 1, TM_ROW)
    p32, s32 = ffn_block(p32, pb, s32, sb, 1, False)
    return (p32, s32, conv_p, conv_s, c_p[None], n_p[None], m_p[None], c_s, n_s, m_s,
            jnp.stack(new_ffn_p), jnp.stack(new_ffn_s))


def kernel(x_prompt, x_sample, p_prompt, p_sample, state_conv, state_mlstm_c, state_mlstm_n,
           state_mlstm_m, state_ffn_conv,
           cv_w_in, cv_b_in, cv_w_dw, cv_b_dw, cv_ln_g, cv_ln_b, cv_w_out, cv_b_out,
           ml_w_in, ml_b_gates, ml_hn_g, ml_w_out, ml_b_out,
           ln_mix_g, ln_mix_b, ln_ffn_g, ln_ffn_b,
           ff_w_gate, ff_w_up, ff_w_dw, ff_b_dw, ff_w_down, ff_b_down,
           pl_w_proj, pl_g, pl_w_gate):
    bp, sp, _ = x_prompt.shape
    bs, ss, _ = x_sample.shape
    w = dict(
        cv_w_in=cv_w_in, cv_b_in=cv_b_in, cv_w_dw=cv_w_dw, cv_b_dw=cv_b_dw,
        cv_ln_g=cv_ln_g, cv_ln_b=cv_ln_b, cv_w_out=cv_w_out.astype(BF16), cv_b_out=cv_b_out,
        ml_w_in_t=jnp.swapaxes(ml_w_in, 1, 2),
        ml_b_gates=ml_b_gates, ml_hn_g=ml_hn_g, ml_w_out=ml_w_out.astype(BF16), ml_b_out=ml_b_out,
        ln_mix_g=ln_mix_g, ln_mix_b=ln_mix_b, ln_ffn_g=ln_ffn_g, ln_ffn_b=ln_ffn_b,
        ff_w_gate=ff_w_gate, ff_w_up=ff_w_up, ff_w_dw=ff_w_dw,
        ff_b_dw=ff_b_dw, ff_w_down=ff_w_down, ff_b_down=ff_b_down,
        pl_w_proj=pl_w_proj.astype(BF16), pl_g=pl_g, pl_w_gate=pl_w_gate.astype(BF16),
    )
    yp, ys, conv_p, conv_s, c_p, n_p, m_p, c_s, n_s, m_s, ffn_p, ffn_s = _trunk(
        x_prompt.reshape(bp * sp, D_MODEL), x_sample.reshape(bs * ss, D_MODEL),
        p_prompt.reshape(DEPTH, bp * sp, D_PLE), p_sample.reshape(DEPTH, bs * ss, D_PLE), bp, sp,
        (jnp.swapaxes(state_conv, 1, 2), state_mlstm_c, state_mlstm_n, state_mlstm_m, state_ffn_conv), w)
    return (yp.reshape(bp, sp, D_MODEL), ys.reshape(bs, ss, D_MODEL), conv_p, conv_s,
            c_p, n_p, m_p, c_s, n_s, m_s, ffn_p, ffn_s)
```

```python
import functools

import jax
import jax.numpy as jnp
from jax import lax
from jax.experimental import pallas as pl
from jax.experimental.pallas import tpu as pltpu

D_MODEL = 2048
D_FF = 5632
D_PLE = 256
N_HEADS = 8
D_QK = 128
D_V = 256
CONV_WIDTH = 31
FFN_CONV_WIDTH = 3
DEPTH = 2
ALPHA = (2.0 * DEPTH) ** 0.25
LN_EPS = 1e-5
MLSTM_CHUNK = 128
GATE_PAD = 128
SUBLANES = 8

F32 = jnp.float32
BF16 = jnp.bfloat16

VMEM_CAP_BYTES = 58 * 1024 * 1024
VMEM_SLACK_BYTES = 8 * 1024 * 1024


def _params(sem, vmem_bytes):
    return pltpu.CompilerParams(
        dimension_semantics=sem,
        vmem_limit_bytes=min(int(vmem_bytes) + VMEM_SLACK_BYTES, VMEM_CAP_BYTES))


def _ln(x, g, b):
    mu = jnp.mean(x, axis=-1, keepdims=True)
    xc = x - mu
    var = jnp.mean(xc * xc, axis=-1, keepdims=True)
    return xc * lax.rsqrt(var + LN_EPS) * g + b


def _sigmoid(x):
    return 1.0 / (1.0 + jnp.exp(-x))


def _log_sigmoid(x):
    return jnp.minimum(x, 0.0) - jnp.log(1.0 + jnp.exp(-jnp.abs(x)))


def _row_scan(x, op, identity):
    rows, cols = x.shape
    shift = 1
    while shift < rows:
        shifted = jnp.concatenate([jnp.full((shift, cols), identity, x.dtype), x[:rows - shift, :]], axis=0)
        x = op(x, shifted)
        shift *= 2
    return x


def _const_spec(shape, index):
    return pl.BlockSpec(shape, lambda *_: index, pipeline_mode=pl.Buffered(1))


def _layer_vec(v):
    return v.reshape(v.shape[0], 1, v.shape[1])


def _vec_spec(n, layer):
    return _const_spec((None, 1, n), (layer, 0, 0))


CONV_ROWS = 64
CONV_HALO = 32
LANES = 128
GLU_SUB_ROWS = 256


def _conv_chunk(pad_ref, w_ref, b_ref, y_ref, row0, lane0):
    lanes = pl.ds(lane0, LANES)
    win = CONV_ROWS + SUBLANES
    acc = jnp.broadcast_to(b_ref[:, lanes], (CONV_ROWS, LANES))
    for r in range(SUBLANES):
        part = None
        for a in range((CONV_WIDTH - 1 - r) // SUBLANES + 1):
            d = SUBLANES * a + r
            start = row0 + CONV_HALO - SUBLANES * (a + 1)
            term = pad_ref[pl.ds(start, win), lanes] * w_ref[pl.ds(CONV_WIDTH - 1 - d, 1), lanes]
            part = term if part is None else part + term
        acc = acc + part[SUBLANES - r:SUBLANES - r + CONV_ROWS, :]
    y_ref[pl.ds(row0, CONV_ROWS), lanes] = acc


def _glu(xb, wab_ref, wgb_ref, ba_ref, bg_ref):
    a = jnp.dot(xb, wab_ref[...], preferred_element_type=F32) + ba_ref[...]
    g = jnp.dot(xb, wgb_ref[...], preferred_element_type=F32) + bg_ref[...]
    return a * _sigmoid(g)


def _mm_glu_conv_body(tiles_per_seq, x_ref, xs_ref, wa_ref, wg_ref, ba_ref, bg_ref, wdw_ref, bdw_ref,
                      y_ref, tail_ref, us_ref, wab_ref, wgb_ref, pad_ref):
    i = pl.program_id(1)
    tm, tn = y_ref.shape

    @pl.when(i == 0)
    def _():
        wab_ref[...] = wa_ref[...].astype(BF16)
        wgb_ref[...] = wg_ref[...].astype(BF16)

    @pl.when((i % tiles_per_seq) == 0)
    def _():
        pad_ref[pl.ds(0, CONV_HALO), :] = jnp.zeros((CONV_HALO, tn), F32)

    for r0 in range(0, tm, GLU_SUB_ROWS):
        xb = x_ref[pl.ds(r0, GLU_SUB_ROWS), :].astype(BF16)
        pad_ref[pl.ds(CONV_HALO + r0, GLU_SUB_ROWS), :] = _glu(xb, wab_ref, wgb_ref, ba_ref, bg_ref)
        for c0 in range(r0, r0 + GLU_SUB_ROWS, CONV_ROWS):
            for l0 in range(0, tn, LANES):
                _conv_chunk(pad_ref, wdw_ref, bdw_ref, y_ref, c0, l0)
    tail = pad_ref[pl.ds(tm, CONV_HALO), :]
    tail_ref[...] = tail
    pad_ref[pl.ds(0, CONV_HALO), :] = tail

    @pl.when(i == pl.num_programs(1) - 1)
    def _():
        us_ref[...] = _glu(xs_ref[...].astype(BF16), wab_ref, wgb_ref, ba_ref, bg_ref)


def mm_glu_conv(x, x_s, w, b, wdw, bdw, layer, tm, tn, seq_len):
    m, k = x.shape
    ms = x_s.shape[0]
    n = w.shape[2] // 2
    nj = n // tn
    b3 = _layer_vec(b)
    vmem = (2 * tm * k * 4 + ms * k * 4 + 4 * k * tn * 4 + 2 * k * tn * 2 + 2 * tm * tn * 4
            + (tm + CONV_HALO) * tn * 4 + GLU_SUB_ROWS * k * 2 + 6 * GLU_SUB_ROWS * tn * 4 + 8 * ms * tn * 4)
    return pl.pallas_call(
        functools.partial(_mm_glu_conv_body, seq_len // tm),
        grid=(nj, m // tm),
        in_specs=[pl.BlockSpec((tm, k), lambda j, i: (i, 0)),
                  _const_spec((ms, k), (0, 0)),
                  pl.BlockSpec((None, k, tn), lambda j, i: (layer, 0, j)),
                  pl.BlockSpec((None, k, tn), lambda j, i: (layer, 0, j + nj)),
                  pl.BlockSpec((None, 1, tn), lambda j, i: (layer, 0, j)),
                  pl.BlockSpec((None, 1, tn), lambda j, i: (layer, 0, j + nj)),
                  pl.BlockSpec((None, CONV_WIDTH, tn), lambda j, i: (layer, 0, j)),
                  pl.BlockSpec((None, 1, tn), lambda j, i: (layer, 0, j))],
        out_specs=[pl.BlockSpec((tm, tn), lambda j, i: (i, j)),
                   pl.BlockSpec((None, CONV_HALO, tn), lambda j, i: (i, 0, j)),
                   pl.BlockSpec((ms, tn), lambda j, i: (0, j))],
        out_shape=[jax.ShapeDtypeStruct((m, n), F32),
                   jax.ShapeDtypeStruct((m // tm, CONV_HALO, n), F32),
                   jax.ShapeDtypeStruct((ms, n), F32)],
        scratch_shapes=[pltpu.VMEM((k, tn), BF16), pltpu.VMEM((k, tn), BF16),
                        pltpu.VMEM((tm + CONV_HALO, tn), F32)],
        compiler_params=_params(("arbitrary", "arbitrary"), vmem),
        name="mm_glu_conv",
    )(x, x_s, w, w, b3, b3, wdw, _layer_vec(bdw))


def _conv_sample_body(p_ref, u_ref, w_ref, b_ref, y_ref, np_ref):
    hist = p_ref.shape[0]
    u = u_ref[...]
    acc = u * w_ref[pl.ds(hist, 1), :] + b_ref[...]
    for j in range(hist):
        acc = acc + p_ref[j] * w_ref[pl.ds(j, 1), :]
    y_ref[...] = acc
    for j in range(hist - 1):
        np_ref[j] = p_ref[j + 1]
    np_ref[hist - 1] = u


def conv_sample(past_t, u, w, b, layer, tc):
    _, hist, bsz, c = past_t.shape
    return pl.pallas_call(
        _conv_sample_body,
        grid=(c // tc,),
        in_specs=[pl.BlockSpec((None, hist, bsz, tc), lambda i: (layer, 0, 0, i)),
                  pl.BlockSpec((bsz, tc), lambda i: (0, i)),
                  pl.BlockSpec((None, CONV_WIDTH, tc), lambda i: (layer, 0, i)),
                  pl.BlockSpec((None, 1, tc), lambda i: (layer, 0, i))],
        out_specs=[pl.BlockSpec((bsz, tc), lambda i: (0, i)),
                   pl.BlockSpec((None, hist, bsz, tc), lambda i: (0, 0, 0, i))],
        out_shape=[jax.ShapeDtypeStruct((bsz, c), F32),
                   jax.ShapeDtypeStruct((1, hist, bsz, c), F32)],
        compiler_params=_params(("arbitrary",), 5 * hist * bsz * tc * 4),
        name="conv_sample",
    )(past_t, u, w, _layer_vec(b))


ROW_SUB = 256


def _mm_ln_body(pre_norm, *refs):
    if pre_norm:
        (xin_ref, res_ref, xins_ref, ress_ref, w_ref, b_ref, pg_ref, pb_ref, g_ref, be_ref,
         o_ref, ob_ref, os_ref, obs_ref) = refs
    else:
        (xin_ref, res_ref, xins_ref, ress_ref, w_ref, b_ref, g_ref, be_ref,
         o_ref, ob_ref, os_ref, obs_ref) = refs

    def rows_out(xin, res):
        if pre_norm:
            t = _ln(xin.astype(F32), pg_ref[...], pb_ref[...])
            xb = (t * _sigmoid(t)).astype(BF16)
        else:
            xb = xin.astype(BF16)
        mix = jnp.dot(xb, w_ref[...], preferred_element_type=F32) + b_ref[...]
        return _ln(ALPHA * res + mix, g_ref[...], be_ref[...])

    tm = o_ref.shape[0]
    sub = min(ROW_SUB, tm)
    for r0 in range(0, tm, sub):
        rows = pl.ds(r0, sub)
        out = rows_out(xin_ref[rows, :], res_ref[rows, :])
        o_ref[rows, :] = out
        ob_ref[rows, :] = out.astype(BF16)

    @pl.when(pl.program_id(0) == pl.num_programs(0) - 1)
    def _():
        out = rows_out(xins_ref[...], ress_ref[...])
        os_ref[...] = out
        obs_ref[...] = out.astype(BF16)


def mm_ln(xin, resid, xin_s, resid_s, w, b, g, be, mix_layer, ln_layer, tm, pre=None):
    m, k = xin.shape
    ms = xin_s.shape[0]
    n = w.shape[2]
    row = lambda i: (i, 0)
    fixed = lambda i: (0, 0)
    in_specs = [pl.BlockSpec((tm, k), row), pl.BlockSpec((tm, n), row),
                _const_spec((ms, k), (0, 0)), _const_spec((ms, n), (0, 0)),
                _const_spec((None, k, n), (mix_layer, 0, 0)), _vec_spec(n, mix_layer)]
    args = [xin, resid, xin_s, resid_s, w, _layer_vec(b)]
    if pre is not None:
        in_specs += [_vec_spec(k, mix_layer), _vec_spec(k, mix_layer)]
        args += [_layer_vec(pre[0]), _layer_vec(pre[1])]
    in_specs += [_vec_spec(n, ln_layer), _vec_spec(n, ln_layer)]
    args += [_layer_vec(g), _layer_vec(be)]
    vmem = (2 * tm * k * xin.dtype.itemsize + 4 * tm * n * 4 + 2 * tm * n * 2 + k * n * 2 + 6 * ROW_SUB * n * 4
            + 12 * ms * n * 4)
    return pl.pallas_call(
        functools.partial(_mm_ln_body, pre is not None),
        grid=(m // tm,),
        in_specs=in_specs,
        out_specs=[pl.BlockSpec((tm, n), row), pl.BlockSpec((tm, n), row),
                   pl.BlockSpec((ms, n), fixed), pl.BlockSpec((ms, n), fixed)],
        out_shape=[jax.ShapeDtypeStruct((m, n), F32), jax.ShapeDtypeStruct((m, n), BF16),
                   jax.ShapeDtypeStruct((ms, n), F32), jax.ShapeDtypeStruct((ms, n), BF16)],
        compiler_params=_params(("arbitrary",), vmem),
        name="mm_ln",
    )(*args)


FFN_TAIL = 8
FFN_SUB_ROWS = 256


def _ffn_hidden(g, g1, g2, up, wdw_ref, bdw_ref):
    gc = (wdw_ref[pl.ds(0, 1), :] * g2 + wdw_ref[pl.ds(1, 1), :] * g1 + wdw_ref[pl.ds(2, 1), :] * g
          + bdw_ref[...])
    return (gc * _sigmoid(gc) * up).astype(BF16)


def _ffn_up_body(tiles_per_seq, sub_rows, x_ref, xs_ref, p2_ref, p1_ref, wg_ref, wu_ref, wdw_ref,
                 bdw_ref, h_ref, gt_ref, hs_ref, ns_ref, wgb_ref, wub_ref, gs_ref):
    i = pl.program_id(1)
    tm, tn = h_ref.shape

    @pl.when(i == 0)
    def _():
        wgb_ref[...] = wg_ref[...].astype(BF16)
        wub_ref[...] = wu_ref[...].astype(BF16)

    @pl.when((i % tiles_per_seq) == 0)
    def _():
        gs_ref[pl.ds(0, FFN_TAIL), :] = jnp.zeros((FFN_TAIL, tn), F32)

    for r0 in range(0, tm, sub_rows):
        xr = x_ref[pl.ds(r0, sub_rows), :]
        g = jnp.dot(xr, wgb_ref[...], preferred_element_type=F32)
        up = jnp.dot(xr, wub_ref[...], preferred_element_type=F32)
        gs_ref[pl.ds(FFN_TAIL + r0, sub_rows), :] = g
        g1 = gs_ref[pl.ds(FFN_TAIL + r0 - 1, sub_rows), :]
        g2 = gs_ref[pl.ds(FFN_TAIL + r0 - 2, sub_rows), :]
        h_ref[pl.ds(r0, sub_rows), :] = _ffn_hidden(g, g1, g2, up, wdw_ref, bdw_ref)
    tail = gs_ref[pl.ds(tm, FFN_TAIL), :]
    gt_ref[...] = tail
    gs_ref[pl.ds(0, FFN_TAIL), :] = tail

    @pl.when(i == pl.num_programs(1) - 1)
    def _():
        xs = xs_ref[...]
        g = jnp.dot(xs, wgb_ref[...], preferred_element_type=F32)
        up = jnp.dot(xs, wub_ref[...], preferred_element_type=F32)
        p1 = p1_ref[...]
        ns_ref[0] = p1
        ns_ref[1] = g
        hs_ref[...] = _ffn_hidden(g, p1, p2_ref[...], up, wdw_ref, bdw_ref)


def ffn_up(xb, xb_s, past_s, wg, wu, wdw, bdw, layer, tm, tn, seq_len):
    m, k = xb.shape
    ms = xb_s.shape[0]
    dff = wg.shape[2]
    nj = dff // tn
    sub = min(FFN_SUB_ROWS, tm)
    vmem = (2 * tm * k * 2 + ms * k * 2 + 4 * k * tn * 4 + 2 * k * tn * 2 + 2 * tm * tn * 2 + tm * tn * 4
            + 8 * sub * tn * 4 + 16 * ms * tn * 4)
    return pl.pallas_call(
        functools.partial(_ffn_up_body, seq_len // tm, sub),
        grid=(nj, m // tm),
        in_specs=[pl.BlockSpec((tm, k), lambda j, i: (i, 0)),
                  _const_spec((ms, k), (0, 0)),
                  pl.BlockSpec((None, ms, tn), lambda j, i: (layer, 0, j)),
                  pl.BlockSpec((None, ms, tn), lambda j, i: (layer, 0, j + nj)),
                  pl.BlockSpec((None, k, tn), lambda j, i: (layer, 0, j)),
                  pl.BlockSpec((None, k, tn), lambda j, i: (layer, 0, j)),
                  pl.BlockSpec((None, FFN_CONV_WIDTH, tn), lambda j, i: (layer, 0, j)),
                  pl.BlockSpec((None, 1, tn), lambda j, i: (layer, 0, j))],
        out_specs=[pl.BlockSpec((tm, tn), lambda j, i: (i, j)),
                   pl.BlockSpec((None, FFN_TAIL, tn), lambda j, i: (i, 0, j)),
                   pl.BlockSpec((ms, tn), lambda j, i: (0, j)),
                   pl.BlockSpec((FFN_CONV_WIDTH - 1, ms, tn), lambda j, i: (0, 0, j))],
        out_shape=[jax.ShapeDtypeStruct((m, dff), BF16),
                   jax.ShapeDtypeStruct((m // tm, FFN_TAIL, dff), F32),
                   jax.ShapeDtypeStruct((ms, dff), BF16),
                   jax.ShapeDtypeStruct((FFN_CONV_WIDTH - 1, ms, dff), F32)],
        scratch_shapes=[pltpu.VMEM((k, tn), BF16), pltpu.VMEM((k, tn), BF16),
                        pltpu.VMEM((tm + FFN_TAIL, tn), F32)],
        compiler_params=_params(("arbitrary", "arbitrary"), vmem),
        name="ffn_up",
    )(xb, xb_s, past_s, past_s, wg, wu, wdw, _layer_vec(bdw))


def _ffn_down_body(h_ref, hs_ref, wd_ref, bd_ref, f_ref, fs_ref, wdb_ref):
    @pl.when(pl.program_id(1) == 0)
    def _():
        wdb_ref[...] = wd_ref[...].astype(BF16)

    f_ref[...] = bd_ref[...] + jnp.dot(h_ref[...], wdb_ref[...], preferred_element_type=F32)

    @pl.when(pl.program_id(1) == pl.num_programs(1) - 1)
    def _():
        fs_ref[...] = bd_ref[...] + jnp.dot(hs_ref[...], wdb_ref[...], preferred_element_type=F32)


def ffn_down(h, h_s, wd, bd, layer, tm, tn):
    m, dff = h.shape
    ms = h_s.shape[0]
    n = wd.shape[2]
    vmem = 2 * tm * dff * 2 + ms * dff * 2 + dff * tn * 4 + dff * tn * 2 + 4 * tm * tn * 4 + 4 * ms * tn * 4
    return pl.pallas_call(
        _ffn_down_body,
        grid=(n // tn, m // tm),
        in_specs=[pl.BlockSpec((tm, dff), lambda j, i: (i, 0)),
                  _const_spec((ms, dff), (0, 0)),
                  pl.BlockSpec((None, dff, tn), lambda j, i: (layer, 0, j), pipeline_mode=pl.Buffered(1)),
                  pl.BlockSpec((None, 1, tn), lambda j, i: (layer, 0, j))],
        out_specs=[pl.BlockSpec((tm, tn), lambda j, i: (i, j)),
                   pl.BlockSpec((ms, tn), lambda j, i: (0, j))],
        out_shape=[jax.ShapeDtypeStruct((m, n), F32), jax.ShapeDtypeStruct((ms, n), F32)],
        scratch_shapes=[pltpu.VMEM((dff, tn), BF16)],
        compiler_params=_params(("arbitrary", "arbitrary"), vmem),
        name="ffn_down",
    )(h, h_s, wd, _layer_vec(bd))


def _ple_body(emit_bf16, f_ref, x_ref, p_ref, fs_ref, xs_ref, ps_ref, lg_ref, lb_ref, wg_ref, wp_ref, g_ref,
              o_ref, os_ref, *bf16_refs):
    def rows_out(f, x, p):
        xn = _ln(ALPHA * x + f, lg_ref[...], lb_ref[...])
        gate = _sigmoid(jnp.dot(xn.astype(BF16), wg_ref[...], preferred_element_type=F32))
        e = jnp.dot(p.astype(BF16), wp_ref[...], preferred_element_type=F32)
        e = e * lax.rsqrt(jnp.mean(e * e, axis=-1, keepdims=True) + LN_EPS) * g_ref[...]
        return xn + gate * e

    tm = o_ref.shape[0]
    sub = min(ROW_SUB, tm)
    for r0 in range(0, tm, sub):
        rows = pl.ds(r0, sub)
        out = rows_out(f_ref[rows, :], x_ref[rows, :], p_ref[rows, :])
        o_ref[rows, :] = out
        if emit_bf16:
            bf16_refs[0][rows, :] = out.astype(BF16)

    @pl.when(pl.program_id(0) == pl.num_programs(0) - 1)
    def _():
        out = rows_out(fs_ref[...], xs_ref[...], ps_ref[...])
        os_ref[...] = out
        if emit_bf16:
            bf16_refs[1][...] = out.astype(BF16)


def ple(f, x, p, f_s, x_s, p_s, lg, lb, wg, wp, g, layer, tm, emit_bf16):
    m, k = f.shape
    ms = f_s.shape[0]
    dp = p.shape[2]
    row = lambda i: (i, 0)
    fixed = lambda i: (0, 0)
    out_specs = [pl.BlockSpec((tm, k), row), pl.BlockSpec((ms, k), fixed)]
    out_shape = [jax.ShapeDtypeStruct((m, k), F32), jax.ShapeDtypeStruct((ms, k), F32)]
    if emit_bf16:
        out_specs += [pl.BlockSpec((tm, k), row), pl.BlockSpec((ms, k), fixed)]
        out_shape += [jax.ShapeDtypeStruct((m, k), BF16), jax.ShapeDtypeStruct((ms, k), BF16)]
    vmem = (6 * tm * k * 4 + 2 * tm * k * 2 + 2 * tm * dp * 4 + k * k * 2 + dp * k * 2 + 6 * ROW_SUB * k * 4
            + 12 * ms * k * 4)
    return pl.pallas_call(
        functools.partial(_ple_body, emit_bf16),
        grid=(m // tm,),
        in_specs=[pl.BlockSpec((tm, k), row), pl.BlockSpec((tm, k), row),
                  pl.BlockSpec((None, tm, dp), lambda i: (layer, i, 0)),
                  _const_spec((ms, k), (0, 0)), _const_spec((ms, k), (0, 0)),
                  _const_spec((None, ms, dp), (layer, 0, 0)),
                  _vec_spec(k, layer), _vec_spec(k, layer),
                  _const_spec((None, k, k), (layer, 0, 0)),
                  _const_spec((None, dp, k), (layer, 0, 0)),
                  _vec_spec(k, layer)],
        out_specs=out_specs,
        out_shape=out_shape,
        compiler_params=_params(("arbitrary",), vmem),
        name="ple",
    )(f, x, p, f_s, x_s, p_s, _layer_vec(lg), _layer_vec(lb), wg, wp, _layer_vec(g))


WT_CHUNK = 512


def _stage_transposed(wt_ref, wb_ref):
    n = wt_ref.shape[0]
    chunk = min(WT_CHUNK, n)
    for c0 in range(0, n, chunk):
        wb_ref[:, pl.ds(c0, chunk)] = wt_ref[pl.ds(c0, chunk), :].T.astype(BF16)


def _mm_qkv_body(q_tiles, x_ref, xs_ref, wt_ref, o_ref, os_ref, wb_ref):
    @pl.when(pl.program_id(1) == 0)
    def _():
        _stage_transposed(wt_ref, wb_ref)

    scale = jnp.where(pl.program_id(0) < q_tiles, D_QK ** -0.5, 1.0).astype(F32)
    z = jnp.dot(x_ref[...], wb_ref[...], preferred_element_type=F32)
    o_ref[...] = (z * scale).astype(BF16)

    @pl.when(pl.program_id(1) == pl.num_programs(1) - 1)
    def _():
        zs = jnp.dot(xs_ref[...], wb_ref[...], preferred_element_type=F32)
        os_ref[...] = (zs * scale).astype(BF16)


def mm_qkv(xb, xb_s, wt, layer, n, tm, tn):
    m, k = xb.shape
    ms = xb_s.shape[0]
    vmem = (2 * tm * k * 2 + ms * k * 2 + 2 * k * tn * 4 + k * tn * 2 + 2 * tm * tn * 2 + 3 * tm * tn * 4
            + WT_CHUNK * k * 8 + 6 * ms * tn * 4)
    return pl.pallas_call(
        functools.partial(_mm_qkv_body, (N_HEADS * D_QK) // tn),
        grid=(n // tn, m // tm),
        in_specs=[pl.BlockSpec((tm, k), lambda j, i: (i, 0)),
                  _const_spec((ms, k), (0, 0)),
                  pl.BlockSpec((None, tn, k), lambda j, i: (layer, j, 0))],
        out_specs=[pl.BlockSpec((tm, tn), lambda j, i: (i, j)),
                   pl.BlockSpec((ms, tn), lambda j, i: (0, j))],
        out_shape=[jax.ShapeDtypeStruct((m, n), BF16), jax.ShapeDtypeStruct((ms, n), BF16)],
        scratch_shapes=[pltpu.VMEM((k, tn), BF16)],
        compiler_params=_params(("arbitrary", "arbitrary"), vmem),
        name="mm_qkv",
    )(xb, xb_s, wt)


def _mm_o_gates_body(x_ref, xs_ref, wot_ref, wgt_ref, o_ref, gt_ref, os_ref, gts_ref, wob_ref, wgb_ref):
    @pl.when(pl.program_id(0) == 0)
    def _():
        _stage_transposed(wot_ref, wob_ref)
        wg = wgt_ref[...].T
        lane = lax.broadcasted_iota(jnp.int32, wg.shape, 1)
        first = lane < N_HEADS
        wgb_ref[:, pl.ds(0, GATE_PAD)] = jnp.where(first, wg, 0.0).astype(BF16)
        wgb_ref[:, pl.ds(GATE_PAD, GATE_PAD)] = jnp.where(
            first, pltpu.roll(wg, GATE_PAD - N_HEADS, axis=1), 0.0).astype(BF16)

    xb = x_ref[...]
    o_ref[...] = jnp.dot(xb, wob_ref[...], preferred_element_type=F32)
    gt_ref[...] = jnp.dot(xb, wgb_ref[...], preferred_element_type=F32)

    @pl.when(pl.program_id(0) == pl.num_programs(0) - 1)
    def _():
        xs = xs_ref[...]
        os_ref[...] = jnp.dot(xs, wob_ref[...], preferred_element_type=F32)
        gts_ref[...] = jnp.dot(xs, wgb_ref[...], preferred_element_type=F32)


def mm_o_gates(xb, xb_s, wt, layer, tm):
    m, k = xb.shape
    ms = xb_s.shape[0]
    n = N_HEADS * D_V
    o_row = 2 * N_HEADS * D_QK + N_HEADS * D_V
    row = lambda i: (i, 0)
    vmem = (2 * tm * k * 2 + ms * k * 2 + k * n * 4 + k * n * 2 + GATE_PAD * k * 6
            + 2 * (tm + ms) * (n + 2 * GATE_PAD) * 4 + 2 * tm * n * 4 + WT_CHUNK * k * 8)
    return pl.pallas_call(
        _mm_o_gates_body,
        grid=(m // tm,),
        in_specs=[pl.BlockSpec((tm, k), row),
                  _const_spec((ms, k), (0, 0)),
                  _const_spec((None, n, k), (layer, o_row // n, 0)),
                  _const_spec((None, GATE_PAD, k), (layer, (o_row + n) // GATE_PAD, 0))],
        out_specs=[pl.BlockSpec((tm, n), row), pl.BlockSpec((tm, 2 * GATE_PAD), row),
                   pl.BlockSpec((ms, n), lambda i: (0, 0)), pl.BlockSpec((ms, 2 * GATE_PAD), lambda i: (0, 0))],
        out_shape=[jax.ShapeDtypeStruct((m, n), F32), jax.ShapeDtypeStruct((m, 2 * GATE_PAD), F32),
                   jax.ShapeDtypeStruct((ms, n), F32), jax.ShapeDtypeStruct((ms, 2 * GATE_PAD), F32)],
        scratch_shapes=[pltpu.VMEM((k, n), BF16), pltpu.VMEM((k, 2 * GATE_PAD), BF16)],
        compiler_params=_params(("arbitrary",), vmem),
        name="mm_o_gates",
    )(xb, xb_s, wt, wt)


def _mlstm_chunk_body(q_ref, k_ref, v_ref, o_ref, gi_ref, gf_ref, gr_ref, bi_ref, bf_ref, bcol_ref,
                      hng_ref, out_ref, c_ref, m_ref, s_ref, tot_ref):
    length = q_ref.shape[0]

    @pl.when(pl.program_id(1) == 0)
    def _():
        c_ref[...] = jnp.zeros(c_ref.shape, F32)
        m_ref[...] = jnp.zeros(m_ref.shape, F32)

    row = lax.broadcasted_iota(jnp.int32, (length, length), 0)
    col = lax.broadcasted_iota(jnp.int32, (length, length), 1)
    causal = row >= col
    triu = (row <= col).astype(F32)

    li_col = gi_ref[...] + bi_ref[...]
    bh_col = _row_scan(_log_sigmoid(gf_ref[...] + bf_ref[...]), jnp.add, 0.0)
    cm_col = _row_scan(li_col - bh_col, jnp.maximum, -jnp.inf)
    grow = gr_ref[...] + bcol_ref[...]
    li_row = grow[0:N_HEADS, :]
    bh_row = jnp.dot(_log_sigmoid(grow[N_HEADS:2 * N_HEADS, :]), triu,
                     precision=lax.Precision.HIGHEST, preferred_element_type=F32)
    a_row = li_row - bh_row
    inter = bh_col + m_ref[0:1, :]
    m_tok = jnp.maximum(inter, bh_col + cm_col)
    w_inter = jnp.exp(inter - m_tok)
    emt = jnp.exp(-m_tok)
    e_col = bh_col - m_tok
    m_new = m_tok[length - 1:length, :]
    g_state = w_inter[length - 1:length, :]
    g_tok = jnp.exp(bh_col[length - 1:length, :] - bh_col + li_col - m_new)
    m_ref[...] = jnp.broadcast_to(m_new, m_ref.shape)
    ones = jnp.ones((length, LANES), BF16)

    for h in range(N_HEADS):
        qk = slice(h * D_QK, (h + 1) * D_QK)
        w_intra = jnp.exp(jnp.where(causal, e_col[:, h:h + 1] + a_row[h:h + 1, :], -jnp.inf))
        s = lax.dot_general(q_ref[:, qk], k_ref[:, qk], (((1,), (1,)), ((), ())),
                            preferred_element_type=F32) * w_intra
        s_ref[h] = s.astype(BF16)

    for h in range(N_HEADS):
        qk = slice(h * D_QK, (h + 1) * D_QK)
        vaug = jnp.concatenate([v_ref[:, h * D_V:(h + 1) * D_V], ones], axis=1)
        c_old = c_ref[h]
        tot_ref[h] = (jnp.dot(s_ref[h], vaug, preferred_element_type=F32)
                      + jnp.dot(q_ref[:, qk], c_old.astype(BF16), preferred_element_type=F32)
                      * w_inter[:, h:h + 1])
        kg = (k_ref[:, qk].astype(F32) * g_tok[:, h:h + 1]).astype(BF16)
        c_ref[h] = g_state[:, h:h + 1] * c_old + lax.dot_general(
            kg, vaug, (((0,), (0,)), ((), ())), preferred_element_type=F32)

    for h in range(N_HEADS):
        tot = tot_ref[h]
        den = jnp.maximum(jnp.abs(tot[:, D_V:D_V + LANES]), emt[:, h:h + 1])
        inv = 1.0 / den
        halves = [tot[:, c0:c0 + LANES] * inv for c0 in range(0, D_V, LANES)]
        moments = jnp.concatenate([sum(halves), sum(x * x for x in halves)], axis=0).astype(BF16)
        stat = jnp.dot(moments, jnp.ones((LANES, LANES), BF16), preferred_element_type=F32) * (1.0 / D_V)
        mu = stat[0:length, :]
        rstd = lax.rsqrt(stat[length:2 * length, :] - mu * mu + LN_EPS)
        for idx, x in enumerate(halves):
            cols = pl.ds(h * D_V + idx * LANES, LANES)
            out_ref[:, cols] = (_sigmoid(o_ref[:, cols]) * ((x - mu) * rstd) * hng_ref[:, cols]).astype(BF16)


FORGET_PAD_BIAS = 30.0


def mlstm_chunk(qkv, o, gates, b_gates, hn_g, bsz, seq_len):
    length = MLSTM_CHUNK
    nch = seq_len // length
    hq = N_HEADS * D_QK
    hv = N_HEADS * D_V
    rows = lambda b, c: (b * nch + c, 0)
    pad = GATE_PAD - N_HEADS
    bi_row = jnp.pad(b_gates[0].reshape(1, N_HEADS), ((0, 0), (0, pad)))
    bf_row = jnp.pad(b_gates[1].reshape(1, N_HEADS), ((0, 0), (0, pad)), constant_values=FORGET_PAD_BIAS)
    bcol = b_gates.reshape(2 * N_HEADS, 1)
    gates_t = jnp.concatenate([gates[:, :N_HEADS], gates[:, GATE_PAD:GATE_PAD + N_HEADS]], axis=1).T
    const2 = lambda b, c: (0, 0)
    out, c_aug, m_new = pl.pallas_call(
        _mlstm_chunk_body,
        grid=(bsz, nch),
        in_specs=[pl.BlockSpec((length, hq), rows),
                  pl.BlockSpec((length, hq), lambda b, c: (b * nch + c, 1)),
                  pl.BlockSpec((length, hv), lambda b, c: (b * nch + c, 1)),
                  pl.BlockSpec((length, hv), rows),
                  pl.BlockSpec((length, GATE_PAD), rows),
                  pl.BlockSpec((length, GATE_PAD), lambda b, c: (b * nch + c, 1)),
                  pl.BlockSpec((2 * N_HEADS, length), lambda b, c: (0, b * nch + c)),
                  pl.BlockSpec((1, GATE_PAD), const2),
                  pl.BlockSpec((1, GATE_PAD), const2),
                  pl.BlockSpec((2 * N_HEADS, 1), const2),
                  pl.BlockSpec((1, hv), const2)],
        out_specs=[pl.BlockSpec((length, hv), rows),
                   pl.BlockSpec((None, N_HEADS, D_QK, D_V + LANES), lambda b, c: (b, 0, 0, 0)),
                   pl.BlockSpec((None, SUBLANES, GATE_PAD), lambda b, c: (b, 0, 0))],
        out_shape=[jax.ShapeDtypeStruct((bsz * seq_len, hv), BF16),
                   jax.ShapeDtypeStruct((bsz, N_HEADS, D_QK, D_V + LANES), F32),
                   jax.ShapeDtypeStruct((bsz, SUBLANES, GATE_PAD), F32)],
        scratch_shapes=[pltpu.VMEM((N_HEADS, length, length), BF16),
                        pltpu.VMEM((N_HEADS, length, D_V + LANES), F32)],
        compiler_params=_params(("arbitrary", "arbitrary"), 24 << 20),
        name="mlstm_chunk",
    )(qkv, qkv, qkv, o, gates, gates, gates_t, bi_row, bf_row, bcol, hn_g.reshape(1, hv))
    return out, c_aug[..., :D_V], c_aug[..., D_V], m_new[:, 0, :N_HEADS]


def _mlstm_step_body(q_ref, k_ref, kt_ref, v_ref, o_ref, gi_ref, gf_ref, c0_ref, n0_ref, m0_ref,
                     bi_ref, bf_ref, hng_ref, out_ref, c_ref, n_ref, m_ref, wv_ref, wi_ref):
    bb = q_ref.shape[0]
    li = gi_ref[...] + bi_ref[...]
    inter = _log_sigmoid(gf_ref[...] + bf_ref[...]) + m0_ref[...]
    m_tok = jnp.maximum(inter, li)
    w_intra = jnp.exp(li - m_tok)
    w_inter = jnp.exp(inter - m_tok)
    q = q_ref[...]
    k = k_ref[...]
    n_old = n0_ref[...]
    v = v_ref[...]
    s = jnp.sum(q * k, axis=-1, keepdims=True) * w_intra
    den = s + jnp.sum(q * n_old, axis=-1, keepdims=True) * w_inter
    den = jnp.maximum(jnp.abs(den), jnp.exp(-m_tok))
    n_ref[...] = w_inter * n_old + w_intra * k
    m_ref[...] = m_tok
    wv_ref[...] = w_intra * v
    wi_ref[...] = jnp.broadcast_to(w_inter, wi_ref.shape)

    head_row = lax.broadcasted_iota(jnp.int32, (N_HEADS, D_V), 0)
    qb16 = q.astype(BF16)
    qc_rows = []
    for b in range(bb):
        qc = jnp.zeros((N_HEADS, D_V), F32)
        for h in range(N_HEADS):
            c_old = c0_ref[b, h]
            all_heads = jnp.dot(qb16[b], c_old.astype(BF16), preferred_element_type=F32)
            qc = jnp.where(head_row == h, all_heads, qc)
            c_ref[b, h] = (wi_ref[b, pl.ds(h, 1), :] * c_old
                           + kt_ref[b, :, h:h + 1] * wv_ref[b, pl.ds(h, 1), :])
        qc_rows.append(qc)
    qc_all = jnp.stack(qc_rows, axis=0)
    hv = (s * v + qc_all * w_inter) * (1.0 / den)
    mu = jnp.mean(hv, axis=-1, keepdims=True)
    hc = hv - mu
    var = jnp.mean(hc * hc, axis=-1, keepdims=True)
    hn = hc * lax.rsqrt(var + LN_EPS) * hng_ref[...][None]
    out_ref[...] = _sigmoid(o_ref[...]) * hn


def mlstm_step(q, k, v, o, gates, c0, n0, m0, b_gates, hn_g, layer, bb):
    bsz = q.shape[0]
    kt = jnp.swapaxes(k, 1, 2)
    gi = gates[:, :N_HEADS, None]
    gf = gates[:, GATE_PAD:GATE_PAD + N_HEADS, None]
    m0 = m0[..., None]
    b3 = lambda i: (i, 0, 0)
    st3 = lambda i: (layer, i, 0, 0)
    bias = lambda v: v.reshape(N_HEADS, 1)
    out, c_new, n_new, m_new = pl.pallas_call(
        _mlstm_step_body,
        grid=(bsz // bb,),
        in_specs=[pl.BlockSpec((bb, N_HEADS, D_QK), b3), pl.BlockSpec((bb, N_HEADS, D_QK), b3),
                  pl.BlockSpec((bb, D_QK, N_HEADS), b3),
                  pl.BlockSpec((bb, N_HEADS, D_V), b3), pl.BlockSpec((bb, N_HEADS, D_V), b3),
                  pl.BlockSpec((bb, N_HEADS, 1), b3), pl.BlockSpec((bb, N_HEADS, 1), b3),
                  pl.BlockSpec((None, bb, N_HEADS, D_QK, D_V), lambda i: (layer, i, 0, 0, 0)),
                  pl.BlockSpec((None, bb, N_HEADS, D_QK), st3),
                  pl.BlockSpec((None, bb, N_HEADS, 1), st3),
                  _const_spec((N_HEADS, 1), (0, 0)), _const_spec((N_HEADS, 1), (0, 0)),
                  _const_spec((N_HEADS, D_V), (0, 0))],
        out_specs=[pl.BlockSpec((bb, N_HEADS, D_V), b3),
                   pl.BlockSpec((None, bb, N_HEADS, D_QK, D_V), lambda i: (0, i, 0, 0, 0)),
                   pl.BlockSpec((None, bb, N_HEADS, D_QK), lambda i: (0, i, 0, 0)),
                   pl.BlockSpec((None, bb, N_HEADS, 1), lambda i: (0, i, 0, 0))],
        out_shape=[jax.ShapeDtypeStruct((bsz, N_HEADS, D_V), F32),
                   jax.ShapeDtypeStruct((1, bsz, N_HEADS, D_QK, D_V), F32),
                   jax.ShapeDtypeStruct((1, bsz, N_HEADS, D_QK), F32),
                   jax.ShapeDtypeStruct((1, bsz, N_HEADS, 1), F32)],
        scratch_shapes=[pltpu.VMEM((bb, N_HEADS, D_V), F32), pltpu.VMEM((bb, N_HEADS, D_V), F32)],
        compiler_params=_params(("arbitrary",), 4 * bb * N_HEADS * D_QK * D_V * 4 + (8 << 20)),
        name="mlstm_step",
    )(q, k, kt, v, o, gi, gf, c0, n0, m0, bias(b_gates[0]), bias(b_gates[1]),
      hn_g.reshape(N_HEADS, D_V))
    return out, c_new, n_new, m_new[..., 0]


TM_GLU = 1024
TM_ROW = 512
TM_FFN_UP = 2048
TM_FFN_DOWN = 512
TM_QKV = 2048
TN_STRIPE = 512
TN_FFN_DOWN = 1024
TN_QKV = 1024
TC_CONV_SAMPLE = 256
BB_MLSTM_STEP = 8


def _trunk(xp, xs, pp, ps, bp, sp, states, w):
    conv_state_t, c_st, n_st, m_st, ffn_state = states
    bs = xs.shape[0]
    hq = N_HEADS * D_QK
    hv = N_HEADS * D_V
    new_ffn_p, new_ffn_s = [], []
    ffn_past = ffn_state.reshape(DEPTH, bs, (FFN_CONV_WIDTH - 1) * D_FF)

    def ffn_block(p32, pb, s32, sb, i, emit_bf16):
        hp, gtp, hs, hist_s = ffn_up(pb, sb, ffn_past, w["ff_w_gate"], w["ff_w_up"],
                                     w["ff_w_dw"], w["ff_b_dw"], i, TM_FFN_UP, TN_STRIPE, sp)
        tps = sp // TM_FFN_UP
        new_ffn_p.append(gtp.reshape(bp, tps, FFN_TAIL, D_FF)[:, tps - 1, FFN_TAIL - 2:, :])
        new_ffn_s.append(hist_s)
        fp, fs = ffn_down(hp, hs, w["ff_w_down"], w["ff_b_down"], i, TM_FFN_DOWN, TN_FFN_DOWN)
        return ple(fp, p32, pp, fs, s32, ps, w["ln_ffn_g"], w["ln_ffn_b"], w["pl_w_gate"], w["pl_w_proj"],
                   w["pl_g"], i, TM_ROW, emit_bf16)

    yp, tails, us = mm_glu_conv(xp, xs, w["cv_w_in"], w["cv_b_in"], w["cv_w_dw"], w["cv_b_dw"], 0,
                                TM_GLU, TN_STRIPE, sp)
    tps = sp // TM_GLU
    hist = CONV_WIDTH - 1
    conv_p = tails.reshape(bp, tps, CONV_HALO, D_MODEL)[:, tps - 1, CONV_HALO - hist:, :][None]
    ys, conv_s_t = conv_sample(conv_state_t, us, w["cv_w_dw"], w["cv_b_dw"], 0, TC_CONV_SAMPLE)
    conv_s = jnp.swapaxes(conv_s_t, 1, 2)
    p32, pb, s32, sb = mm_ln(yp, xp, ys, xs, w["cv_w_out"], w["cv_b_out"], w["ln_mix_g"], w["ln_mix_b"], 0, 0,
                             TM_ROW, pre=(w["cv_ln_g"], w["cv_ln_b"]))
    p32, s32, pb, sb = ffn_block(p32, pb, s32, sb, 0, True)

    qkv_p, qkv_s = mm_qkv(pb, sb, w["ml_w_in_t"], 0, 2 * hq + hv, TM_QKV, TN_QKV)
    o_p, gates_p, o_s, gates_s = mm_o_gates(pb, sb, w["ml_w_in_t"], 0, TM_ROW)
    mix_p, c_p, n_p, m_p = mlstm_chunk(qkv_p, o_p, gates_p, w["ml_b_gates"][0], w["ml_hn_g"][0], bp, sp)
    qf = qkv_s[:, :hq].astype(F32).reshape(bs, N_HEADS, D_QK)
    kf = qkv_s[:, hq:2 * hq].astype(F32).reshape(bs, N_HEADS, D_QK)
    vf = qkv_s[:, 2 * hq:].astype(F32).reshape(bs, N_HEADS, D_V)
    mix_s, c_s, n_s, m_s = mlstm_step(qf, kf, vf, o_s.reshape(bs, N_HEADS, D_V), gates_s, c_st, n_st, m_st,
                                      w["ml_b_gates"][0], w["ml_hn_g"][0], 0, BB_MLSTM_STEP)
    p32, pb, s32, sb = mm_ln(mix_p, p32, mix_s.reshape(bs, hv), s32, w["ml_w_out"], w["ml_b_out"],
                             w["ln_mix_g"], w["ln_mix_b"], 0, 1, TM_ROW)
    p32, s32 = ffn_block(p32, pb, s32, sb, 1, False)
    return (p32, s32, conv_p, conv_s, c_p[None], n_p[None], m_p[None], c_s, n_s, m_s,
            jnp.stack(new_ffn_p), jnp.swapaxes(jnp.stack(new_ffn_s), 1, 2))


def kernel(x_prompt, x_sample, p_prompt, p_sample, state_conv, state_mlstm_c, state_mlstm_n,
           state_mlstm_m, state_ffn_conv,
           cv_w_in, cv_b_in, cv_w_dw, cv_b_dw, cv_ln_g, cv_ln_b, cv_w_out, cv_b_out,
           ml_w_in, ml_b_gates, ml_hn_g, ml_w_out, ml_b_out,
           ln_mix_g, ln_mix_b, ln_ffn_g, ln_ffn_b,
           ff_w_gate, ff_w_up, ff_w_dw, ff_b_dw, ff_w_down, ff_b_down,
           pl_w_proj, pl_g, pl_w_gate):
    bp, sp, _ = x_prompt.shape
    bs, ss, _ = x_sample.shape
    w = dict(
        cv_w_in=cv_w_in, cv_b_in=cv_b_in, cv_w_dw=cv_w_dw, cv_b_dw=cv_b_dw,
        cv_ln_g=cv_ln_g, cv_ln_b=cv_ln_b, cv_w_out=cv_w_out.astype(BF16), cv_b_out=cv_b_out,
        ml_w_in_t=jnp.swapaxes(ml_w_in, 1, 2),
        ml_b_gates=ml_b_gates, ml_hn_g=ml_hn_g, ml_w_out=ml_w_out.astype(BF16), ml_b_out=ml_b_out,
        ln_mix_g=ln_mix_g, ln_mix_b=ln_mix_b, ln_ffn_g=ln_ffn_g, ln_ffn_b=ln_ffn_b,
        ff_w_gate=ff_w_gate, ff_w_up=ff_w_up, ff_w_dw=ff_w_dw,
        ff_b_dw=ff_b_dw, ff_w_down=ff_w_down, ff_b_down=ff_b_down,
        pl_w_proj=pl_w_proj.astype(BF16), pl_g=pl_g, pl_w_gate=pl_w_gate.astype(BF16),
    )
    yp, ys, conv_p, conv_s, c_p, n_p, m_p, c_s, n_s, m_s, ffn_p, ffn_s = _trunk(
        x_prompt.reshape(bp * sp, D_MODEL), x_sample.reshape(bs * ss, D_MODEL),
        p_prompt.reshape(DEPTH, bp * sp, D_PLE), p_sample.reshape(DEPTH, bs * ss, D_PLE), bp, sp,
        (jnp.swapaxes(state_conv, 1, 2), state_mlstm_c, state_mlstm_n, state_mlstm_m, state_ffn_conv), w)
    return (yp.reshape(bp, sp, D_MODEL), ys.reshape(bs, ss, D_MODEL), conv_p, conv_s,
            c_p, n_p, m_p, c_s, n_s, m_s, ffn_p, ffn_s)
```

```python
import functools

import jax
import jax.numpy as jnp
from jax import lax
from jax.experimental import pallas as pl
from jax.experimental.pallas import tpu as pltpu

D_MODEL = 2048
D_FF = 5632
D_PLE = 256
N_HEADS = 8
D_QK = 128
D_V = 256
CONV_WIDTH = 31
FFN_CONV_WIDTH = 3
DEPTH = 2
ALPHA = (2.0 * DEPTH) ** 0.25
LN_EPS = 1e-5
MLSTM_CHUNK = 128
GATE_PAD = 128
SUBLANES = 8

F32 = jnp.float32
BF16 = jnp.bfloat16

VMEM_CAP_BYTES = 58 * 1024 * 1024
VMEM_SLACK_BYTES = 8 * 1024 * 1024


def _params(sem, vmem_bytes):
    return pltpu.CompilerParams(
        dimension_semantics=sem,
        vmem_limit_bytes=min(int(vmem_bytes) + VMEM_SLACK_BYTES, VMEM_CAP_BYTES))


def _ln(x, g, b):
    mu = jnp.mean(x, axis=-1, keepdims=True)
    xc = x - mu
    var = jnp.mean(xc * xc, axis=-1, keepdims=True)
    return xc * lax.rsqrt(var + LN_EPS) * g + b


def _sigmoid(x):
    return 1.0 / (1.0 + jnp.exp(-x))


def _log_sigmoid(x):
    return jnp.minimum(x, 0.0) - jnp.log(1.0 + jnp.exp(-jnp.abs(x)))


def _row_scan(x, op, identity):
    rows, cols = x.shape
    shift = 1
    while shift < rows:
        shifted = jnp.concatenate([jnp.full((shift, cols), identity, x.dtype), x[:rows - shift, :]], axis=0)
        x = op(x, shifted)
        shift *= 2
    return x


def _const_spec(shape, index):
    return pl.BlockSpec(shape, lambda *_: index, pipeline_mode=pl.Buffered(1))


def _layer_vec(v):
    return v.reshape(v.shape[0], 1, v.shape[1])


def _vec_spec(n, layer):
    return _const_spec((None, 1, n), (layer, 0, 0))


CONV_ROWS = 64
CONV_HALO = 32
LANES = 128
GLU_SUB_ROWS = 256


def _conv_chunk(pad_ref, w_ref, b_ref, y_ref, row0, lane0):
    lanes = pl.ds(lane0, LANES)
    win = CONV_ROWS + SUBLANES
    acc = jnp.broadcast_to(b_ref[:, lanes], (CONV_ROWS, LANES))
    for r in range(SUBLANES):
        part = None
        for a in range((CONV_WIDTH - 1 - r) // SUBLANES + 1):
            d = SUBLANES * a + r
            start = row0 + CONV_HALO - SUBLANES * (a + 1)
            term = pad_ref[pl.ds(start, win), lanes] * w_ref[pl.ds(CONV_WIDTH - 1 - d, 1), lanes]
            part = term if part is None else part + term
        acc = acc + part[SUBLANES - r:SUBLANES - r + CONV_ROWS, :]
    y_ref[pl.ds(row0, CONV_ROWS), lanes] = acc


def _glu(xb, wab_ref, wgb_ref, ba_ref, bg_ref):
    a = jnp.dot(xb, wab_ref[...], preferred_element_type=F32) + ba_ref[...]
    g = jnp.dot(xb, wgb_ref[...], preferred_element_type=F32) + bg_ref[...]
    return a * _sigmoid(g)


def _mm_glu_conv_body(tiles_per_seq, x_ref, xs_ref, wa_ref, wg_ref, ba_ref, bg_ref, wdw_ref, bdw_ref,
                      y_ref, tail_ref, us_ref, wab_ref, wgb_ref, pad_ref):
    i = pl.program_id(1)
    tm, tn = y_ref.shape

    @pl.when(i == 0)
    def _():
        wab_ref[...] = wa_ref[...].astype(BF16)
        wgb_ref[...] = wg_ref[...].astype(BF16)

    @pl.when((i % tiles_per_seq) == 0)
    def _():
        pad_ref[pl.ds(0, CONV_HALO), :] = jnp.zeros((CONV_HALO, tn), F32)

    for r0 in range(0, tm, GLU_SUB_ROWS):
        xb = x_ref[pl.ds(r0, GLU_SUB_ROWS), :].astype(BF16)
        pad_ref[pl.ds(CONV_HALO + r0, GLU_SUB_ROWS), :] = _glu(xb, wab_ref, wgb_ref, ba_ref, bg_ref)
        for c0 in range(r0, r0 + GLU_SUB_ROWS, CONV_ROWS):
            for l0 in range(0, tn, LANES):
                _conv_chunk(pad_ref, wdw_ref, bdw_ref, y_ref, c0, l0)
    tail = pad_ref[pl.ds(tm, CONV_HALO), :]
    tail_ref[...] = tail
    pad_ref[pl.ds(0, CONV_HALO), :] = tail

    @pl.when(i == pl.num_programs(1) - 1)
    def _():
        us_ref[...] = _glu(xs_ref[...].astype(BF16), wab_ref, wgb_ref, ba_ref, bg_ref)


def mm_glu_conv(x, x_s, w, b, wdw, bdw, layer, tm, tn, seq_len):
    m, k = x.shape
    ms = x_s.shape[0]
    n = w.shape[2] // 2
    nj = n // tn
    b3 = _layer_vec(b)
    vmem = (2 * tm * k * 4 + ms * k * 4 + 4 * k * tn * 4 + 2 * k * tn * 2 + 2 * tm * tn * 4
            + (tm + CONV_HALO) * tn * 4 + GLU_SUB_ROWS * k * 2 + 6 * GLU_SUB_ROWS * tn * 4 + 8 * ms * tn * 4)
    return pl.pallas_call(
        functools.partial(_mm_glu_conv_body, seq_len // tm),
        grid=(nj, m // tm),
        in_specs=[pl.BlockSpec((tm, k), lambda j, i: (i, 0)),
                  _const_spec((ms, k), (0, 0)),
                  pl.BlockSpec((None, k, tn), lambda j, i: (layer, 0, j)),
                  pl.BlockSpec((None, k, tn), lambda j, i: (layer, 0, j + nj)),
                  pl.BlockSpec((None, 1, tn), lambda j, i: (layer, 0, j)),
                  pl.BlockSpec((None, 1, tn), lambda j, i: (layer, 0, j + nj)),
                  pl.BlockSpec((None, CONV_WIDTH, tn), lambda j, i: (layer, 0, j)),
                  pl.BlockSpec((None, 1, tn), lambda j, i: (layer, 0, j))],
        out_specs=[pl.BlockSpec((tm, tn), lambda j, i: (i, j)),
                   pl.BlockSpec((None, CONV_HALO, tn), lambda j, i: (i, 0, j)),
                   pl.BlockSpec((ms, tn), lambda j, i: (0, j))],
        out_shape=[jax.ShapeDtypeStruct((m, n), F32),
                   jax.ShapeDtypeStruct((m // tm, CONV_HALO, n), F32),
                   jax.ShapeDtypeStruct((ms, n), F32)],
        scratch_shapes=[pltpu.VMEM((k, tn), BF16), pltpu.VMEM((k, tn), BF16),
                        pltpu.VMEM((tm + CONV_HALO, tn), F32)],
        compiler_params=_params(("arbitrary", "arbitrary"), vmem),
        name="mm_glu_conv",
    )(x, x_s, w, w, b3, b3, wdw, _layer_vec(bdw))


def _conv_sample_body(p_ref, u_ref, w_ref, b_ref, y_ref, np_ref):
    hist = p_ref.shape[0]
    u = u_ref[...]
    acc = u * w_ref[pl.ds(hist, 1), :] + b_ref[...]
    for j in range(hist):
        acc = acc + p_ref[j] * w_ref[pl.ds(j, 1), :]
    y_ref[...] = acc
    for j in range(hist - 1):
        np_ref[j] = p_ref[j + 1]
    np_ref[hist - 1] = u


def conv_sample(past_t, u, w, b, layer, tc):
    _, hist, bsz, c = past_t.shape
    return pl.pallas_call(
        _conv_sample_body,
        grid=(c // tc,),
        in_specs=[pl.BlockSpec((None, hist, bsz, tc), lambda i: (layer, 0, 0, i)),
                  pl.BlockSpec((bsz, tc), lambda i: (0, i)),
                  pl.BlockSpec((None, CONV_WIDTH, tc), lambda i: (layer, 0, i)),
                  pl.BlockSpec((None, 1, tc), lambda i: (layer, 0, i))],
        out_specs=[pl.BlockSpec((bsz, tc), lambda i: (0, i)),
                   pl.BlockSpec((None, hist, bsz, tc), lambda i: (0, 0, 0, i))],
        out_shape=[jax.ShapeDtypeStruct((bsz, c), F32),
                   jax.ShapeDtypeStruct((1, hist, bsz, c), F32)],
        compiler_params=_params(("arbitrary",), 5 * hist * bsz * tc * 4),
        name="conv_sample",
    )(past_t, u, w, _layer_vec(b))


ROW_SUB = 256


def _mm_ln_body(pre_norm, *refs):
    if pre_norm:
        (xin_ref, res_ref, xins_ref, ress_ref, w_ref, b_ref, pg_ref, pb_ref, g_ref, be_ref,
         o_ref, ob_ref, os_ref, obs_ref) = refs
    else:
        (xin_ref, res_ref, xins_ref, ress_ref, w_ref, b_ref, g_ref, be_ref,
         o_ref, ob_ref, os_ref, obs_ref) = refs

    def rows_out(xin, res):
        if pre_norm:
            t = _ln(xin.astype(F32), pg_ref[...], pb_ref[...])
            xb = (t * _sigmoid(t)).astype(BF16)
        else:
            xb = xin.astype(BF16)
        mix = jnp.dot(xb, w_ref[...], preferred_element_type=F32) + b_ref[...]
        return _ln(ALPHA * res + mix, g_ref[...], be_ref[...])

    tm = o_ref.shape[0]
    sub = min(ROW_SUB, tm)
    for r0 in range(0, tm, sub):
        rows = pl.ds(r0, sub)
        out = rows_out(xin_ref[rows, :], res_ref[rows, :])
        o_ref[rows, :] = out
        ob_ref[rows, :] = out.astype(BF16)

    @pl.when(pl.program_id(0) == pl.num_programs(0) - 1)
    def _():
        out = rows_out(xins_ref[...], ress_ref[...])
        os_ref[...] = out
        obs_ref[...] = out.astype(BF16)


def mm_ln(xin, resid, xin_s, resid_s, w, b, g, be, mix_layer, ln_layer, tm, pre=None):
    m, k = xin.shape
    ms = xin_s.shape[0]
    n = w.shape[2]
    row = lambda i: (i, 0)
    fixed = lambda i: (0, 0)
    in_specs = [pl.BlockSpec((tm, k), row), pl.BlockSpec((tm, n), row),
                _const_spec((ms, k), (0, 0)), _const_spec((ms, n), (0, 0)),
                _const_spec((None, k, n), (mix_layer, 0, 0)), _vec_spec(n, mix_layer)]
    args = [xin, resid, xin_s, resid_s, w, _layer_vec(b)]
    if pre is not None:
        in_specs += [_vec_spec(k, mix_layer), _vec_spec(k, mix_layer)]
        args += [_layer_vec(pre[0]), _layer_vec(pre[1])]
    in_specs += [_vec_spec(n, ln_layer), _vec_spec(n, ln_layer)]
    args += [_layer_vec(g), _layer_vec(be)]
    vmem = (2 * tm * k * xin.dtype.itemsize + 4 * tm * n * 4 + 2 * tm * n * 2 + k * n * 2 + 6 * ROW_SUB * n * 4
            + 12 * ms * n * 4)
    return pl.pallas_call(
        functools.partial(_mm_ln_body, pre is not None),
        grid=(m // tm,),
        in_specs=in_specs,
        out_specs=[pl.BlockSpec((tm, n), row), pl.BlockSpec((tm, n), row),
                   pl.BlockSpec((ms, n), fixed), pl.BlockSpec((ms, n), fixed)],
        out_shape=[jax.ShapeDtypeStruct((m, n), F32), jax.ShapeDtypeStruct((m, n), BF16),
                   jax.ShapeDtypeStruct((ms, n), F32), jax.ShapeDtypeStruct((ms, n), BF16)],
        compiler_params=_params(("arbitrary",), vmem),
        name="mm_ln",
    )(*args)


FFN_TAIL = 8
FFN_SUB_ROWS = 256


def _ffn_hidden(g, g1, g2, up, wdw_ref, bdw_ref):
    gc = (wdw_ref[pl.ds(0, 1), :] * g2 + wdw_ref[pl.ds(1, 1), :] * g1 + wdw_ref[pl.ds(2, 1), :] * g
          + bdw_ref[...])
    return (gc * _sigmoid(gc) * up).astype(BF16)


def _ffn_up_body(tiles_per_seq, sub_rows, x_ref, xs_ref, p2_ref, p1_ref, wg_ref, wu_ref, wdw_ref,
                 bdw_ref, h_ref, gt_ref, hs_ref, ns_ref, wgb_ref, wub_ref, gs_ref):
    i = pl.program_id(1)
    tm, tn = h_ref.shape

    @pl.when(i == 0)
    def _():
        wgb_ref[...] = wg_ref[...].astype(BF16)
        wub_ref[...] = wu_ref[...].astype(BF16)

    @pl.when((i % tiles_per_seq) == 0)
    def _():
        gs_ref[pl.ds(0, FFN_TAIL), :] = jnp.zeros((FFN_TAIL, tn), F32)

    for r0 in range(0, tm, sub_rows):
        xr = x_ref[pl.ds(r0, sub_rows), :]
        g = jnp.dot(xr, wgb_ref[...], preferred_element_type=F32)
        up = jnp.dot(xr, wub_ref[...], preferred_element_type=F32)
        gs_ref[pl.ds(FFN_TAIL + r0, sub_rows), :] = g
        g1 = gs_ref[pl.ds(FFN_TAIL + r0 - 1, sub_rows), :]
        g2 = gs_ref[pl.ds(FFN_TAIL + r0 - 2, sub_rows), :]
        h_ref[pl.ds(r0, sub_rows), :] = _ffn_hidden(g, g1, g2, up, wdw_ref, bdw_ref)
    tail = gs_ref[pl.ds(tm, FFN_TAIL), :]
    gt_ref[...] = tail
    gs_ref[pl.ds(0, FFN_TAIL), :] = tail

    @pl.when(i == pl.num_programs(1) - 1)
    def _():
        xs = xs_ref[...]
        g = jnp.dot(xs, wgb_ref[...], preferred_element_type=F32)
        up = jnp.dot(xs, wub_ref[...], preferred_element_type=F32)
        p1 = p1_ref[...]
        ns_ref[0] = p1
        ns_ref[1] = g
        hs_ref[...] = _ffn_hidden(g, p1, p2_ref[...], up, wdw_ref, bdw_ref)


def ffn_up(xb, xb_s, past_s, wg, wu, wdw, bdw, layer, tm, tn, seq_len):
    m, k = xb.shape
    ms = xb_s.shape[0]
    dff = wg.shape[2]
    nj = dff // tn
    sub = min(FFN_SUB_ROWS, tm)
    vmem = (2 * tm * k * 2 + ms * k * 2 + 4 * k * tn * 4 + 2 * k * tn * 2 + 2 * tm * tn * 2 + tm * tn * 4
            + 8 * sub * tn * 4 + 16 * ms * tn * 4)
    return pl.pallas_call(
        functools.partial(_ffn_up_body, seq_len // tm, sub),
        grid=(nj, m // tm),
        in_specs=[pl.BlockSpec((tm, k), lambda j, i: (i, 0)),
                  _const_spec((ms, k), (0, 0)),
                  pl.BlockSpec((None, ms, tn), lambda j, i: (layer, 0, j)),
                  pl.BlockSpec((None, ms, tn), lambda j, i: (layer, 0, j + nj)),
                  pl.BlockSpec((None, k, tn), lambda j, i: (layer, 0, j)),
                  pl.BlockSpec((None, k, tn), lambda j, i: (layer, 0, j)),
                  pl.BlockSpec((None, FFN_CONV_WIDTH, tn), lambda j, i: (layer, 0, j)),
                  pl.BlockSpec((None, 1, tn), lambda j, i: (layer, 0, j))],
        out_specs=[pl.BlockSpec((tm, tn), lambda j, i: (i, j)),
                   pl.BlockSpec((None, FFN_TAIL, tn), lambda j, i: (i, 0, j)),
                   pl.BlockSpec((ms, tn), lambda j, i: (0, j)),
                   pl.BlockSpec((FFN_CONV_WIDTH - 1, ms, tn), lambda j, i: (0, 0, j))],
        out_shape=[jax.ShapeDtypeStruct((m, dff), BF16),
                   jax.ShapeDtypeStruct((m // tm, FFN_TAIL, dff), F32),
                   jax.ShapeDtypeStruct((ms, dff), BF16),
                   jax.ShapeDtypeStruct((FFN_CONV_WIDTH - 1, ms, dff), F32)],
        scratch_shapes=[pltpu.VMEM((k, tn), BF16), pltpu.VMEM((k, tn), BF16),
                        pltpu.VMEM((tm + FFN_TAIL, tn), F32)],
        compiler_params=_params(("arbitrary", "arbitrary"), vmem),
        name="ffn_up",
    )(xb, xb_s, past_s, past_s, wg, wu, wdw, _layer_vec(bdw))


def _ffn_down_body(h_ref, hs_ref, wd_ref, bd_ref, f_ref, fs_ref, wdb_ref):
    @pl.when(pl.program_id(1) == 0)
    def _():
        wdb_ref[...] = wd_ref[...].astype(BF16)

    f_ref[...] = bd_ref[...] + jnp.dot(h_ref[...], wdb_ref[...], preferred_element_type=F32)

    @pl.when(pl.program_id(1) == pl.num_programs(1) - 1)
    def _():
        fs_ref[...] = bd_ref[...] + jnp.dot(hs_ref[...], wdb_ref[...], preferred_element_type=F32)


def ffn_down(h, h_s, wd, bd, layer, tm, tn):
    m, dff = h.shape
    ms = h_s.shape[0]
    n = wd.shape[2]
    vmem = 2 * tm * dff * 2 + ms * dff * 2 + dff * tn * 4 + dff * tn * 2 + 4 * tm * tn * 4 + 4 * ms * tn * 4
    return pl.pallas_call(
        _ffn_down_body,
        grid=(n // tn, m // tm),
        in_specs=[pl.BlockSpec((tm, dff), lambda j, i: (i, 0)),
                  _const_spec((ms, dff), (0, 0)),
                  pl.BlockSpec((None, dff, tn), lambda j, i: (layer, 0, j), pipeline_mode=pl.Buffered(1)),
                  pl.BlockSpec((None, 1, tn), lambda j, i: (layer, 0, j))],
        out_specs=[pl.BlockSpec((tm, tn), lambda j, i: (i, j)),
                   pl.BlockSpec((ms, tn), lambda j, i: (0, j))],
        out_shape=[jax.ShapeDtypeStruct((m, n), F32), jax.ShapeDtypeStruct((ms, n), F32)],
        scratch_shapes=[pltpu.VMEM((dff, tn), BF16)],
        compiler_params=_params(("arbitrary", "arbitrary"), vmem),
        name="ffn_down",
    )(h, h_s, wd, _layer_vec(bd))


def _ple_body(emit_bf16, f_ref, x_ref, p_ref, fs_ref, xs_ref, ps_ref, lg_ref, lb_ref, wg_ref, wp_ref, g_ref,
              o_ref, os_ref, *bf16_refs):
    def rows_out(f, x, p):
        xn = _ln(ALPHA * x + f, lg_ref[...], lb_ref[...])
        gate = _sigmoid(jnp.dot(xn.astype(BF16), wg_ref[...], preferred_element_type=F32))
        e = jnp.dot(p.astype(BF16), wp_ref[...], preferred_element_type=F32)
        e = e * lax.rsqrt(jnp.mean(e * e, axis=-1, keepdims=True) + LN_EPS) * g_ref[...]
        return xn + gate * e

    tm = o_ref.shape[0]
    sub = min(ROW_SUB, tm)
    for r0 in range(0, tm, sub):
        rows = pl.ds(r0, sub)
        out = rows_out(f_ref[rows, :], x_ref[rows, :], p_ref[rows, :])
        o_ref[rows, :] = out
        if emit_bf16:
            bf16_refs[0][rows, :] = out.astype(BF16)

    @pl.when(pl.program_id(0) == pl.num_programs(0) - 1)
    def _():
        out = rows_out(fs_ref[...], xs_ref[...], ps_ref[...])
        os_ref[...] = out
        if emit_bf16:
            bf16_refs[1][...] = out.astype(BF16)


def ple(f, x, p, f_s, x_s, p_s, lg, lb, wg, wp, g, layer, tm, emit_bf16):
    m, k = f.shape
    ms = f_s.shape[0]
    dp = p.shape[2]
    row = lambda i: (i, 0)
    fixed = lambda i: (0, 0)
    out_specs = [pl.BlockSpec((tm, k), row), pl.BlockSpec((ms, k), fixed)]
    out_shape = [jax.ShapeDtypeStruct((m, k), F32), jax.ShapeDtypeStruct((ms, k), F32)]
    if emit_bf16:
        out_specs += [pl.BlockSpec((tm, k), row), pl.BlockSpec((ms, k), fixed)]
        out_shape += [jax.ShapeDtypeStruct((m, k), BF16), jax.ShapeDtypeStruct((ms, k), BF16)]
    vmem = (6 * tm * k * 4 + 2 * tm * k * 2 + 2 * tm * dp * 4 + k * k * 2 + dp * k * 2 + 6 * ROW_SUB * k * 4
            + 12 * ms * k * 4)
    return pl.pallas_call(
        functools.partial(_ple_body, emit_bf16),
        grid=(m // tm,),
        in_specs=[pl.BlockSpec((tm, k), row), pl.BlockSpec((tm, k), row),
                  pl.BlockSpec((None, tm, dp), lambda i: (layer, i, 0)),
                  _const_spec((ms, k), (0, 0)), _const_spec((ms, k), (0, 0)),
                  _const_spec((None, ms, dp), (layer, 0, 0)),
                  _vec_spec(k, layer), _vec_spec(k, layer),
                  _const_spec((None, k, k), (layer, 0, 0)),
                  _const_spec((None, dp, k), (layer, 0, 0)),
                  _vec_spec(k, layer)],
        out_specs=out_specs,
        out_shape=out_shape,
        compiler_params=_params(("arbitrary",), vmem),
        name="ple",
    )(f, x, p, f_s, x_s, p_s, _layer_vec(lg), _layer_vec(lb), wg, wp, _layer_vec(g))


WT_CHUNK = 512


def _stage_transposed(wt_ref, wb_ref):
    n = wt_ref.shape[0]
    chunk = min(WT_CHUNK, n)
    for c0 in range(0, n, chunk):
        wb_ref[:, pl.ds(c0, chunk)] = wt_ref[pl.ds(c0, chunk), :].T.astype(BF16)


def _mm_qkv_body(q_tiles, x_ref, xs_ref, wt_ref, o_ref, os_ref, wb_ref):
    @pl.when(pl.program_id(1) == 0)
    def _():
        _stage_transposed(wt_ref, wb_ref)

    scale = jnp.where(pl.program_id(0) < q_tiles, D_QK ** -0.5, 1.0).astype(F32)
    z = jnp.dot(x_ref[...], wb_ref[...], preferred_element_type=F32)
    o_ref[...] = (z * scale).astype(BF16)

    @pl.when(pl.program_id(1) == pl.num_programs(1) - 1)
    def _():
        zs = jnp.dot(xs_ref[...], wb_ref[...], preferred_element_type=F32)
        os_ref[...] = (zs * scale).astype(BF16)


def mm_qkv(xb, xb_s, wt, layer, n, tm, tn):
    m, k = xb.shape
    ms = xb_s.shape[0]
    vmem = (2 * tm * k * 2 + ms * k * 2 + 2 * k * tn * 4 + k * tn * 2 + 2 * tm * tn * 2 + 3 * tm * tn * 4
            + WT_CHUNK * k * 8 + 6 * ms * tn * 4)
    return pl.pallas_call(
        functools.partial(_mm_qkv_body, (N_HEADS * D_QK) // tn),
        grid=(n // tn, m // tm),
        in_specs=[pl.BlockSpec((tm, k), lambda j, i: (i, 0)),
                  _const_spec((ms, k), (0, 0)),
                  pl.BlockSpec((None, tn, k), lambda j, i: (layer, j, 0))],
        out_specs=[pl.BlockSpec((tm, tn), lambda j, i: (i, j)),
                   pl.BlockSpec((ms, tn), lambda j, i: (0, j))],
        out_shape=[jax.ShapeDtypeStruct((m, n), BF16), jax.ShapeDtypeStruct((ms, n), BF16)],
        scratch_shapes=[pltpu.VMEM((k, tn), BF16)],
        compiler_params=_params(("arbitrary", "arbitrary"), vmem),
        name="mm_qkv",
    )(xb, xb_s, wt)


def _mm_o_gates_body(x_ref, xs_ref, wot_ref, wgt_ref, o_ref, gt_ref, os_ref, gts_ref, wob_ref, wgb_ref):
    @pl.when(pl.program_id(0) == 0)
    def _():
        _stage_transposed(wot_ref, wob_ref)
        wg = wgt_ref[...].T
        lane = lax.broadcasted_iota(jnp.int32, wg.shape, 1)
        first = lane < N_HEADS
        wgb_ref[:, pl.ds(0, GATE_PAD)] = jnp.where(first, wg, 0.0).astype(BF16)
        wgb_ref[:, pl.ds(GATE_PAD, GATE_PAD)] = jnp.where(
            first, pltpu.roll(wg, GATE_PAD - N_HEADS, axis=1), 0.0).astype(BF16)

    xb = x_ref[...]
    o_ref[...] = jnp.dot(xb, wob_ref[...], preferred_element_type=F32)
    gt_ref[...] = jnp.dot(xb, wgb_ref[...], preferred_element_type=F32)

    @pl.when(pl.program_id(0) == pl.num_programs(0) - 1)
    def _():
        xs = xs_ref[...]
        os_ref[...] = jnp.dot(xs, wob_ref[...], preferred_element_type=F32)
        gts_ref[...] = jnp.dot(xs, wgb_ref[...], preferred_element_type=F32)


def mm_o_gates(xb, xb_s, wt, layer, tm):
    m, k = xb.shape
    ms = xb_s.shape[0]
    n = N_HEADS * D_V
    o_row = 2 * N_HEADS * D_QK + N_HEADS * D_V
    row = lambda i: (i, 0)
    vmem = (2 * tm * k * 2 + ms * k * 2 + k * n * 4 + k * n * 2 + GATE_PAD * k * 6
            + 2 * (tm + ms) * (n + 2 * GATE_PAD) * 4 + 2 * tm * n * 4 + WT_CHUNK * k * 8)
    return pl.pallas_call(
        _mm_o_gates_body,
        grid=(m // tm,),
        in_specs=[pl.BlockSpec((tm, k), row),
                  _const_spec((ms, k), (0, 0)),
                  _const_spec((None, n, k), (layer, o_row // n, 0)),
                  _const_spec((None, GATE_PAD, k), (layer, (o_row + n) // GATE_PAD, 0))],
        out_specs=[pl.BlockSpec((tm, n), row), pl.BlockSpec((tm, 2 * GATE_PAD), row),
                   pl.BlockSpec((ms, n), lambda i: (0, 0)), pl.BlockSpec((ms, 2 * GATE_PAD), lambda i: (0, 0))],
        out_shape=[jax.ShapeDtypeStruct((m, n), F32), jax.ShapeDtypeStruct((m, 2 * GATE_PAD), F32),
                   jax.ShapeDtypeStruct((ms, n), F32), jax.ShapeDtypeStruct((ms, 2 * GATE_PAD), F32)],
        scratch_shapes=[pltpu.VMEM((k, n), BF16), pltpu.VMEM((k, 2 * GATE_PAD), BF16)],
        compiler_params=_params(("arbitrary",), vmem),
        name="mm_o_gates",
    )(xb, xb_s, wt, wt)


def _mlstm_chunk_body(q_ref, k_ref, v_ref, o_ref, gi_ref, gf_ref, gr_ref, bi_ref, bf_ref, bcol_ref,
                      hng_ref, out_ref, c_ref, m_ref, s_ref, tot_ref):
    length = q_ref.shape[0]

    @pl.when(pl.program_id(1) == 0)
    def _():
        c_ref[...] = jnp.zeros(c_ref.shape, F32)
        m_ref[...] = jnp.zeros(m_ref.shape, F32)

    row = lax.broadcasted_iota(jnp.int32, (length, length), 0)
    col = lax.broadcasted_iota(jnp.int32, (length, length), 1)
    causal = row >= col
    triu = (row <= col).astype(F32)

    li_col = gi_ref[...] + bi_ref[...]
    bh_col = _row_scan(_log_sigmoid(gf_ref[...] + bf_ref[...]), jnp.add, 0.0)
    cm_col = _row_scan(li_col - bh_col, jnp.maximum, -jnp.inf)
    grow = gr_ref[...] + bcol_ref[...]
    li_row = grow[0:N_HEADS, :]
    bh_row = jnp.dot(_log_sigmoid(grow[N_HEADS:2 * N_HEADS, :]), triu,
                     precision=lax.Precision.HIGHEST, preferred_element_type=F32)
    a_row = li_row - bh_row
    inter = bh_col + m_ref[0:1, :]
    m_tok = jnp.maximum(inter, bh_col + cm_col)
    w_inter = jnp.exp(inter - m_tok)
    emt = jnp.exp(-m_tok)
    e_col = bh_col - m_tok
    m_new = m_tok[length - 1:length, :]
    g_state = w_inter[length - 1:length, :]
    g_tok = jnp.exp(bh_col[length - 1:length, :] - bh_col + li_col - m_new)
    m_ref[...] = jnp.broadcast_to(m_new, m_ref.shape)
    ones = jnp.ones((length, LANES), BF16)

    for h in range(N_HEADS):
        qk = slice(h * D_QK, (h + 1) * D_QK)
        w_intra = jnp.exp(jnp.where(causal, e_col[:, h:h + 1] + a_row[h:h + 1, :], -jnp.inf))
        s = lax.dot_general(q_ref[:, qk], k_ref[:, qk], (((1,), (1,)), ((), ())),
                            preferred_element_type=F32) * w_intra
        s_ref[h] = s.astype(BF16)

    for h in range(N_HEADS):
        qk = slice(h * D_QK, (h + 1) * D_QK)
        vaug = jnp.concatenate([v_ref[:, h * D_V:(h + 1) * D_V], ones], axis=1)
        c_old = c_ref[h]
        tot_ref[h] = (jnp.dot(s_ref[h], vaug, preferred_element_type=F32)
                      + jnp.dot(q_ref[:, qk], c_old.astype(BF16), preferred_element_type=F32)
                      * w_inter[:, h:h + 1])
        kg = (k_ref[:, qk].astype(F32) * g_tok[:, h:h + 1]).astype(BF16)
        c_ref[h] = g_state[:, h:h + 1] * c_old + lax.dot_general(
            kg, vaug, (((0,), (0,)), ((), ())), preferred_element_type=F32)

    for h in range(N_HEADS):
        tot = tot_ref[h]
        den = jnp.maximum(jnp.abs(tot[:, D_V:D_V + LANES]), emt[:, h:h + 1])
        inv = 1.0 / den
        halves = [tot[:, c0:c0 + LANES] * inv for c0 in range(0, D_V, LANES)]
        moments = jnp.concatenate([sum(halves), sum(x * x for x in halves)], axis=0).astype(BF16)
        stat = jnp.dot(moments, jnp.ones((LANES, LANES), BF16), preferred_element_type=F32) * (1.0 / D_V)
        mu = stat[0:length, :]
        rstd = lax.rsqrt(stat[length:2 * length, :] - mu * mu + LN_EPS)
        for idx, x in enumerate(halves):
            cols = pl.ds(h * D_V + idx * LANES, LANES)
            out_ref[:, cols] = (_sigmoid(o_ref[:, cols]) * ((x - mu) * rstd) * hng_ref[:, cols]).astype(BF16)


FORGET_PAD_BIAS = 30.0


def mlstm_chunk(qkv, o, gates, b_gates, hn_g, bsz, seq_len):
    length = MLSTM_CHUNK
    nch = seq_len // length
    hq = N_HEADS * D_QK
    hv = N_HEADS * D_V
    rows = lambda b, c: (b * nch + c, 0)
    pad = GATE_PAD - N_HEADS
    bi_row = jnp.pad(b_gates[0].reshape(1, N_HEADS), ((0, 0), (0, pad)))
    bf_row = jnp.pad(b_gates[1].reshape(1, N_HEADS), ((0, 0), (0, pad)), constant_values=FORGET_PAD_BIAS)
    bcol = b_gates.reshape(2 * N_HEADS, 1)
    gates_t = jnp.concatenate([gates[:, :N_HEADS], gates[:, GATE_PAD:GATE_PAD + N_HEADS]], axis=1).T
    const2 = lambda b, c: (0, 0)
    out, c_aug, m_new = pl.pallas_call(
        _mlstm_chunk_body,
        grid=(bsz, nch),
        in_specs=[pl.BlockSpec((length, hq), rows),
                  pl.BlockSpec((length, hq), lambda b, c: (b * nch + c, 1)),
                  pl.BlockSpec((length, hv), lambda b, c: (b * nch + c, 1)),
                  pl.BlockSpec((length, hv), rows),
                  pl.BlockSpec((length, GATE_PAD), rows),
                  pl.BlockSpec((length, GATE_PAD), lambda b, c: (b * nch + c, 1)),
                  pl.BlockSpec((2 * N_HEADS, length), lambda b, c: (0, b * nch + c)),
                  pl.BlockSpec((1, GATE_PAD), const2),
                  pl.BlockSpec((1, GATE_PAD), const2),
                  pl.BlockSpec((2 * N_HEADS, 1), const2),
                  pl.BlockSpec((1, hv), const2)],
        out_specs=[pl.BlockSpec((length, hv), rows),
                   pl.BlockSpec((None, N_HEADS, D_QK, D_V + LANES), lambda b, c: (b, 0, 0, 0)),
                   pl.BlockSpec((None, SUBLANES, GATE_PAD), lambda b, c: (b, 0, 0))],
        out_shape=[jax.ShapeDtypeStruct((bsz * seq_len, hv), BF16),
                   jax.ShapeDtypeStruct((bsz, N_HEADS, D_QK, D_V + LANES), F32),
                   jax.ShapeDtypeStruct((bsz, SUBLANES, GATE_PAD), F32)],
        scratch_shapes=[pltpu.VMEM((N_HEADS, length, length), BF16),
                        pltpu.VMEM((N_HEADS, length, D_V + LANES), F32)],
        compiler_params=_params(("arbitrary", "arbitrary"), 24 << 20),
        name="mlstm_chunk",
    )(qkv, qkv, qkv, o, gates, gates, gates_t, bi_row, bf_row, bcol, hn_g.reshape(1, hv))
    return out, c_aug[..., :D_V], c_aug[..., D_V], m_new[:, 0, :N_HEADS]


def _mlstm_step_body(q_ref, k_ref, kt_ref, v_ref, o_ref, gi_ref, gf_ref, c0_ref, n0_ref, m0_ref,
                     bi_ref, bf_ref, hng_ref, out_ref, c_ref, n_ref, m_ref, wv_ref, wi_ref):
    bb = q_ref.shape[0]
    li = gi_ref[...] + bi_ref[...]
    inter = _log_sigmoid(gf_ref[...] + bf_ref[...]) + m0_ref[...]
    m_tok = jnp.maximum(inter, li)
    w_intra = jnp.exp(li - m_tok)
    w_inter = jnp.exp(inter - m_tok)
    q = q_ref[...]
    k = k_ref[...]
    n_old = n0_ref[...]
    v = v_ref[...]
    s = jnp.sum(q * k, axis=-1, keepdims=True) * w_intra
    den = s + jnp.sum(q * n_old, axis=-1, keepdims=True) * w_inter
    den = jnp.maximum(jnp.abs(den), jnp.exp(-m_tok))
    n_ref[...] = w_inter * n_old + w_intra * k
    m_ref[...] = m_tok
    wv_ref[...] = w_intra * v
    wi_ref[...] = jnp.broadcast_to(w_inter, wi_ref.shape)

    head_row = lax.broadcasted_iota(jnp.int32, (N_HEADS, D_V), 0)
    qb16 = q.astype(BF16)
    qc_rows = []
    for b in range(bb):
        qc = jnp.zeros((N_HEADS, D_V), F32)
        for h in range(N_HEADS):
            c_old = c0_ref[b, h]
            all_heads = jnp.dot(qb16[b], c_old.astype(BF16), preferred_element_type=F32)
            qc = jnp.where(head_row == h, all_heads, qc)
            c_ref[b, h] = (wi_ref[b, pl.ds(h, 1), :] * c_old
                           + kt_ref[b, :, h:h + 1] * wv_ref[b, pl.ds(h, 1), :])
        qc_rows.append(qc)
    qc_all = jnp.stack(qc_rows, axis=0)
    hv = (s * v + qc_all * w_inter) * (1.0 / den)
    mu = jnp.mean(hv, axis=-1, keepdims=True)
    hc = hv - mu
    var = jnp.mean(hc * hc, axis=-1, keepdims=True)
    hn = hc * lax.rsqrt(var + LN_EPS) * hng_ref[...][None]
    out_ref[...] = _sigmoid(o_ref[...]) * hn


def mlstm_step(q, k, v, o, gates, c0, n0, m0, b_gates, hn_g, layer, bb):
    bsz = q.shape[0]
    kt = jnp.swapaxes(k, 1, 2)
    gi = gates[:, :N_HEADS, None]
    gf = gates[:, GATE_PAD:GATE_PAD + N_HEADS, None]
    m0 = m0[..., None]
    b3 = lambda i: (i, 0, 0)
    st3 = lambda i: (layer, i, 0, 0)
    bias = lambda v: v.reshape(N_HEADS, 1)
    out, c_new, n_new, m_new = pl.pallas_call(
        _mlstm_step_body,
        grid=(bsz // bb,),
        in_specs=[pl.BlockSpec((bb, N_HEADS, D_QK), b3), pl.BlockSpec((bb, N_HEADS, D_QK), b3),
                  pl.BlockSpec((bb, D_QK, N_HEADS), b3),
                  pl.BlockSpec((bb, N_HEADS, D_V), b3), pl.BlockSpec((bb, N_HEADS, D_V), b3),
                  pl.BlockSpec((bb, N_HEADS, 1), b3), pl.BlockSpec((bb, N_HEADS, 1), b3),
                  pl.BlockSpec((None, bb, N_HEADS, D_QK, D_V), lambda i: (layer, i, 0, 0, 0)),
                  pl.BlockSpec((None, bb, N_HEADS, D_QK), st3),
                  pl.BlockSpec((None, bb, N_HEADS, 1), st3),
                  _const_spec((N_HEADS, 1), (0, 0)), _const_spec((N_HEADS, 1), (0, 0)),
                  _const_spec((N_HEADS, D_V), (0, 0))],
        out_specs=[pl.BlockSpec((bb, N_HEADS, D_V), b3),
                   pl.BlockSpec((None, bb, N_HEADS, D_QK, D_V), lambda i: (0, i, 0, 0, 0)),
                   pl.BlockSpec((None, bb, N_HEADS, D_QK), lambda i: (0, i, 0, 0)),
                   pl.BlockSpec((None, bb, N_HEADS, 1), lambda i: (0, i, 0, 0))],
        out_shape=[jax.ShapeDtypeStruct((bsz, N_HEADS, D_V), F32),
                   jax.ShapeDtypeStruct((1, bsz, N_HEADS, D_QK, D_V), F32),
                   jax.ShapeDtypeStruct((1, bsz, N_HEADS, D_QK), F32),
                   jax.ShapeDtypeStruct((1, bsz, N_HEADS, 1), F32)],
        scratch_shapes=[pltpu.VMEM((bb, N_HEADS, D_V), F32), pltpu.VMEM((bb, N_HEADS, D_V), F32)],
        compiler_params=_params(("arbitrary",), 4 * bb * N_HEADS * D_QK * D_V * 4 + (8 << 20)),
        name="mlstm_step",
    )(q, k, kt, v, o, gi, gf, c0, n0, m0, bias(b_gates[0]), bias(b_gates[1]),
      hn_g.reshape(N_HEADS, D_V))
    return out, c_new, n_new, m_new[..., 0]


TM_GLU = 1024
TM_ROW = 512
TM_FFN_UP = 2048
TM_FFN_DOWN = 512
TM_QKV = 2048
TN_STRIPE = 512
TN_FFN_DOWN = 1024
TN_QKV = 1024
TC_CONV_SAMPLE = 512
BB_MLSTM_STEP = 8


def _trunk(xp, xs, pp, ps, bp, sp, states, w):
    conv_state_t, c_st, n_st, m_st, ffn_state = states
    bs = xs.shape[0]
    hq = N_HEADS * D_QK
    hv = N_HEADS * D_V
    new_ffn_p, new_ffn_s = [], []
    ffn_past = ffn_state.reshape(DEPTH, bs, (FFN_CONV_WIDTH - 1) * D_FF)

    def ffn_block(p32, pb, s32, sb, i, emit_bf16):
        hp, gtp, hs, hist_s = ffn_up(pb, sb, ffn_past, w["ff_w_gate"], w["ff_w_up"],
                                     w["ff_w_dw"], w["ff_b_dw"], i, TM_FFN_UP, TN_STRIPE, sp)
        tps = sp // TM_FFN_UP
        new_ffn_p.append(gtp.reshape(bp, tps, FFN_TAIL, D_FF)[:, tps - 1, FFN_TAIL - 2:, :])
        new_ffn_s.append(hist_s)
        fp, fs = ffn_down(hp, hs, w["ff_w_down"], w["ff_b_down"], i, TM_FFN_DOWN, TN_FFN_DOWN)
        return ple(fp, p32, pp, fs, s32, ps, w["ln_ffn_g"], w["ln_ffn_b"], w["pl_w_gate"], w["pl_w_proj"],
                   w["pl_g"], i, TM_ROW, emit_bf16)

    yp, tails, us = mm_glu_conv(xp, xs, w["cv_w_in"], w["cv_b_in"], w["cv_w_dw"], w["cv_b_dw"], 0,
                                TM_GLU, TN_STRIPE, sp)
    tps = sp // TM_GLU
    hist = CONV_WIDTH - 1
    conv_p = tails.reshape(bp, tps, CONV_HALO, D_MODEL)[:, tps - 1, CONV_HALO - hist:, :][None]
    ys, conv_s_t = conv_sample(conv_state_t, us, w["cv_w_dw"], w["cv_b_dw"], 0, TC_CONV_SAMPLE)
    conv_s = jnp.swapaxes(conv_s_t, 1, 2)
    p32, pb, s32, sb = mm_ln(yp, xp, ys, xs, w["cv_w_out"], w["cv_b_out"], w["ln_mix_g"], w["ln_mix_b"], 0, 0,
                             TM_ROW, pre=(w["cv_ln_g"], w["cv_ln_b"]))
    p32, s32, pb, sb = ffn_block(p32, pb, s32, sb, 0, True)

    qkv_p, qkv_s = mm_qkv(pb, sb, w["ml_w_in_t"], 0, 2 * hq + hv, TM_QKV, TN_QKV)
    o_p, gates_p, o_s, gates_s = mm_o_gates(pb, sb, w["ml_w_in_t"], 0, TM_ROW)
    mix_p, c_p, n_p, m_p = mlstm_chunk(qkv_p, o_p, gates_p, w["ml_b_gates"][0], w["ml_hn_g"][0], bp, sp)
    qf = qkv_s[:, :hq].astype(F32).reshape(bs, N_HEADS, D_QK)
    kf = qkv_s[:, hq:2 * hq].astype(F32).reshape(bs, N_HEADS, D_QK)
    vf = qkv_s[:, 2 * hq:].astype(F32).reshape(bs, N_HEADS, D_V)
    mix_s, c_s, n_s, m_s = mlstm_step(qf, kf, vf, o_s.reshape(bs, N_HEADS, D_V), gates_s, c_st, n_st, m_st,
                                      w["ml_b_gates"][0], w["ml_hn_g"][0], 0, BB_MLSTM_STEP)
    p32, pb, s32, sb = mm_ln(mix_p, p32, mix_s.reshape(bs, hv), s32, w["ml_w_out"], w["ml_b_out"],
                             w["ln_mix_g"], w["ln_mix_b"], 0, 1, TM_ROW)
    p32, s32 = ffn_block(p32, pb, s32, sb, 1, False)
    return (p32, s32, conv_p, conv_s, c_p[None], n_p[None], m_p[None], c_s, n_s, m_s,
            jnp.stack(new_ffn_p), jnp.swapaxes(jnp.stack(new_ffn_s), 1, 2))


def kernel(x_prompt, x_sample, p_prompt, p_sample, state_conv, state_mlstm_c, state_mlstm_n,
           state_mlstm_m, state_ffn_conv,
           cv_w_in, cv_b_in, cv_w_dw, cv_b_dw, cv_ln_g, cv_ln_b, cv_w_out, cv_b_out,
           ml_w_in, ml_b_gates, ml_hn_g, ml_w_out, ml_b_out,
           ln_mix_g, ln_mix_b, ln_ffn_g, ln_ffn_b,
           ff_w_gate, ff_w_up, ff_w_dw, ff_b_dw, ff_w_down, ff_b_down,
           pl_w_proj, pl_g, pl_w_gate):
    bp, sp, _ = x_prompt.shape
    bs, ss, _ = x_sample.shape
    w = dict(
        cv_w_in=cv_w_in, cv_b_in=cv_b_in, cv_w_dw=cv_w_dw, cv_b_dw=cv_b_dw,
        cv_ln_g=cv_ln_g, cv_ln_b=cv_ln_b, cv_w_out=cv_w_out.astype(BF16), cv_b_out=cv_b_out,
        ml_w_in_t=jnp.swapaxes(ml_w_in, 1, 2),
        ml_b_gates=ml_b_gates, ml_hn_g=ml_hn_g, ml_w_out=ml_w_out.astype(BF16), ml_b_out=ml_b_out,
        ln_mix_g=ln_mix_g, ln_mix_b=ln_mix_b, ln_ffn_g=ln_ffn_g, ln_ffn_b=ln_ffn_b,
        ff_w_gate=ff_w_gate, ff_w_up=ff_w_up, ff_w_dw=ff_w_dw,
        ff_b_dw=ff_b_dw, ff_w_down=ff_w_down, ff_b_down=ff_b_down,
        pl_w_proj=pl_w_proj.astype(BF16), pl_g=pl_g, pl_w_gate=pl_w_gate.astype(BF16),
    )
    yp, ys, conv_p, conv_s, c_p, n_p, m_p, c_s, n_s, m_s, ffn_p, ffn_s = _trunk(
        x_prompt.reshape(bp * sp, D_MODEL), x_sample.reshape(bs * ss, D_MODEL),
        p_prompt.reshape(DEPTH, bp * sp, D_PLE), p_sample.reshape(DEPTH, bs * ss, D_PLE), bp, sp,
        (jnp.swapaxes(state_conv, 1, 2), state_mlstm_c, state_mlstm_n, state_mlstm_m, state_ffn_conv), w)
    return (yp.reshape(bp, sp, D_MODEL), ys.reshape(bs, ss, D_MODEL), conv_p, conv_s,
            c_p, n_p, m_p, c_s, n_s, m_s, ffn_p, ffn_s)
```

```python
import functools

import jax
import jax.numpy as jnp
from jax import lax
from jax.experimental import pallas as pl
from jax.experimental.pallas import tpu as pltpu

D_MODEL = 2048
D_FF = 5632
D_PLE = 256
N_HEADS = 8
D_QK = 128
D_V = 256
CONV_WIDTH = 31
FFN_CONV_WIDTH = 3
DEPTH = 2
ALPHA = (2.0 * DEPTH) ** 0.25
LN_EPS = 1e-5
MLSTM_CHUNK = 128
GATE_PAD = 128
SUBLANES = 8

F32 = jnp.float32
BF16 = jnp.bfloat16

VMEM_CAP_BYTES = 58 * 1024 * 1024
VMEM_SLACK_BYTES = 8 * 1024 * 1024


def _params(sem, vmem_bytes):
    return pltpu.CompilerParams(
        dimension_semantics=sem,
        vmem_limit_bytes=min(int(vmem_bytes) + VMEM_SLACK_BYTES, VMEM_CAP_BYTES))


def _ln(x, g, b):
    mu = jnp.mean(x, axis=-1, keepdims=True)
    xc = x - mu
    var = jnp.mean(xc * xc, axis=-1, keepdims=True)
    return xc * lax.rsqrt(var + LN_EPS) * g + b


def _sigmoid(x):
    return 1.0 / (1.0 + jnp.exp(-x))


def _log_sigmoid(x):
    return jnp.minimum(x, 0.0) - jnp.log(1.0 + jnp.exp(-jnp.abs(x)))


def _row_scan(x, op, identity):
    rows, cols = x.shape
    shift = 1
    while shift < rows:
        shifted = jnp.concatenate([jnp.full((shift, cols), identity, x.dtype), x[:rows - shift, :]], axis=0)
        x = op(x, shifted)
        shift *= 2
    return x


def _const_spec(shape, index):
    return pl.BlockSpec(shape, lambda *_: index, pipeline_mode=pl.Buffered(1))


def _layer_vec(v):
    return v.reshape(v.shape[0], 1, v.shape[1])


def _vec_spec(n, layer):
    return _const_spec((None, 1, n), (layer, 0, 0))


CONV_ROWS = 64
CONV_HALO = 32
LANES = 128
GLU_SUB_ROWS = 256


def _conv_chunk(pad_ref, w_ref, b_ref, y_ref, row0, lane0):
    lanes = pl.ds(lane0, LANES)
    win = CONV_ROWS + SUBLANES
    acc = jnp.broadcast_to(b_ref[:, lanes], (CONV_ROWS, LANES))
    for r in range(SUBLANES):
        part = None
        for a in range((CONV_WIDTH - 1 - r) // SUBLANES + 1):
            d = SUBLANES * a + r
            start = row0 + CONV_HALO - SUBLANES * (a + 1)
            term = pad_ref[pl.ds(start, win), lanes] * w_ref[pl.ds(CONV_WIDTH - 1 - d, 1), lanes]
            part = term if part is None else part + term
        acc = acc + part[SUBLANES - r:SUBLANES - r + CONV_ROWS, :]
    y_ref[pl.ds(row0, CONV_ROWS), lanes] = acc


def _glu(xb, wab_ref, wgb_ref, ba_ref, bg_ref):
    a = jnp.dot(xb, wab_ref[...], preferred_element_type=F32) + ba_ref[...]
    g = jnp.dot(xb, wgb_ref[...], preferred_element_type=F32) + bg_ref[...]
    return a * _sigmoid(g)


def _mm_glu_conv_body(tiles_per_seq, x_ref, xs_ref, wa_ref, wg_ref, ba_ref, bg_ref, wdw_ref, bdw_ref,
                      y_ref, tail_ref, us_ref, wab_ref, wgb_ref, pad_ref):
    i = pl.program_id(1)
    tm, tn = y_ref.shape

    @pl.when(i == 0)
    def _():
        wab_ref[...] = wa_ref[...].astype(BF16)
        wgb_ref[...] = wg_ref[...].astype(BF16)

    @pl.when((i % tiles_per_seq) == 0)
    def _():
        pad_ref[pl.ds(0, CONV_HALO), :] = jnp.zeros((CONV_HALO, tn), F32)

    for r0 in range(0, tm, GLU_SUB_ROWS):
        xb = x_ref[pl.ds(r0, GLU_SUB_ROWS), :].astype(BF16)
        pad_ref[pl.ds(CONV_HALO + r0, GLU_SUB_ROWS), :] = _glu(xb, wab_ref, wgb_ref, ba_ref, bg_ref)
        for c0 in range(r0, r0 + GLU_SUB_ROWS, CONV_ROWS):
            for l0 in range(0, tn, LANES):
                _conv_chunk(pad_ref, wdw_ref, bdw_ref, y_ref, c0, l0)
    tail = pad_ref[pl.ds(tm, CONV_HALO), :]
    tail_ref[...] = tail
    pad_ref[pl.ds(0, CONV_HALO), :] = tail

    @pl.when(i == pl.num_programs(1) - 1)
    def _():
        us_ref[...] = _glu(xs_ref[...].astype(BF16), wab_ref, wgb_ref, ba_ref, bg_ref)


def mm_glu_conv(x, x_s, w, b, wdw, bdw, layer, tm, tn, seq_len):
    m, k = x.shape
    ms = x_s.shape[0]
    n = w.shape[2] // 2
    nj = n // tn
    b3 = _layer_vec(b)
    vmem = (2 * tm * k * 4 + ms * k * 4 + 4 * k * tn * 4 + 2 * k * tn * 2 + 2 * tm * tn * 4
            + (tm + CONV_HALO) * tn * 4 + GLU_SUB_ROWS * k * 2 + 6 * GLU_SUB_ROWS * tn * 4 + 8 * ms * tn * 4)
    return pl.pallas_call(
        functools.partial(_mm_glu_conv_body, seq_len // tm),
        grid=(nj, m // tm),
        in_specs=[pl.BlockSpec((tm, k), lambda j, i: (i, 0)),
                  _const_spec((ms, k), (0, 0)),
                  pl.BlockSpec((None, k, tn), lambda j, i: (layer, 0, j)),
                  pl.BlockSpec((None, k, tn), lambda j, i: (layer, 0, j + nj)),
                  pl.BlockSpec((None, 1, tn), lambda j, i: (layer, 0, j)),
                  pl.BlockSpec((None, 1, tn), lambda j, i: (layer, 0, j + nj)),
                  pl.BlockSpec((None, CONV_WIDTH, tn), lambda j, i: (layer, 0, j)),
                  pl.BlockSpec((None, 1, tn), lambda j, i: (layer, 0, j))],
        out_specs=[pl.BlockSpec((tm, tn), lambda j, i: (i, j)),
                   pl.BlockSpec((None, CONV_HALO, tn), lambda j, i: (i, 0, j)),
                   pl.BlockSpec((ms, tn), lambda j, i: (0, j))],
        out_shape=[jax.ShapeDtypeStruct((m, n), F32),
                   jax.ShapeDtypeStruct((m // tm, CONV_HALO, n), F32),
                   jax.ShapeDtypeStruct((ms, n), F32)],
        scratch_shapes=[pltpu.VMEM((k, tn), BF16), pltpu.VMEM((k, tn), BF16),
                        pltpu.VMEM((tm + CONV_HALO, tn), F32)],
        compiler_params=_params(("arbitrary", "arbitrary"), vmem),
        name="mm_glu_conv",
    )(x, x_s, w, w, b3, b3, wdw, _layer_vec(bdw))


def _conv_sample_body(p_ref, u_ref, w_ref, b_ref, y_ref, np_ref):
    hist = p_ref.shape[0]
    u = u_ref[...]
    acc = u * w_ref[pl.ds(hist, 1), :] + b_ref[...]
    for j in range(hist):
        acc = acc + p_ref[j] * w_ref[pl.ds(j, 1), :]
    y_ref[...] = acc
    for j in range(hist - 1):
        np_ref[j] = p_ref[j + 1]
    np_ref[hist - 1] = u


def conv_sample(past_t, u, w, b, layer, tc):
    _, hist, bsz, c = past_t.shape
    return pl.pallas_call(
        _conv_sample_body,
        grid=(c // tc,),
        in_specs=[pl.BlockSpec((None, hist, bsz, tc), lambda i: (layer, 0, 0, i)),
                  pl.BlockSpec((bsz, tc), lambda i: (0, i)),
                  pl.BlockSpec((None, CONV_WIDTH, tc), lambda i: (layer, 0, i)),
                  pl.BlockSpec((None, 1, tc), lambda i: (layer, 0, i))],
        out_specs=[pl.BlockSpec((bsz, tc), lambda i: (0, i)),
                   pl.BlockSpec((None, hist, bsz, tc), lambda i: (0, 0, 0, i))],
        out_shape=[jax.ShapeDtypeStruct((bsz, c), F32),
                   jax.ShapeDtypeStruct((1, hist, bsz, c), F32)],
        compiler_params=_params(("arbitrary",), 5 * hist * bsz * tc * 4),
        name="conv_sample",
    )(past_t, u, w, _layer_vec(b))


ROW_SUB = 256


def _mm_ln_body(pre_norm, *refs):
    if pre_norm:
        (xin_ref, res_ref, xins_ref, ress_ref, w_ref, b_ref, pg_ref, pb_ref, g_ref, be_ref,
         o_ref, ob_ref, os_ref, obs_ref) = refs
    else:
        (xin_ref, res_ref, xins_ref, ress_ref, w_ref, b_ref, g_ref, be_ref,
         o_ref, ob_ref, os_ref, obs_ref) = refs

    def rows_out(xin, res):
        if pre_norm:
            t = _ln(xin.astype(F32), pg_ref[...], pb_ref[...])
            xb = (t * _sigmoid(t)).astype(BF16)
        else:
            xb = xin.astype(BF16)
        mix = jnp.dot(xb, w_ref[...], preferred_element_type=F32) + b_ref[...]
        return _ln(ALPHA * res + mix, g_ref[...], be_ref[...])

    tm = o_ref.shape[0]
    sub = min(ROW_SUB, tm)
    for r0 in range(0, tm, sub):
        rows = pl.ds(r0, sub)
        out = rows_out(xin_ref[rows, :], res_ref[rows, :])
        o_ref[rows, :] = out
        ob_ref[rows, :] = out.astype(BF16)

    @pl.when(pl.program_id(0) == pl.num_programs(0) - 1)
    def _():
        out = rows_out(xins_ref[...], ress_ref[...])
        os_ref[...] = out
        obs_ref[...] = out.astype(BF16)


def mm_ln(xin, resid, xin_s, resid_s, w, b, g, be, mix_layer, ln_layer, tm, pre=None):
    m, k = xin.shape
    ms = xin_s.shape[0]
    n = w.shape[2]
    row = lambda i: (i, 0)
    fixed = lambda i: (0, 0)
    in_specs = [pl.BlockSpec((tm, k), row), pl.BlockSpec((tm, n), row),
                _const_spec((ms, k), (0, 0)), _const_spec((ms, n), (0, 0)),
                _const_spec((None, k, n), (mix_layer, 0, 0)), _vec_spec(n, mix_layer)]
    args = [xin, resid, xin_s, resid_s, w, _layer_vec(b)]
    if pre is not None:
        in_specs += [_vec_spec(k, mix_layer), _vec_spec(k, mix_layer)]
        args += [_layer_vec(pre[0]), _layer_vec(pre[1])]
    in_specs += [_vec_spec(n, ln_layer), _vec_spec(n, ln_layer)]
    args += [_layer_vec(g), _layer_vec(be)]
    vmem = (2 * tm * k * xin.dtype.itemsize + 4 * tm * n * 4 + 2 * tm * n * 2 + k * n * 2 + 6 * ROW_SUB * n * 4
            + 12 * ms * n * 4)
    return pl.pallas_call(
        functools.partial(_mm_ln_body, pre is not None),
        grid=(m // tm,),
        in_specs=in_specs,
        out_specs=[pl.BlockSpec((tm, n), row), pl.BlockSpec((tm, n), row),
                   pl.BlockSpec((ms, n), fixed), pl.BlockSpec((ms, n), fixed)],
        out_shape=[jax.ShapeDtypeStruct((m, n), F32), jax.ShapeDtypeStruct((m, n), BF16),
                   jax.ShapeDtypeStruct((ms, n), F32), jax.ShapeDtypeStruct((ms, n), BF16)],
        compiler_params=_params(("arbitrary",), vmem),
        name="mm_ln",
    )(*args)


FFN_TAIL = 8
FFN_SUB_ROWS = 256


def _ffn_hidden(g, g1, g2, up, wdw_ref, bdw_ref):
    gc = (wdw_ref[pl.ds(0, 1), :] * g2 + wdw_ref[pl.ds(1, 1), :] * g1 + wdw_ref[pl.ds(2, 1), :] * g
          + bdw_ref[...])
    return (gc * _sigmoid(gc) * up).astype(BF16)


def _ffn_up_body(tiles_per_seq, sub_rows, x_ref, xs_ref, p2_ref, p1_ref, wg_ref, wu_ref, wdw_ref,
                 bdw_ref, h_ref, gt_ref, hs_ref, ns_ref, wcat_ref, gs_ref):
    i = pl.program_id(1)
    tm, tn = h_ref.shape

    @pl.when(i == 0)
    def _():
        wcat_ref[:, pl.ds(0, tn)] = wg_ref[...].astype(BF16)
        wcat_ref[:, pl.ds(tn, tn)] = wu_ref[...].astype(BF16)

    @pl.when((i % tiles_per_seq) == 0)
    def _():
        gs_ref[pl.ds(0, FFN_TAIL), :] = jnp.zeros((FFN_TAIL, tn), F32)

    for r0 in range(0, tm, sub_rows):
        xr = x_ref[pl.ds(r0, sub_rows), :]
        gu = jnp.dot(xr, wcat_ref[...], preferred_element_type=F32)
        g = gu[:, :tn]
        up = gu[:, tn:]
        gs_ref[pl.ds(FFN_TAIL + r0, sub_rows), :] = g
        g1 = gs_ref[pl.ds(FFN_TAIL + r0 - 1, sub_rows), :]
        g2 = gs_ref[pl.ds(FFN_TAIL + r0 - 2, sub_rows), :]
        h_ref[pl.ds(r0, sub_rows), :] = _ffn_hidden(g, g1, g2, up, wdw_ref, bdw_ref)
    tail = gs_ref[pl.ds(tm, FFN_TAIL), :]
    gt_ref[...] = tail
    gs_ref[pl.ds(0, FFN_TAIL), :] = tail

    @pl.when(i == pl.num_programs(1) - 1)
    def _():
        xs = xs_ref[...]
        gu = jnp.dot(xs, wcat_ref[...], preferred_element_type=F32)
        g = gu[:, :tn]
        up = gu[:, tn:]
        p1 = p1_ref[...]
        ns_ref[0] = p1
        ns_ref[1] = g
        hs_ref[...] = _ffn_hidden(g, p1, p2_ref[...], up, wdw_ref, bdw_ref)


def ffn_up(xb, xb_s, past_s, wg, wu, wdw, bdw, layer, tm, tn, seq_len):
    m, k = xb.shape
    ms = xb_s.shape[0]
    dff = wg.shape[2]
    nj = dff // tn
    sub = min(FFN_SUB_ROWS, tm)
    vmem = (2 * tm * k * 2 + ms * k * 2 + 4 * k * tn * 4 + 2 * k * tn * 2 + 2 * tm * tn * 2 + tm * tn * 4
            + 8 * sub * tn * 4 + 16 * ms * tn * 4)
    return pl.pallas_call(
        functools.partial(_ffn_up_body, seq_len // tm, sub),
        grid=(nj, m // tm),
        in_specs=[pl.BlockSpec((tm, k), lambda j, i: (i, 0)),
                  _const_spec((ms, k), (0, 0)),
                  pl.BlockSpec((None, ms, tn), lambda j, i: (layer, 0, j)),
                  pl.BlockSpec((None, ms, tn), lambda j, i: (layer, 0, j + nj)),
                  pl.BlockSpec((None, k, tn), lambda j, i: (layer, 0, j)),
                  pl.BlockSpec((None, k, tn), lambda j, i: (layer, 0, j)),
                  pl.BlockSpec((None, FFN_CONV_WIDTH, tn), lambda j, i: (layer, 0, j)),
                  pl.BlockSpec((None, 1, tn), lambda j, i: (layer, 0, j))],
        out_specs=[pl.BlockSpec((tm, tn), lambda j, i: (i, j)),
                   pl.BlockSpec((None, FFN_TAIL, tn), lambda j, i: (i, 0, j)),
                   pl.BlockSpec((ms, tn), lambda j, i: (0, j)),
                   pl.BlockSpec((FFN_CONV_WIDTH - 1, ms, tn), lambda j, i: (0, 0, j))],
        out_shape=[jax.ShapeDtypeStruct((m, dff), BF16),
                   jax.ShapeDtypeStruct((m // tm, FFN_TAIL, dff), F32),
                   jax.ShapeDtypeStruct((ms, dff), BF16),
                   jax.ShapeDtypeStruct((FFN_CONV_WIDTH - 1, ms, dff), F32)],
        scratch_shapes=[pltpu.VMEM((k, 2 * tn), BF16),
                        pltpu.VMEM((tm + FFN_TAIL, tn), F32)],
        compiler_params=_params(("arbitrary", "arbitrary"), vmem),
        name="ffn_up",
    )(xb, xb_s, past_s, past_s, wg, wu, wdw, _layer_vec(bdw))


def _ffn_down_body(h_ref, hs_ref, wd_ref, bd_ref, f_ref, fs_ref, wdb_ref):
    @pl.when(pl.program_id(1) == 0)
    def _():
        wdb_ref[...] = wd_ref[...].astype(BF16)

    f_ref[...] = bd_ref[...] + jnp.dot(h_ref[...], wdb_ref[...], preferred_element_type=F32)

    @pl.when(pl.program_id(1) == pl.num_programs(1) - 1)
    def _():
        fs_ref[...] = bd_ref[...] + jnp.dot(hs_ref[...], wdb_ref[...], preferred_element_type=F32)


def ffn_down(h, h_s, wd, bd, layer, tm, tn):
    m, dff = h.shape
    ms = h_s.shape[0]
    n = wd.shape[2]
    vmem = 2 * tm * dff * 2 + ms * dff * 2 + dff * tn * 4 + dff * tn * 2 + 4 * tm * tn * 4 + 4 * ms * tn * 4
    return pl.pallas_call(
        _ffn_down_body,
        grid=(n // tn, m // tm),
        in_specs=[pl.BlockSpec((tm, dff), lambda j, i: (i, 0)),
                  _const_spec((ms, dff), (0, 0)),
                  pl.BlockSpec((None, dff, tn), lambda j, i: (layer, 0, j), pipeline_mode=pl.Buffered(1)),
                  pl.BlockSpec((None, 1, tn), lambda j, i: (layer, 0, j))],
        out_specs=[pl.BlockSpec((tm, tn), lambda j, i: (i, j)),
                   pl.BlockSpec((ms, tn), lambda j, i: (0, j))],
        out_shape=[jax.ShapeDtypeStruct((m, n), F32), jax.ShapeDtypeStruct((ms, n), F32)],
        scratch_shapes=[pltpu.VMEM((dff, tn), BF16)],
        compiler_params=_params(("arbitrary", "arbitrary"), vmem),
        name="ffn_down",
    )(h, h_s, wd, _layer_vec(bd))


def _ple_body(emit_bf16, f_ref, x_ref, p_ref, fs_ref, xs_ref, ps_ref, lg_ref, lb_ref, wg_ref, wp_ref, g_ref,
              o_ref, os_ref, *bf16_refs):
    def rows_out(f, x, p):
        xn = _ln(ALPHA * x + f, lg_ref[...], lb_ref[...])
        gate = _sigmoid(jnp.dot(xn.astype(BF16), wg_ref[...], preferred_element_type=F32))
        e = jnp.dot(p.astype(BF16), wp_ref[...], preferred_element_type=F32)
        e = e * lax.rsqrt(jnp.mean(e * e, axis=-1, keepdims=True) + LN_EPS) * g_ref[...]
        return xn + gate * e

    tm = o_ref.shape[0]
    sub = min(ROW_SUB, tm)
    for r0 in range(0, tm, sub):
        rows = pl.ds(r0, sub)
        out = rows_out(f_ref[rows, :], x_ref[rows, :], p_ref[rows, :])
        o_ref[rows, :] = out
        if emit_bf16:
            bf16_refs[0][rows, :] = out.astype(BF16)

    @pl.when(pl.program_id(0) == pl.num_programs(0) - 1)
    def _():
        out = rows_out(fs_ref[...], xs_ref[...], ps_ref[...])
        os_ref[...] = out
        if emit_bf16:
            bf16_refs[1][...] = out.astype(BF16)


def ple(f, x, p, f_s, x_s, p_s, lg, lb, wg, wp, g, layer, tm, emit_bf16):
    m, k = f.shape
    ms = f_s.shape[0]
    dp = p.shape[2]
    row = lambda i: (i, 0)
    fixed = lambda i: (0, 0)
    out_specs = [pl.BlockSpec((tm, k), row), pl.BlockSpec((ms, k), fixed)]
    out_shape = [jax.ShapeDtypeStruct((m, k), F32), jax.ShapeDtypeStruct((ms, k), F32)]
    if emit_bf16:
        out_specs += [pl.BlockSpec((tm, k), row), pl.BlockSpec((ms, k), fixed)]
        out_shape += [jax.ShapeDtypeStruct((m, k), BF16), jax.ShapeDtypeStruct((ms, k), BF16)]
    vmem = (6 * tm * k * 4 + 2 * tm * k * 2 + 2 * tm * dp * 4 + k * k * 2 + dp * k * 2 + 6 * ROW_SUB * k * 4
            + 12 * ms * k * 4)
    return pl.pallas_call(
        functools.partial(_ple_body, emit_bf16),
        grid=(m // tm,),
        in_specs=[pl.BlockSpec((tm, k), row), pl.BlockSpec((tm, k), row),
                  pl.BlockSpec((None, tm, dp), lambda i: (layer, i, 0)),
                  _const_spec((ms, k), (0, 0)), _const_spec((ms, k), (0, 0)),
                  _const_spec((None, ms, dp), (layer, 0, 0)),
                  _vec_spec(k, layer), _vec_spec(k, layer),
                  _const_spec((None, k, k), (layer, 0, 0)),
                  _const_spec((None, dp, k), (layer, 0, 0)),
                  _vec_spec(k, layer)],
        out_specs=out_specs,
        out_shape=out_shape,
        compiler_params=_params(("arbitrary",), vmem),
        name="ple",
    )(f, x, p, f_s, x_s, p_s, _layer_vec(lg), _layer_vec(lb), wg, wp, _layer_vec(g))


WT_CHUNK = 512


def _stage_transposed(wt_ref, wb_ref):
    n = wt_ref.shape[0]
    chunk = min(WT_CHUNK, n)
    for c0 in range(0, n, chunk):
        wb_ref[:, pl.ds(c0, chunk)] = wt_ref[pl.ds(c0, chunk), :].T.astype(BF16)


def _mm_qkv_body(q_tiles, x_ref, xs_ref, wt_ref, o_ref, os_ref, wb_ref):
    @pl.when(pl.program_id(1) == 0)
    def _():
        _stage_transposed(wt_ref, wb_ref)

    scale = jnp.where(pl.program_id(0) < q_tiles, D_QK ** -0.5, 1.0).astype(F32)
    z = jnp.dot(x_ref[...], wb_ref[...], preferred_element_type=F32)
    o_ref[...] = (z * scale).astype(BF16)

    @pl.when(pl.program_id(1) == pl.num_programs(1) - 1)
    def _():
        zs = jnp.dot(xs_ref[...], wb_ref[...], preferred_element_type=F32)
        os_ref[...] = (zs * scale).astype(BF16)


def mm_qkv(xb, xb_s, wt, layer, n, tm, tn):
    m, k = xb.shape
    ms = xb_s.shape[0]
    vmem = (2 * tm * k * 2 + ms * k * 2 + 2 * k * tn * 4 + k * tn * 2 + 2 * tm * tn * 2 + 3 * tm * tn * 4
            + WT_CHUNK * k * 8 + 6 * ms * tn * 4)
    return pl.pallas_call(
        functools.partial(_mm_qkv_body, (N_HEADS * D_QK) // tn),
        grid=(n // tn, m // tm),
        in_specs=[pl.BlockSpec((tm, k), lambda j, i: (i, 0)),
                  _const_spec((ms, k), (0, 0)),
                  pl.BlockSpec((None, tn, k), lambda j, i: (layer, j, 0))],
        out_specs=[pl.BlockSpec((tm, tn), lambda j, i: (i, j)),
                   pl.BlockSpec((ms, tn), lambda j, i: (0, j))],
        out_shape=[jax.ShapeDtypeStruct((m, n), BF16), jax.ShapeDtypeStruct((ms, n), BF16)],
        scratch_shapes=[pltpu.VMEM((k, tn), BF16)],
        compiler_params=_params(("arbitrary", "arbitrary"), vmem),
        name="mm_qkv",
    )(xb, xb_s, wt)


def _mm_o_gates_body(x_ref, xs_ref, wot_ref, wgt_ref, o_ref, gt_ref, os_ref, gts_ref, wob_ref, wgb_ref):
    @pl.when(pl.program_id(0) == 0)
    def _():
        _stage_transposed(wot_ref, wob_ref)
        wg = wgt_ref[...].T
        lane = lax.broadcasted_iota(jnp.int32, wg.shape, 1)
        first = lane < N_HEADS
        wgb_ref[:, pl.ds(0, GATE_PAD)] = jnp.where(first, wg, 0.0).astype(BF16)
        wgb_ref[:, pl.ds(GATE_PAD, GATE_PAD)] = jnp.where(
            first, pltpu.roll(wg, GATE_PAD - N_HEADS, axis=1), 0.0).astype(BF16)

    xb = x_ref[...]
    o_ref[...] = jnp.dot(xb, wob_ref[...], preferred_element_type=F32)
    gt_ref[...] = jnp.dot(xb, wgb_ref[...], preferred_element_type=F32)

    @pl.when(pl.program_id(0) == pl.num_programs(0) - 1)
    def _():
        xs = xs_ref[...]
        os_ref[...] = jnp.dot(xs, wob_ref[...], preferred_element_type=F32)
        gts_ref[...] = jnp.dot(xs, wgb_ref[...], preferred_element_type=F32)


def mm_o_gates(xb, xb_s, wt, layer, tm):
    m, k = xb.shape
    ms = xb_s.shape[0]
    n = N_HEADS * D_V
    o_row = 2 * N_HEADS * D_QK + N_HEADS * D_V
    row = lambda i: (i, 0)
    vmem = (2 * tm * k * 2 + ms * k * 2 + k * n * 4 + k * n * 2 + GATE_PAD * k * 6
            + 2 * (tm + ms) * (n + 2 * GATE_PAD) * 4 + 2 * tm * n * 4 + WT_CHUNK * k * 8)
    return pl.pallas_call(
        _mm_o_gates_body,
        grid=(m // tm,),
        in_specs=[pl.BlockSpec((tm, k), row),
                  _const_spec((ms, k), (0, 0)),
                  _const_spec((None, n, k), (layer, o_row // n, 0)),
                  _const_spec((None, GATE_PAD, k), (layer, (o_row + n) // GATE_PAD, 0))],
        out_specs=[pl.BlockSpec((tm, n), row), pl.BlockSpec((tm, 2 * GATE_PAD), row),
                   pl.BlockSpec((ms, n), lambda i: (0, 0)), pl.BlockSpec((ms, 2 * GATE_PAD), lambda i: (0, 0))],
        out_shape=[jax.ShapeDtypeStruct((m, n), F32), jax.ShapeDtypeStruct((m, 2 * GATE_PAD), F32),
                   jax.ShapeDtypeStruct((ms, n), F32), jax.ShapeDtypeStruct((ms, 2 * GATE_PAD), F32)],
        scratch_shapes=[pltpu.VMEM((k, n), BF16), pltpu.VMEM((k, 2 * GATE_PAD), BF16)],
        compiler_params=_params(("arbitrary",), vmem),
        name="mm_o_gates",
    )(xb, xb_s, wt, wt)


def _mlstm_chunk_body(q_ref, k_ref, v_ref, o_ref, gi_ref, gf_ref, gr_ref, bi_ref, bf_ref, bcol_ref,
                      hng_ref, out_ref, c_ref, m_ref, s_ref, tot_ref):
    length = q_ref.shape[0]

    @pl.when(pl.program_id(1) == 0)
    def _():
        c_ref[...] = jnp.zeros(c_ref.shape, F32)
        m_ref[...] = jnp.zeros(m_ref.shape, F32)

    row = lax.broadcasted_iota(jnp.int32, (length, length), 0)
    col = lax.broadcasted_iota(jnp.int32, (length, length), 1)
    causal = row >= col
    triu = (row <= col).astype(F32)

    li_col = gi_ref[...] + bi_ref[...]
    bh_col = _row_scan(_log_sigmoid(gf_ref[...] + bf_ref[...]), jnp.add, 0.0)
    cm_col = _row_scan(li_col - bh_col, jnp.maximum, -jnp.inf)
    grow = gr_ref[...] + bcol_ref[...]
    li_row = grow[0:N_HEADS, :]
    bh_row = jnp.dot(_log_sigmoid(grow[N_HEADS:2 * N_HEADS, :]), triu,
                     precision=lax.Precision.HIGHEST, preferred_element_type=F32)
    a_row = li_row - bh_row
    inter = bh_col + m_ref[0:1, :]
    m_tok = jnp.maximum(inter, bh_col + cm_col)
    w_inter = jnp.exp(inter - m_tok)
    emt = jnp.exp(-m_tok)
    e_col = bh_col - m_tok
    m_new = m_tok[length - 1:length, :]
    g_state = w_inter[length - 1:length, :]
    g_tok = jnp.exp(bh_col[length - 1:length, :] - bh_col + li_col - m_new)
    m_ref[...] = jnp.broadcast_to(m_new, m_ref.shape)
    ones = jnp.ones((length, LANES), BF16)

    for h in range(N_HEADS):
        qk = slice(h * D_QK, (h + 1) * D_QK)
        w_intra = jnp.exp(jnp.where(causal, e_col[:, h:h + 1] + a_row[h:h + 1, :], -jnp.inf))
        s = lax.dot_general(q_ref[:, qk], k_ref[:, qk], (((1,), (1,)), ((), ())),
                            preferred_element_type=F32) * w_intra
        s_ref[h] = s.astype(BF16)

    for h in range(N_HEADS):
        qk = slice(h * D_QK, (h + 1) * D_QK)
        vaug = jnp.concatenate([v_ref[:, h * D_V:(h + 1) * D_V], ones], axis=1)
        c_old = c_ref[h]
        tot_ref[h] = (jnp.dot(s_ref[h], vaug, preferred_element_type=F32)
                      + jnp.dot(q_ref[:, qk], c_old.astype(BF16), preferred_element_type=F32)
                      * w_inter[:, h:h + 1])
        kg = (k_ref[:, qk].astype(F32) * g_tok[:, h:h + 1]).astype(BF16)
        c_ref[h] = g_state[:, h:h + 1] * c_old + lax.dot_general(
            kg, vaug, (((0,), (0,)), ((), ())), preferred_element_type=F32)

    for h in range(N_HEADS):
        tot = tot_ref[h]
        den = jnp.maximum(jnp.abs(tot[:, D_V:D_V + LANES]), emt[:, h:h + 1])
        inv = 1.0 / den
        halves = [tot[:, c0:c0 + LANES] * inv for c0 in range(0, D_V, LANES)]
        moments = jnp.concatenate([sum(halves), sum(x * x for x in halves)], axis=0).astype(BF16)
        stat = jnp.dot(moments, jnp.ones((LANES, LANES), BF16), preferred_element_type=F32) * (1.0 / D_V)
        mu = stat[0:length, :]
        rstd = lax.rsqrt(stat[length:2 * length, :] - mu * mu + LN_EPS)
        for idx, x in enumerate(halves):
            cols = pl.ds(h * D_V + idx * LANES, LANES)
            out_ref[:, cols] = (_sigmoid(o_ref[:, cols]) * ((x - mu) * rstd) * hng_ref[:, cols]).astype(BF16)


FORGET_PAD_BIAS = 30.0


def mlstm_chunk(qkv, o, gates, b_gates, hn_g, bsz, seq_len):
    length = MLSTM_CHUNK
    nch = seq_len // length
    hq = N_HEADS * D_QK
    hv = N_HEADS * D_V
    rows = lambda b, c: (b * nch + c, 0)
    pad = GATE_PAD - N_HEADS
    bi_row = jnp.pad(b_gates[0].reshape(1, N_HEADS), ((0, 0), (0, pad)))
    bf_row = jnp.pad(b_gates[1].reshape(1, N_HEADS), ((0, 0), (0, pad)), constant_values=FORGET_PAD_BIAS)
    bcol = b_gates.reshape(2 * N_HEADS, 1)
    gates_t = jnp.concatenate([gates[:, :N_HEADS], gates[:, GATE_PAD:GATE_PAD + N_HEADS]], axis=1).T
    const2 = lambda b, c: (0, 0)
    out, c_aug, m_new = pl.pallas_call(
        _mlstm_chunk_body,
        grid=(bsz, nch),
        in_specs=[pl.BlockSpec((length, hq), rows),
                  pl.BlockSpec((length, hq), lambda b, c: (b * nch + c, 1)),
                  pl.BlockSpec((length, hv), lambda b, c: (b * nch + c, 1)),
                  pl.BlockSpec((length, hv), rows),
                  pl.BlockSpec((length, GATE_PAD), rows),
                  pl.BlockSpec((length, GATE_PAD), lambda b, c: (b * nch + c, 1)),
                  pl.BlockSpec((2 * N_HEADS, length), lambda b, c: (0, b * nch + c)),
                  pl.BlockSpec((1, GATE_PAD), const2),
                  pl.BlockSpec((1, GATE_PAD), const2),
                  pl.BlockSpec((2 * N_HEADS, 1), const2),
                  pl.BlockSpec((1, hv), const2)],
        out_specs=[pl.BlockSpec((length, hv), rows),
                   pl.BlockSpec((None, N_HEADS, D_QK, D_V + LANES), lambda b, c: (b, 0, 0, 0)),
                   pl.BlockSpec((None, SUBLANES, GATE_PAD), lambda b, c: (b, 0, 0))],
        out_shape=[jax.ShapeDtypeStruct((bsz * seq_len, hv), BF16),
                   jax.ShapeDtypeStruct((bsz, N_HEADS, D_QK, D_V + LANES), F32),
                   jax.ShapeDtypeStruct((bsz, SUBLANES, GATE_PAD), F32)],
        scratch_shapes=[pltpu.VMEM((N_HEADS, length, length), BF16),
                        pltpu.VMEM((N_HEADS, length, D_V + LANES), F32)],
        compiler_params=_params(("arbitrary", "arbitrary"), 24 << 20),
        name="mlstm_chunk",
    )(qkv, qkv, qkv, o, gates, gates, gates_t, bi_row, bf_row, bcol, hn_g.reshape(1, hv))
    return out, c_aug[..., :D_V], c_aug[..., D_V], m_new[:, 0, :N_HEADS]


def _mlstm_step_body(q_ref, k_ref, kt_ref, v_ref, o_ref, gi_ref, gf_ref, c0_ref, n0_ref, m0_ref,
                     bi_ref, bf_ref, hng_ref, out_ref, c_ref, n_ref, m_ref, wv_ref, wi_ref):
    bb = q_ref.shape[0]
    li = gi_ref[...] + bi_ref[...]
    inter = _log_sigmoid(gf_ref[...] + bf_ref[...]) + m0_ref[...]
    m_tok = jnp.maximum(inter, li)
    w_intra = jnp.exp(li - m_tok)
    w_inter = jnp.exp(inter - m_tok)
    q = q_ref[...]
    k = k_ref[...]
    n_old = n0_ref[...]
    v = v_ref[...]
    s = jnp.sum(q * k, axis=-1, keepdims=True) * w_intra
    den = s + jnp.sum(q * n_old, axis=-1, keepdims=True) * w_inter
    den = jnp.maximum(jnp.abs(den), jnp.exp(-m_tok))
    n_ref[...] = w_inter * n_old + w_intra * k
    m_ref[...] = m_tok
    wv_ref[...] = w_intra * v
    wi_ref[...] = jnp.broadcast_to(w_inter, wi_ref.shape)

    head_row = lax.broadcasted_iota(jnp.int32, (N_HEADS, D_V), 0)
    qb16 = q.astype(BF16)
    qc_rows = []
    for b in range(bb):
        qc = jnp.zeros((N_HEADS, D_V), F32)
        for h in range(N_HEADS):
            c_old = c0_ref[b, h]
            all_heads = jnp.dot(qb16[b], c_old.astype(BF16), preferred_element_type=F32)
            qc = jnp.where(head_row == h, all_heads, qc)
            c_ref[b, h] = (wi_ref[b, pl.ds(h, 1), :] * c_old
                           + kt_ref[b, :, h:h + 1] * wv_ref[b, pl.ds(h, 1), :])
        qc_rows.append(qc)
    qc_all = jnp.stack(qc_rows, axis=0)
    hv = (s * v + qc_all * w_inter) * (1.0 / den)
    mu = jnp.mean(hv, axis=-1, keepdims=True)
    hc = hv - mu
    var = jnp.mean(hc * hc, axis=-1, keepdims=True)
    hn = hc * lax.rsqrt(var + LN_EPS) * hng_ref[...][None]
    out_ref[...] = _sigmoid(o_ref[...]) * hn


def mlstm_step(q, k, v, o, gates, c0, n0, m0, b_gates, hn_g, layer, bb):
    bsz = q.shape[0]
    kt = jnp.swapaxes(k, 1, 2)
    gi = gates[:, :N_HEADS, None]
    gf = gates[:, GATE_PAD:GATE_PAD + N_HEADS, None]
    m0 = m0[..., None]
    b3 = lambda i: (i, 0, 0)
    st3 = lambda i: (layer, i, 0, 0)
    bias = lambda v: v.reshape(N_HEADS, 1)
    out, c_new, n_new, m_new = pl.pallas_call(
        _mlstm_step_body,
        grid=(bsz // bb,),
        in_specs=[pl.BlockSpec((bb, N_HEADS, D_QK), b3), pl.BlockSpec((bb, N_HEADS, D_QK), b3),
                  pl.BlockSpec((bb, D_QK, N_HEADS), b3),
                  pl.BlockSpec((bb, N_HEADS, D_V), b3), pl.BlockSpec((bb, N_HEADS, D_V), b3),
                  pl.BlockSpec((bb, N_HEADS, 1), b3), pl.BlockSpec((bb, N_HEADS, 1), b3),
                  pl.BlockSpec((None, bb, N_HEADS, D_QK, D_V), lambda i: (layer, i, 0, 0, 0)),
                  pl.BlockSpec((None, bb, N_HEADS, D_QK), st3),
                  pl.BlockSpec((None, bb, N_HEADS, 1), st3),
                  _const_spec((N_HEADS, 1), (0, 0)), _const_spec((N_HEADS, 1), (0, 0)),
                  _const_spec((N_HEADS, D_V), (0, 0))],
        out_specs=[pl.BlockSpec((bb, N_HEADS, D_V), b3),
                   pl.BlockSpec((None, bb, N_HEADS, D_QK, D_V), lambda i: (0, i, 0, 0, 0)),
                   pl.BlockSpec((None, bb, N_HEADS, D_QK), lambda i: (0, i, 0, 0)),
                   pl.BlockSpec((None, bb, N_HEADS, 1), lambda i: (0, i, 0, 0))],
        out_shape=[jax.ShapeDtypeStruct((bsz, N_HEADS, D_V), F32),
                   jax.ShapeDtypeStruct((1, bsz, N_HEADS, D_QK, D_V), F32),
                   jax.ShapeDtypeStruct((1, bsz, N_HEADS, D_QK), F32),
                   jax.ShapeDtypeStruct((1, bsz, N_HEADS, 1), F32)],
        scratch_shapes=[pltpu.VMEM((bb, N_HEADS, D_V), F32), pltpu.VMEM((bb, N_HEADS, D_V), F32)],
        compiler_params=_params(("arbitrary",), 4 * bb * N_HEADS * D_QK * D_V * 4 + (8 << 20)),
        name="mlstm_step",
    )(q, k, kt, v, o, gi, gf, c0, n0, m0, bias(b_gates[0]), bias(b_gates[1]),
      hn_g.reshape(N_HEADS, D_V))
    return out, c_new, n_new, m_new[..., 0]


TM_GLU = 1024
TM_ROW = 512
TM_FFN_UP = 2048
TM_FFN_DOWN = 512
TM_QKV = 2048
TN_STRIPE = 512
TN_FFN_DOWN = 1024
TN_QKV = 1024
TC_CONV_SAMPLE = 256
BB_MLSTM_STEP = 8


def _trunk(xp, xs, pp, ps, bp, sp, states, w):
    conv_state_t, c_st, n_st, m_st, ffn_state = states
    bs = xs.shape[0]
    hq = N_HEADS * D_QK
    hv = N_HEADS * D_V
    new_ffn_p, new_ffn_s = [], []
    ffn_past = ffn_state.reshape(DEPTH, bs, (FFN_CONV_WIDTH - 1) * D_FF)

    def ffn_block(p32, pb, s32, sb, i, emit_bf16):
        hp, gtp, hs, hist_s = ffn_up(pb, sb, ffn_past, w["ff_w_gate"], w["ff_w_up"],
                                     w["ff_w_dw"], w["ff_b_dw"], i, TM_FFN_UP, TN_STRIPE, sp)
        tps = sp // TM_FFN_UP
        new_ffn_p.append(gtp.reshape(bp, tps, FFN_TAIL, D_FF)[:, tps - 1, FFN_TAIL - 2:, :])
        new_ffn_s.append(hist_s)
        fp, fs = ffn_down(hp, hs, w["ff_w_down"], w["ff_b_down"], i, TM_FFN_DOWN, TN_FFN_DOWN)
        return ple(fp, p32, pp, fs, s32, ps, w["ln_ffn_g"], w["ln_ffn_b"], w["pl_w_gate"], w["pl_w_proj"],
                   w["pl_g"], i, TM_ROW, emit_bf16)

    yp, tails, us = mm_glu_conv(xp, xs, w["cv_w_in"], w["cv_b_in"], w["cv_w_dw"], w["cv_b_dw"], 0,
                                TM_GLU, TN_STRIPE, sp)
    tps = sp // TM_GLU
    hist = CONV_WIDTH - 1
    conv_p = tails.reshape(bp, tps, CONV_HALO, D_MODEL)[:, tps - 1, CONV_HALO - hist:, :][None]
    ys, conv_s_t = conv_sample(conv_state_t, us, w["cv_w_dw"], w["cv_b_dw"], 0, TC_CONV_SAMPLE)
    conv_s = jnp.swapaxes(conv_s_t, 1, 2)
    p32, pb, s32, sb = mm_ln(yp, xp, ys, xs, w["cv_w_out"], w["cv_b_out"], w["ln_mix_g"], w["ln_mix_b"], 0, 0,
                             TM_ROW, pre=(w["cv_ln_g"], w["cv_ln_b"]))
    p32, s32, pb, sb = ffn_block(p32, pb, s32, sb, 0, True)

    qkv_p, qkv_s = mm_qkv(pb, sb, w["ml_w_in_t"], 0, 2 * hq + hv, TM_QKV, TN_QKV)
    o_p, gates_p, o_s, gates_s = mm_o_gates(pb, sb, w["ml_w_in_t"], 0, TM_ROW)
    mix_p, c_p, n_p, m_p = mlstm_chunk(qkv_p, o_p, gates_p, w["ml_b_gates"][0], w["ml_hn_g"][0], bp, sp)
    qf = qkv_s[:, :hq].astype(F32).reshape(bs, N_HEADS, D_QK)
    kf = qkv_s[:, hq:2 * hq].astype(F32).reshape(bs, N_HEADS, D_QK)
    vf = qkv_s[:, 2 * hq:].astype(F32).reshape(bs, N_HEADS, D_V)
    mix_s, c_s, n_s, m_s = mlstm_step(qf, kf, vf, o_s.reshape(bs, N_HEADS, D_V), gates_s, c_st, n_st, m_st,
                                      w["ml_b_gates"][0], w["ml_hn_g"][0], 0, BB_MLSTM_STEP)
    p32, pb, s32, sb = mm_ln(mix_p, p32, mix_s.reshape(bs, hv), s32, w["ml_w_out"], w["ml_b_out"],
                             w["ln_mix_g"], w["ln_mix_b"], 0, 1, TM_ROW)
    p32, s32 = ffn_block(p32, pb, s32, sb, 1, False)
    return (p32, s32, conv_p, conv_s, c_p[None], n_p[None], m_p[None], c_s, n_s, m_s,
            jnp.stack(new_ffn_p), jnp.swapaxes(jnp.stack(new_ffn_s), 1, 2))


def kernel(x_prompt, x_sample, p_prompt, p_sample, state_conv, state_mlstm_c, state_mlstm_n,
           state_mlstm_m, state_ffn_conv,
           cv_w_in, cv_b_in, cv_w_dw, cv_b_dw, cv_ln_g, cv_ln_b, cv_w_out, cv_b_out,
           ml_w_in, ml_b_gates, ml_hn_g, ml_w_out, ml_b_out,
           ln_mix_g, ln_mix_b, ln_ffn_g, ln_ffn_b,
           ff_w_gate, ff_w_up, ff_w_dw, ff_b_dw, ff_w_down, ff_b_down,
           pl_w_proj, pl_g, pl_w_gate):
    bp, sp, _ = x_prompt.shape
    bs, ss, _ = x_sample.shape
    w = dict(
        cv_w_in=cv_w_in, cv_b_in=cv_b_in, cv_w_dw=cv_w_dw, cv_b_dw=cv_b_dw,
        cv_ln_g=cv_ln_g, cv_ln_b=cv_ln_b, cv_w_out=cv_w_out.astype(BF16), cv_b_out=cv_b_out,
        ml_w_in_t=jnp.swapaxes(ml_w_in, 1, 2),
        ml_b_gates=ml_b_gates, ml_hn_g=ml_hn_g, ml_w_out=ml_w_out.astype(BF16), ml_b_out=ml_b_out,
        ln_mix_g=ln_mix_g, ln_mix_b=ln_mix_b, ln_ffn_g=ln_ffn_g, ln_ffn_b=ln_ffn_b,
        ff_w_gate=ff_w_gate, ff_w_up=ff_w_up, ff_w_dw=ff_w_dw,
        ff_b_dw=ff_b_dw, ff_w_down=ff_w_down, ff_b_down=ff_b_down,
        pl_w_proj=pl_w_proj.astype(BF16), pl_g=pl_g, pl_w_gate=pl_w_gate.astype(BF16),
    )
    yp, ys, conv_p, conv_s, c_p, n_p, m_p, c_s, n_s, m_s, ffn_p, ffn_s = _trunk(
        x_prompt.reshape(bp * sp, D_MODEL), x_sample.reshape(bs * ss, D_MODEL),
        p_prompt.reshape(DEPTH, bp * sp, D_PLE), p_sample.reshape(DEPTH, bs * ss, D_PLE), bp, sp,
        (jnp.swapaxes(state_conv, 1, 2), state_mlstm_c, state_mlstm_n, state_mlstm_m, state_ffn_conv), w)
    return (yp.reshape(bp, sp, D_MODEL), ys.reshape(bs, ss, D_MODEL), conv_p, conv_s,
            c_p, n_p, m_p, c_s, n_s, m_s, ffn_p, ffn_s)
```
